```python
import math
import jax, jax.numpy as jnp
from jax import lax
import numpy as np

D_MODEL = 1024
BATCH = 8
SEQ = 2048
DEPTH = 1

GRID_W = 64
CTX_LEN = 256
N_HEADS = 8
N_KV_HEADS = 2
HEAD_DIM = 64
GQA_REP = N_HEADS // N_KV_HEADS
ATTN_WIDTH = N_HEADS * HEAD_DIM
KV_WIDTH = N_KV_HEADS * HEAD_DIM
ATTN_SCALE = HEAD_DIM ** -0.5
Q_BLOCK = 128
ROPE_THETA = 10000.0
ROPE_FREQS = HEAD_DIM // 4
S5_WIDTH = 512
S5_GROUP = 16
S5_GROUPS = S5_WIDTH // S5_GROUP
S5_STATE = 64
DT_MIN = 1e-3
DT_MAX = 1e-1
EPS = 1e-6
IN_SIZES = (ATTN_WIDTH, KV_WIDTH, KV_WIDTH, ATTN_WIDTH, S5_WIDTH, S5_WIDTH, D_MODEL, D_MODEL)
IN_WIDTH = sum(IN_SIZES)

kernel_name = "hybrid_gqa_s5_prefix_dit_block"


def _rmsnorm(x, g):
    xf = x.astype(jnp.float32)
    y = xf * lax.rsqrt(jnp.mean(xf * xf, axis=-1, keepdims=True) + EPS)
    return (y * g.astype(jnp.float32)).astype(x.dtype)


def _adaln(cond, w, b):
    m = jax.nn.silu(cond) @ w + b
    return jnp.split(m, 3, axis=-1)


def _project(hn, w_in, q_g, k_g):
    p = hn @ w_in
    offsets = np.cumsum(IN_SIZES)[:-1].tolist()
    q, k, v, ga, u, gb, ma, mb = jnp.split(p, offsets, axis=-1)
    bsz, n = hn.shape[:2]
    q = _rmsnorm(q.reshape(bsz, n, N_HEADS, HEAD_DIM), q_g)
    k = _rmsnorm(k.reshape(bsz, n, N_KV_HEADS, HEAD_DIM), k_g)
    v = v.reshape(bsz, n, N_KV_HEADS, HEAD_DIM)
    return q, k, v, ga, u, gb, ma, mb


def _axial_angles(n):
    rows = n // GRID_W
    row_ids = jnp.repeat(jnp.arange(rows, dtype=jnp.float32), GRID_W)
    col_ids = jnp.tile(jnp.arange(GRID_W, dtype=jnp.float32), rows)
    freqs = ROPE_THETA ** (-jnp.arange(ROPE_FREQS, dtype=jnp.float32) / ROPE_FREQS)
    return row_ids[:, None] * freqs, col_ids[:, None] * freqs


def _rope_1d(x, ang):
    cos = jnp.cos(ang)[:, None, :].astype(x.dtype)
    sin = jnp.sin(ang)[:, None, :].astype(x.dtype)
    x1, x2 = jnp.split(x, 2, axis=-1)
    return jnp.concatenate([x1 * cos - x2 * sin, x2 * cos + x1 * sin], axis=-1)


def _axial_rope(x, ang_row, ang_col):
    x_row, x_col = jnp.split(x, 2, axis=-1)
    return jnp.concatenate([_rope_1d(x_row, ang_row), _rope_1d(x_col, ang_col)], axis=-1)


def _attend(q5, k, v):
    s = jnp.einsum('bqgrd,bkgd->bgrqk', q5, k).astype(jnp.float32) * ATTN_SCALE
    p = jax.nn.softmax(s, axis=-1).astype(v.dtype)
    return jnp.einsum('bgrqk,bkgd->bqgrd', p, v)


def _latent_attention(q, k_lat, v_lat, k_ctx, v_ctx):
    bsz, n = q.shape[:2]
    ang_row, ang_col = _axial_angles(n)
    q = _axial_rope(q, ang_row, ang_col)
    k_lat = _axial_rope(k_lat, ang_row, ang_col)
    k_all = jnp.concatenate([k_ctx, k_lat], axis=1)
    v_all = jnp.concatenate([v_ctx, v_lat], axis=1)
    nb = n // Q_BLOCK
    qb = q.reshape(bsz, nb, Q_BLOCK, N_KV_HEADS, GQA_REP, HEAD_DIM).transpose(1, 0, 2, 3, 4, 5)
    o = lax.map(lambda blk: _attend(blk, k_all, v_all), qb)
    return o.transpose(1, 0, 2, 3, 4, 5).reshape(bsz, n, ATTN_WIDTH)


def _context_attention(q, k, v):
    bsz, n = q.shape[:2]
    q5 = q.reshape(bsz, n, N_KV_HEADS, GQA_REP, HEAD_DIM)
    return _attend(q5, k, v).reshape(bsz, n, ATTN_WIDTH)


def _s5_discretize(lam_re, lam_im, log_dt, b_re, b_im):
    lam = lax.complex(jnp.minimum(lam_re.astype(jnp.float32), -1e-4), lam_im.astype(jnp.float32))
    dt = jnp.exp(log_dt.astype(jnp.float32))[:, None]
    lam_bar = jnp.exp(lam * dt)
    b = lax.complex(b_re.astype(jnp.float32), b_im.astype(jnp.float32))
    b_bar = ((lam_bar - 1.0) / lam)[..., None] * b
    return lam_bar, b_bar


def _ssm_combine(e1, e2):
    a1, b1 = e1
    a2, b2 = e2
    return a1 * a2, a2 * b1 + b2


def _s5_scan(u, lam_bar, b_bar, reverse, h0=None):
    bu = jnp.einsum('blgh,gph->blgp', u.astype(jnp.complex64), b_bar)
    a = jnp.broadcast_to(lam_bar, bu.shape)
    a_cum, h = lax.associative_scan(_ssm_combine, (a, bu), axis=1, reverse=reverse)
    if h0 is not None:
        h = h + a_cum * h0[:, None]
    return h


def _half_glu(y, w, b):
    z = jax.nn.gelu(y)
    return z * jax.nn.sigmoid(z @ w + b)


def _s5_branch(u_lat, u_ctx, lam_re, lam_im, log_dt, b_re, b_im, c_re, c_im, d_skip, w_glu, b_glu, with_ctx):
    bsz, n = u_lat.shape[:2]
    n_ctx = u_ctx.shape[1]
    ul = u_lat.astype(jnp.float32)
    uc = u_ctx.astype(jnp.float32)
    ul4 = ul.reshape(bsz, n, S5_GROUPS, S5_GROUP)
    uc4 = uc.reshape(bsz, n_ctx, S5_GROUPS, S5_GROUP)
    d = d_skip.astype(jnp.float32)
    y_lat = d * ul
    y_ctx = d * uc
    for direction in range(2):
        rev = direction == 1
        lam_bar, b_bar = _s5_discretize(lam_re[direction], lam_im[direction], log_dt[direction],
                                        b_re[direction], b_im[direction])
        cm = lax.complex(c_re[direction].astype(jnp.float32), c_im[direction].astype(jnp.float32))
        h_ctx = _s5_scan(uc4, lam_bar, b_bar, rev)
        h_end = h_ctx[:, 0] if rev else h_ctx[:, -1]
        h_lat = _s5_scan(ul4, lam_bar, b_bar, rev, h_end)
        y_lat = y_lat + jnp.einsum('blgp,ghp->blgh', h_lat, cm).real.reshape(bsz, n, S5_WIDTH)
        if with_ctx:
            y_ctx = y_ctx + jnp.einsum('blgp,ghp->blgh', h_ctx, cm).real.reshape(bsz, n_ctx, S5_WIDTH)
    out_lat = _half_glu(y_lat, w_glu, b_glu).astype(u_lat.dtype)
    out_ctx = _half_glu(y_ctx, w_glu, b_glu).astype(u_ctx.dtype) if with_ctx else None
    return out_lat, out_ctx


def _merge(y_attn, g_attn, y_s5, g_s5, m_attn, m_s5, w_br_a, w_br_b, w_out):
    ya = (y_attn * jax.nn.silu(g_attn)) @ w_br_a
    yb = (y_s5 * jax.nn.silu(g_s5)) @ w_br_b
    return (jax.nn.sigmoid(m_attn) * ya + jax.nn.sigmoid(m_s5) * yb) @ w_out


def setup_inputs(seed: int = 0) -> dict:
    key = jax.random.key(seed)
    ks = jax.random.split(key, 24)
    f32 = jnp.float32
    nrm = lambda k, shape, s: jax.random.normal(k, shape, f32) * s
    G, P, H = S5_GROUPS, S5_STATE, S5_GROUP
    lam_im_init = jnp.pi * jnp.arange(P, dtype=f32)
    return {
        "x": nrm(ks[0], (BATCH, SEQ, D_MODEL), 1.0),
        "c": nrm(ks[1], (BATCH, D_MODEL), 1.0),
        "ctx": nrm(ks[2], (BATCH, CTX_LEN, D_MODEL), 1.0),
        "c_ctx": nrm(ks[3], (D_MODEL,), 1.0),
        "norm_g": 1.0 + nrm(ks[4], (DEPTH, D_MODEL), 0.02),
        "w_ada": nrm(ks[5], (DEPTH, D_MODEL, 3 * D_MODEL), 0.5 * D_MODEL ** -0.5),
        "b_ada": nrm(ks[6], (DEPTH, 3 * D_MODEL), 0.02),
        "w_in": nrm(ks[7], (DEPTH, D_MODEL, IN_WIDTH), D_MODEL ** -0.5),
        "q_norm_g": 1.0 + nrm(ks[8], (DEPTH, HEAD_DIM), 0.02),
        "k_norm_g": 1.0 + nrm(ks[9], (DEPTH, HEAD_DIM), 0.02),
        "s5_lam_re": -0.5 + nrm(ks[10], (DEPTH, 2, G, P), 0.01),
        "s5_lam_im": lam_im_init + nrm(ks[11], (DEPTH, 2, G, P), 0.01),
        "s5_log_dt": jax.random.uniform(ks[12], (DEPTH, 2, G), f32, math.log(DT_MIN), math.log(DT_MAX)),
        "s5_b_re": nrm(ks[13], (DEPTH, 2, G, P, H), (2 * H) ** -0.5),
        "s5_b_im": nrm(ks[14], (DEPTH, 2, G, P, H), (2 * H) ** -0.5),
        "s5_c_re": nrm(ks[15], (DEPTH, 2, G, H, P), (2 * P) ** -0.5),
        "s5_c_im": nrm(ks[16], (DEPTH, 2, G, H, P), (2 * P) ** -0.5),
        "s5_d": nrm(ks[17], (DEPTH, S5_WIDTH), 0.5),
        "w_glu": nrm(ks[18], (DEPTH, S5_WIDTH, S5_WIDTH), S5_WIDTH ** -0.5),
        "b_glu": nrm(ks[19], (DEPTH, S5_WIDTH), 0.02),
        "w_branch_attn": nrm(ks[20], (DEPTH, ATTN_WIDTH, D_MODEL), ATTN_WIDTH ** -0.5),
        "w_branch_s5": nrm(ks[21], (DEPTH, S5_WIDTH, D_MODEL), S5_WIDTH ** -0.5),
        "w_out": nrm(ks[22], (DEPTH, D_MODEL, D_MODEL), D_MODEL ** -0.5),
        "final_norm_g": 1.0 + nrm(ks[23], (D_MODEL,), 0.02),
    }


def reference(x, c, ctx, c_ctx, norm_g, w_ada, b_ada, w_in, q_norm_g, k_norm_g,
              s5_lam_re, s5_lam_im, s5_log_dt, s5_b_re, s5_b_im, s5_c_re, s5_c_im, s5_d,
              w_glu, b_glu, w_branch_attn, w_branch_s5, w_out, final_norm_g):
    h = x
    hc = ctx
    for layer in range(DEPTH):
        with_ctx = layer + 1 < DEPTH
        shift, scale, gate = _adaln(c, w_ada[layer], b_ada[layer])
        shift_c, scale_c, gate_c = _adaln(c_ctx, w_ada[layer], b_ada[layer])
        xn = _rmsnorm(h, norm_g[layer]) * (1.0 + scale[:, None]) + shift[:, None]
        cn = _rmsnorm(hc, norm_g[layer]) * (1.0 + scale_c) + shift_c
        q, k, v, ga, u, gb, ma, mb = _project(xn, w_in[layer], q_norm_g[layer], k_norm_g[layer])
        qc, kc, vc, gac, uc, gbc, mac, mbc = _project(cn, w_in[layer], q_norm_g[layer], k_norm_g[layer])
        y_attn = _latent_attention(q, k, v, kc, vc)
        y_s5, y_s5_c = _s5_branch(u, uc, s5_lam_re[layer], s5_lam_im[layer], s5_log_dt[layer],
                                  s5_b_re[layer], s5_b_im[layer], s5_c_re[layer], s5_c_im[layer],
                                  s5_d[layer], w_glu[layer], b_glu[layer], with_ctx)
        h = h + gate[:, None] * _merge(y_attn, ga, y_s5, gb, ma, mb,
                                       w_branch_attn[layer], w_branch_s5[layer], w_out[layer])
        if with_ctx:
            y_attn_c = _context_attention(qc, kc, vc)
            hc = hc + gate_c * _merge(y_attn_c, gac, y_s5_c, gbc, mac, mbc,
                                      w_branch_attn[layer], w_branch_s5[layer], w_out[layer])
    return _rmsnorm(h, final_norm_g)
```

```python
import functools

import jax
import jax.numpy as jnp
from jax import lax
from jax.experimental import pallas as pl
from jax.experimental.pallas import tpu as pltpu

D_MODEL = 1024
GRID_W = 64
N_HEADS = 8
N_KV_HEADS = 2
HEAD_DIM = 64
GQA_REP = N_HEADS // N_KV_HEADS
ATTN_WIDTH = N_HEADS * HEAD_DIM
KV_WIDTH = N_KV_HEADS * HEAD_DIM
ATTN_SCALE = HEAD_DIM ** -0.5
ROPE_THETA = 10000.0
ROPE_FREQS = HEAD_DIM // 4
S5_WIDTH = 512
S5_GROUP = 16
S5_GROUPS = S5_WIDTH // S5_GROUP
S5_STATE = 64
EPS = 1e-6
IN_SIZES = (ATTN_WIDTH, KV_WIDTH, KV_WIDTH, ATTN_WIDTH, S5_WIDTH, S5_WIDTH, D_MODEL, D_MODEL)

LANES = 128
SUBLANES = 8
S5_CHUNK = 16
S5_CW = S5_CHUNK * S5_GROUP
REP_W = GQA_REP * HEAD_DIM
VMEM_LIMIT = 56 * 1024 * 1024

F32 = jnp.float32
BF16 = jnp.bfloat16


def _silu(t):
    return t * jax.nn.sigmoid(t)


def _modulated_norm(x, mod_ref, ng_ref):
    ms = jnp.mean(x * x, axis=-1, keepdims=True)
    y = x * lax.rsqrt(ms + EPS) * ng_ref[...]
    return y * (1.0 + mod_ref[:, D_MODEL:2 * D_MODEL]) + mod_ref[:, 0:D_MODEL]


def _head_rmsnorm(t, ones_ref, g_ref):
    ss = jnp.dot((t * t).astype(BF16), ones_ref[...], preferred_element_type=F32)
    return t * lax.rsqrt(ss * (1.0 / HEAD_DIM) + EPS) * g_ref[...]


def _rope(t, cos, sin_signed):
    rows = t.shape[0]
    lane = lax.broadcasted_iota(jnp.int32, (rows, LANES), 1)
    first = (lane & ROPE_FREQS) == 0
    outs = []
    for j in range(t.shape[1] // LANES):
        blk = t[:, j * LANES:(j + 1) * LANES]
        partner = jnp.where(first, pltpu.roll(blk, LANES - ROPE_FREQS, 1), pltpu.roll(blk, ROPE_FREQS, 1))
        outs.append(blk * cos + partner * sin_signed)
    return outs[0] if len(outs) == 1 else jnp.concatenate(outs, axis=1)


def _ada_kernel(c_ref, w_ref, b_ref, o_ref):
    s = _silu(c_ref[...])
    o_ref[...] = jnp.dot(s, w_ref[...], preferred_element_type=F32,
                         precision=lax.Precision.HIGHEST) + b_ref[...]


def _ada_call(cc, w, b):
    rows, n = cc.shape[0], w.shape[1]
    tn = 512
    return pl.pallas_call(
        _ada_kernel,
        out_shape=jax.ShapeDtypeStruct((rows, n), F32),
        grid=(n // tn,),
        in_specs=[pl.BlockSpec((rows, D_MODEL), lambda j: (0, 0)),
                  pl.BlockSpec((D_MODEL, tn), lambda j: (0, j)),
                  pl.BlockSpec((1, tn), lambda j: (0, j))],
        out_specs=pl.BlockSpec((rows, tn), lambda j: (0, j)),
        compiler_params=pltpu.CompilerParams(dimension_semantics=("arbitrary",)),
        name="ada",
    )(cc, w, b)


def _pre_lat_kernel(x_ref, mod_ref, ng_ref, w_ref, onesq_ref, onesk_ref, qg_ref, kg_ref, cos_ref, sin_ref,
                    q_out, k_out, v_out, u_out):
    xn = _modulated_norm(x_ref[...], mod_ref, ng_ref).astype(BF16)
    p = jnp.dot(xn, w_ref[...], preferred_element_type=F32)
    o1, o2, o3 = ATTN_WIDTH, ATTN_WIDTH + KV_WIDTH, ATTN_WIDTH + 2 * KV_WIDTH
    cos, sin = cos_ref[...], sin_ref[...]
    q = _rope(_head_rmsnorm(p[:, 0:o1], onesq_ref, qg_ref), cos, sin)
    k = _rope(_head_rmsnorm(p[:, o1:o2], onesk_ref, kg_ref), cos, sin)
    q_out[...] = (q * ATTN_SCALE).astype(BF16)
    k_out[...] = k.astype(BF16)
    v_out[...] = p[:, o2:o3].astype(BF16)
    u_out[...] = p[:, o3:].astype(BF16)


def _pre_ctx_kernel(x_ref, mod_ref, ng_ref, w_ref, onesk_ref, kg_ref, k_out, v_out, u_out):
    xn = _modulated_norm(x_ref[...], mod_ref, ng_ref).astype(BF16)
    p = jnp.dot(xn, w_ref[...], preferred_element_type=F32)
    k_out[...] = _head_rmsnorm(p[:, 0:KV_WIDTH], onesk_ref, kg_ref).astype(BF16)
    v_out[...] = p[:, KV_WIDTH:2 * KV_WIDTH].astype(BF16)
    u_out[...] = p[:, 2 * KV_WIDTH:].astype(BF16)


def _const_spec(shape):
    return pl.BlockSpec(shape, lambda b, i: (0,) * len(shape))


def _pre_lat_call(x, mod3, ng, w, onesq, onesk, qg, kg, cos, sin, tm):
    bsz, n, _ = x.shape
    wn = w.shape[1]
    tok = lambda width: pl.BlockSpec((None, tm, width), lambda b, i: (b, i, 0))
    return pl.pallas_call(
        _pre_lat_kernel,
        out_shape=(jax.ShapeDtypeStruct((bsz, n, ATTN_WIDTH), BF16),
                   jax.ShapeDtypeStruct((bsz, n, KV_WIDTH), BF16),
                   jax.ShapeDtypeStruct((bsz, n, KV_WIDTH), BF16),
                   jax.ShapeDtypeStruct((bsz, n, S5_WIDTH), BF16)),
        grid=(bsz, n // tm),
        in_specs=[tok(D_MODEL),
                  pl.BlockSpec((None, 1, 3 * D_MODEL), lambda b, i: (b, 0, 0)),
                  _const_spec((1, D_MODEL)),
                  _const_spec((D_MODEL, wn)),
                  _const_spec((ATTN_WIDTH, ATTN_WIDTH)),
                  _const_spec((KV_WIDTH, KV_WIDTH)),
                  _const_spec((1, ATTN_WIDTH)),
                  _const_spec((1, KV_WIDTH)),
                  pl.BlockSpec((tm, LANES), lambda b, i: (i, 0)),
                  pl.BlockSpec((tm, LANES), lambda b, i: (i, 0))],
        out_specs=(tok(ATTN_WIDTH), tok(KV_WIDTH), tok(KV_WIDTH), tok(S5_WIDTH)),
        compiler_params=pltpu.CompilerParams(dimension_semantics=("arbitrary", "arbitrary"),
                                             vmem_limit_bytes=VMEM_LIMIT),
        name="pre_lat",
    )(x, mod3, ng, w, onesq, onesk, qg, kg, cos, sin)


def _pre_ctx_call(ctx, mod_ctx, ng, w, onesk, kg):
    bsz, n, _ = ctx.shape
    wn = w.shape[1]
    tok = lambda width: pl.BlockSpec((None, n, width), lambda b, i: (b, 0, 0))
    return pl.pallas_call(
        _pre_ctx_kernel,
        out_shape=(jax.ShapeDtypeStruct((bsz, n, KV_WIDTH), BF16),
                   jax.ShapeDtypeStruct((bsz, n, KV_WIDTH), BF16),
                   jax.ShapeDtypeStruct((bsz, n, S5_WIDTH), BF16)),
        grid=(bsz, 1),
        in_specs=[tok(D_MODEL),
                  _const_spec((1, 3 * D_MODEL)),
                  _const_spec((1, D_MODEL)),
                  _const_spec((D_MODEL, wn)),
                  _const_spec((KV_WIDTH, KV_WIDTH)),
                  _const_spec((1, KV_WIDTH))],
        out_specs=(tok(KV_WIDTH), tok(KV_WIDTH), tok(S5_WIDTH)),
        compiler_params=pltpu.CompilerParams(dimension_semantics=("arbitrary", "arbitrary"),
                                             vmem_limit_bytes=VMEM_LIMIT),
        name="pre_ctx",
    )(ctx, mod_ctx, ng, w, onesk, kg)


def _attn_kernel(q_ref, k_ref, v_ref, e_ref, o_ref, k4t_ref, v4_ref):
    @pl.when(pl.program_id(1) == 0)
    def _():
        k4 = jnp.dot(k_ref[...], e_ref[...], preferred_element_type=F32)
        k4t = k4.T
        v4 = jnp.dot(v_ref[...], e_ref[...], preferred_element_type=F32)
        for g in range(N_KV_HEADS):
            k4t_ref[g] = k4t[g * REP_W:(g + 1) * REP_W].astype(BF16)
            v4_ref[g] = v4[:, g * REP_W:(g + 1) * REP_W].astype(BF16)

    tq = q_ref.shape[0]
    lane = lax.broadcasted_iota(jnp.int32, (tq, REP_W), 1)
    for g in range(N_KV_HEADS):
        qg = q_ref[:, g * REP_W:(g + 1) * REP_W]
        acc = jnp.zeros((tq, REP_W), F32)
        for r in range(GQA_REP):
            in_head = (lane >= r * HEAD_DIM) & (lane < (r + 1) * HEAD_DIM)
            qr = jnp.where(in_head, qg, jnp.zeros_like(qg))
            s = jnp.dot(qr, k4t_ref[g], preferred_element_type=F32)
            m = jnp.max(s, axis=1, keepdims=True)
            p = jnp.exp(s - m)
            l = jnp.sum(p, axis=1, keepdims=True)
            o = jnp.dot(p.astype(BF16), v4_ref[g], preferred_element_type=F32)
            acc = jnp.where(in_head, o / l, acc)
        o_ref[:, g * REP_W:(g + 1) * REP_W] = acc.astype(o_ref.dtype)


def _attn_call(q, k_all, v_all, expand, tq):
    bsz, n, _ = q.shape
    nk = k_all.shape[1]
    return pl.pallas_call(
        _attn_kernel,
        out_shape=jax.ShapeDtypeStruct((bsz, n, ATTN_WIDTH), BF16),
        grid=(bsz, n // tq),
        in_specs=[pl.BlockSpec((None, tq, ATTN_WIDTH), lambda b, i: (b, i, 0)),
                  pl.BlockSpec((None, nk, KV_WIDTH), lambda b, i: (b, 0, 0)),
                  pl.BlockSpec((None, nk, KV_WIDTH), lambda b, i: (b, 0, 0)),
                  _const_spec((KV_WIDTH, N_KV_HEADS * REP_W))],
        out_specs=pl.BlockSpec((None, tq, ATTN_WIDTH), lambda b, i: (b, i, 0)),
        scratch_shapes=[pltpu.VMEM((N_KV_HEADS, REP_W, nk), BF16),
                        pltpu.VMEM((N_KV_HEADS, nk, REP_W), BF16)],
        compiler_params=pltpu.CompilerParams(dimension_semantics=("arbitrary", "arbitrary"),
                                             vmem_limit_bytes=VMEM_LIMIT),
        name="attn",
    )(q, k_all, v_all, expand)


def _s5_kernel(u_ref, m_ref, sin_ref, g_ref, a_ref, y_ref, ss_ref, hin_ref, *, n_ctx_chunks, n_chunks):
    half = S5_STATE
    ss_ref[...] = jnp.dot(u_ref[...], sin_ref[...], preferred_element_type=F32)
    a_re = jnp.broadcast_to(a_ref[0:1, :], (SUBLANES, LANES))
    a_im = jnp.broadcast_to(a_ref[1:2, :], (SUBLANES, LANES))
    is_fwd = lax.broadcasted_iota(jnp.int32, (SUBLANES, LANES), 1) < half

    def step(k, carry):
        h_re, h_im = carry
        pos_b = jnp.where(k < n_ctx_chunks, n_ctx_chunks - 1 - k, n_chunks + n_ctx_chunks - 1 - k)
        rf = pl.multiple_of(k * SUBLANES, SUBLANES)
        rb = pl.multiple_of(pos_b * SUBLANES, SUBLANES)
        hin_ref[pl.ds(rf, SUBLANES), 0:half] = h_re[:, 0:half]
        hin_ref[pl.ds(rf, SUBLANES), 2 * half:3 * half] = h_im[:, 0:half]
        hin_ref[pl.ds(rb, SUBLANES), half:2 * half] = h_re[:, half:]
        hin_ref[pl.ds(rb, SUBLANES), 3 * half:] = h_im[:, half:]
        xf = ss_ref[pl.ds(rf, SUBLANES), :]
        xb = ss_ref[pl.ds(rb, SUBLANES), :]
        x_re = jnp.where(is_fwd, xf[:, 0:LANES], xb[:, 0:LANES])
        x_im = jnp.where(is_fwd, xf[:, LANES:], xb[:, LANES:])
        return (a_re * h_re - a_im * h_im + x_re, a_re * h_im + a_im * h_re + x_im)

    zero = jnp.zeros((SUBLANES, LANES), F32)
    lax.fori_loop(0, n_chunks, step, (zero, zero))

    r0 = n_ctx_chunks * SUBLANES
    y = jnp.dot(u_ref[r0:, :], m_ref[...], preferred_element_type=F32)
    y = y + jnp.dot(hin_ref[r0:, :].astype(BF16), g_ref[...], preferred_element_type=F32)
    y_ref[...] = y.astype(y_ref.dtype)


def _s5_call(ug, m, sin, gmat, a16, n_ctx_chunks):
    groups, rows, _ = ug.shape
    n_chunks = rows // SUBLANES
    out_rows = rows - n_ctx_chunks * SUBLANES
    mat = lambda: pl.BlockSpec((None, S5_CW, S5_CW), lambda g: (g, 0, 0))
    return pl.pallas_call(
        functools.partial(_s5_kernel, n_ctx_chunks=n_ctx_chunks, n_chunks=n_chunks),
        out_shape=jax.ShapeDtypeStruct((groups, out_rows, S5_CW), BF16),
        grid=(groups,),
        in_specs=[pl.BlockSpec((None, rows, S5_CW), lambda g: (g, 0, 0)),
                  mat(), mat(), mat(),
                  pl.BlockSpec((None, 2, LANES), lambda g: (g, 0, 0))],
        out_specs=pl.BlockSpec((None, out_rows, S5_CW), lambda g: (g, 0, 0)),
        scratch_shapes=[pltpu.VMEM((rows, S5_CW), F32), pltpu.VMEM((rows, S5_CW), F32)],
        compiler_params=pltpu.CompilerParams(dimension_semantics=("arbitrary",),
                                             vmem_limit_bytes=VMEM_LIMIT),
        name="s5",
    )(ug, m, sin, gmat, a16)


def _s5_operators(lam_re, lam_im, log_dt, b_re, b_im, c_re, c_im, d_skip):
    hi = lax.Precision.HIGHEST
    T, H = S5_CHUNK, S5_GROUP
    lam = lax.complex(jnp.minimum(lam_re.astype(F32), -1e-4), lam_im.astype(F32))
    dt = jnp.exp(log_dt.astype(F32))[..., None]
    z = lam * dt
    taus = jnp.arange(T + 1, dtype=F32)
    pw = jnp.exp(z[None] * taus[:, None, None, None])
    b = lax.complex(b_re.astype(F32), b_im.astype(F32))
    b_bar = ((pw[1] - 1.0) / lam)[..., None] * b
    cm = lax.complex(c_re.astype(F32), c_im.astype(F32))
    kern = jnp.einsum('dgkp,tdgp,dgph->tdgkh', cm, pw[:T], b_bar, precision=hi).real
    s_idx = jnp.arange(T)[:, None]
    t_idx = jnp.arange(T)[None, :]
    lag_f = t_idx - s_idx
    mf = jnp.where((lag_f >= 0)[:, :, None, None, None], kern[jnp.clip(lag_f, 0, T - 1), 0], 0.0)
    mb = jnp.where((lag_f <= 0)[:, :, None, None, None], kern[jnp.clip(-lag_f, 0, T - 1), 1], 0.0)
    m = (mf + mb).transpose(2, 0, 4, 1, 3)
    skip = d_skip.astype(F32).reshape(S5_GROUPS, H)
    eye_t = jnp.eye(T, dtype=F32)
    eye_h = jnp.eye(H, dtype=F32)
    m = m + skip[:, None, :, None, None] * eye_t[None, :, None, :, None] * eye_h[None, None, :, None, :]
    m = m.reshape(S5_GROUPS, S5_CW, S5_CW)

    inc_f = pw[T - 1 - jnp.arange(T), 0][..., None] * b_bar[0][None]
    inc_b = pw[jnp.arange(T), 1][..., None] * b_bar[1][None]
    to_rows = lambda w: w.transpose(1, 0, 3, 2).reshape(S5_GROUPS, S5_CW, S5_STATE)
    sin = jnp.concatenate([to_rows(inc_f.real), to_rows(inc_b.real),
                           to_rows(inc_f.imag), to_rows(inc_b.imag)], axis=-1)

    out_f = cm[0][None] * pw[1 + jnp.arange(T), 0][:, :, None, :]
    out_b = cm[1][None] * pw[T - jnp.arange(T), 1][:, :, None, :]
    to_cols = lambda w: w.transpose(1, 3, 0, 2).reshape(S5_GROUPS, S5_STATE, S5_CW)
    gmat = jnp.concatenate([to_cols(out_f.real), to_cols(out_b.real),
                            to_cols(-out_f.imag), to_cols(-out_b.imag)], axis=1)

    a = pw[T]
    a16 = jnp.stack([jnp.concatenate([a[0].real, a[1].real], axis=-1),
                     jnp.concatenate([a[0].imag, a[1].imag], axis=-1)], axis=1)
    return m.astype(BF16), sin.astype(BF16), gmat.astype(BF16), a16


def _final_kernel(x_ref, mod_ref, ng_ref, wg_ref, ya_ref, ys_ref, wglu_ref, bglu_ref, wa_ref, wb_ref, wo_ref,
                  fg_ref, o_ref):
    x = x_ref[...]
    xn = _modulated_norm(x, mod_ref, ng_ref).astype(BF16)
    gates = jnp.dot(xn, wg_ref[...], preferred_element_type=F32)
    o1, o2, o3 = ATTN_WIDTH, ATTN_WIDTH + S5_WIDTH, ATTN_WIDTH + S5_WIDTH + D_MODEL
    y = ys_ref[...].astype(F32)
    z = y * (0.5 * (1.0 + jnp.tanh(0.7978845608028654 * (y + 0.044715 * (y * y * y)))))
    zz = z * jax.nn.sigmoid(jnp.dot(z.astype(BF16), wglu_ref[...], preferred_element_type=F32) + bglu_ref[...])
    ta = (ya_ref[...].astype(F32) * _silu(gates[:, 0:o1])).astype(BF16)
    tb = (zz * _silu(gates[:, o1:o2])).astype(BF16)
    pa = jnp.dot(ta, wa_ref[...], preferred_element_type=F32)
    pb = jnp.dot(tb, wb_ref[...], preferred_element_type=F32)
    mix = jax.nn.sigmoid(gates[:, o2:o3]) * pa + jax.nn.sigmoid(gates[:, o3:]) * pb
    o = jnp.dot(mix.astype(BF16), wo_ref[...], preferred_element_type=F32)
    h = x + mod_ref[:, 2 * D_MODEL:] * o
    ms = jnp.mean(h * h, axis=-1, keepdims=True)
    o_ref[...] = h * lax.rsqrt(ms + EPS) * fg_ref[...]


def _final_call(x, mod3, ng, wg, ya, ys, wglu, bglu, wa, wb, wo, fg, tm):
    bsz, n, _ = x.shape
    tok = lambda width: pl.BlockSpec((None, tm, width), lambda b, i: (b, i, 0))
    return pl.pallas_call(
        _final_kernel,
        out_shape=jax.ShapeDtypeStruct((bsz, n, D_MODEL), F32),
        grid=(bsz, n // tm),
        in_specs=[tok(D_MODEL),
                  pl.BlockSpec((None, 1, 3 * D_MODEL), lambda b, i: (b, 0, 0)),
                  _const_spec((1, D_MODEL)),
                  _const_spec(wg.shape),
                  tok(ATTN_WIDTH), tok(S5_WIDTH),
                  _const_spec(wglu.shape), _const_spec((1, S5_WIDTH)),
                  _const_spec(wa.shape), _const_spec(wb.shape), _const_spec(wo.shape),
                  _const_spec((1, D_MODEL))],
        out_specs=tok(D_MODEL),
        compiler_params=pltpu.CompilerParams(dimension_semantics=("arbitrary", "arbitrary"),
                                             vmem_limit_bytes=VMEM_LIMIT),
        name="final",
    )(x, mod3, ng, wg, ya, ys, wglu, bglu, wa, wb, wo, fg)


def _rope_tables(n):
    rows = n // GRID_W
    row_ids = jnp.repeat(jnp.arange(rows, dtype=F32), GRID_W)
    col_ids = jnp.tile(jnp.arange(GRID_W, dtype=F32), rows)
    freqs = ROPE_THETA ** (-jnp.arange(ROPE_FREQS, dtype=F32) / ROPE_FREQS)
    ang_r, ang_c = row_ids[:, None] * freqs, col_ids[:, None] * freqs
    cos = jnp.concatenate([jnp.cos(ang_r)] * 2 + [jnp.cos(ang_c)] * 2, axis=1)
    sin = jnp.concatenate([-jnp.sin(ang_r), jnp.sin(ang_r), -jnp.sin(ang_c), jnp.sin(ang_c)], axis=1)
    reps = LANES // HEAD_DIM
    return jnp.tile(cos, (1, reps)), jnp.tile(sin, (1, reps))


def _block_ones(width):
    idx = jnp.arange(width) // HEAD_DIM
    return (idx[:, None] == idx[None, :]).astype(BF16)


def kernel(x, c, ctx, c_ctx, norm_g, w_ada, b_ada, w_in, q_norm_g, k_norm_g, s5_lam_re, s5_lam_im, s5_log_dt,
           s5_b_re, s5_b_im, s5_c_re, s5_c_im, s5_d, w_glu, b_glu, w_branch_attn, w_branch_s5, w_out,
           final_norm_g):
    assert w_in.shape[0] == 1, "single-layer block"
    bsz, n, _ = x.shape
    n_ctx = ctx.shape[1]
    assert n % S5_CHUNK == 0 and n_ctx % S5_CHUNK == 0 and bsz == SUBLANES

    ada_rows = 2 * SUBLANES
    cc = jnp.concatenate([c, c_ctx[None], jnp.zeros((ada_rows - bsz - 1, D_MODEL), F32)], axis=0)
    mod = _ada_call(cc, w_ada[0], b_ada[0][None])
    mod3 = mod.reshape(ada_rows, 1, 3 * D_MODEL)
    mod_ctx = mod[bsz:bsz + 1]

    offs = [0]
    for s in IN_SIZES:
        offs.append(offs[-1] + s)
    w_bf = w_in[0].astype(BF16)
    w_qkv = w_bf[:, offs[0]:offs[3]]
    w_u = w_bf[:, offs[4]:offs[5]]
    w_pre = jnp.concatenate([w_qkv, w_u], axis=1)
    w_pre_ctx = w_pre[:, ATTN_WIDTH:]
    w_gates = jnp.concatenate([w_bf[:, offs[3]:offs[4]], w_bf[:, offs[5]:]], axis=1)

    ng = norm_g[0][None]
    qg = jnp.tile(q_norm_g[0], N_HEADS)[None]
    kg = jnp.tile(k_norm_g[0], N_KV_HEADS)[None]
    onesq, onesk = _block_ones(ATTN_WIDTH), _block_ones(KV_WIDTH)
    cos, sin = _rope_tables(n)

    q, k_lat, v_lat, u_lat = _pre_lat_call(x, mod3, ng, w_pre, onesq, onesk, qg, kg, cos, sin, tm=512)
    k_ctx, v_ctx, u_ctx = _pre_ctx_call(ctx, mod_ctx, ng, w_pre_ctx, onesk, kg)

    k_all = jnp.concatenate([k_ctx, k_lat], axis=1)
    v_all = jnp.concatenate([v_ctx, v_lat], axis=1)
    col = jnp.arange(N_KV_HEADS * REP_W)
    src = (col // REP_W) * HEAD_DIM + col % HEAD_DIM
    expand = (jnp.arange(KV_WIDTH)[:, None] == src[None, :]).astype(BF16)
    y_attn = _attn_call(q, k_all, v_all, expand, tq=256)

    n_ctx_chunks = n_ctx // S5_CHUNK
    n_chunks = (n_ctx + n) // S5_CHUNK
    u_all = jnp.concatenate([u_ctx, u_lat], axis=1)
    ug = u_all.reshape(bsz, n_chunks, S5_CHUNK, S5_GROUPS, S5_GROUP).transpose(3, 1, 0, 2, 4)
    ug = ug.reshape(S5_GROUPS, n_chunks * bsz, S5_CW)
    m, s_in, gmat, a16 = _s5_operators(s5_lam_re[0], s5_lam_im[0], s5_log_dt[0], s5_b_re[0], s5_b_im[0],
                                       s5_c_re[0], s5_c_im[0], s5_d[0])
    yg = _s5_call(ug, m, s_in, gmat, a16, n_ctx_chunks)
    y_s5 = yg.reshape(S5_GROUPS, n // S5_CHUNK, bsz, S5_CHUNK, S5_GROUP).transpose(2, 1, 3, 0, 4)
    y_s5 = y_s5.reshape(bsz, n, S5_WIDTH)

    return _final_call(x, mod3, ng, w_gates, y_attn, y_s5, w_glu[0].astype(BF16), b_glu[0][None],
                       w_branch_attn[0].astype(BF16), w_branch_s5[0].astype(BF16), w_out[0].astype(BF16),
                       final_norm_g[None], tm=256)
```

```python
import functools

import jax
import jax.numpy as jnp
from jax import lax
from jax.experimental import pallas as pl
from jax.experimental.pallas import tpu as pltpu

D_MODEL = 1024
GRID_W = 64
N_HEADS = 8
N_KV_HEADS = 2
HEAD_DIM = 64
GQA_REP = N_HEADS // N_KV_HEADS
ATTN_WIDTH = N_HEADS * HEAD_DIM
KV_WIDTH = N_KV_HEADS * HEAD_DIM
ATTN_SCALE = HEAD_DIM ** -0.5
ROPE_THETA = 10000.0
ROPE_FREQS = HEAD_DIM // 4
S5_WIDTH = 512
S5_GROUP = 16
S5_GROUPS = S5_WIDTH // S5_GROUP
S5_STATE = 64
EPS = 1e-6
IN_SIZES = (ATTN_WIDTH, KV_WIDTH, KV_WIDTH, ATTN_WIDTH, S5_WIDTH, S5_WIDTH, D_MODEL, D_MODEL)

LANES = 128
SUBLANES = 8
S5_CHUNK = 16
S5_CW = S5_CHUNK * S5_GROUP
GROUPS_PER_BLOCK = LANES // S5_GROUP
N_BLOCKS = S5_WIDTH // LANES
BLOCK_W = S5_CHUNK * LANES
TOK_TILE = 64
REP_W = GQA_REP * HEAD_DIM
VMEM_LIMIT = 56 * 1024 * 1024

F32 = jnp.float32
BF16 = jnp.bfloat16


def _silu(t):
    return t * jax.nn.sigmoid(t)


def _modulated_norm(x3, mod_ref, ng_ref):
    ms = jnp.mean(x3 * x3, axis=-1, keepdims=True)
    y = x3 * lax.rsqrt(ms + EPS) * ng_ref[...]
    return y * (1.0 + mod_ref[:, :, D_MODEL:2 * D_MODEL]) + mod_ref[:, :, 0:D_MODEL]


def _head_rmsnorm(t, ones_ref, g_ref):
    ss = jnp.dot((t * t).astype(BF16), ones_ref[...], preferred_element_type=F32)
    return t * lax.rsqrt(ss * (1.0 / HEAD_DIM) + EPS) * g_ref[...]


def _rope(t, cos, sin_signed):
    rows = t.shape[0]
    lane = lax.broadcasted_iota(jnp.int32, (rows, LANES), 1)
    first = (lane & ROPE_FREQS) == 0
    outs = []
    for j in range(t.shape[1] // LANES):
        blk = t[:, j * LANES:(j + 1) * LANES]
        partner = jnp.where(first, pltpu.roll(blk, LANES - ROPE_FREQS, 1), pltpu.roll(blk, ROPE_FREQS, 1))
        outs.append(blk * cos + partner * sin_signed)
    return outs[0] if len(outs) == 1 else jnp.concatenate(outs, axis=1)


def _const_spec(shape, grid_rank):
    zeros = (0,) * len(shape)
    return pl.BlockSpec(shape, lambda *_: zeros)


def _ada_kernel(c_ref, w_ref, b_ref, o_ref):
    s = _silu(c_ref[...])
    o_ref[...] = jnp.dot(s, w_ref[...], preferred_element_type=F32,
                         precision=lax.Precision.HIGHEST) + b_ref[...]


def _ada_call(cc, w, b):
    rows, n = cc.shape[0], w.shape[1]
    tn = 512
    return pl.pallas_call(
        _ada_kernel,
        out_shape=jax.ShapeDtypeStruct((rows, n), F32),
        grid=(n // tn,),
        in_specs=[pl.BlockSpec((rows, D_MODEL), lambda j: (0, 0)),
                  pl.BlockSpec((D_MODEL, tn), lambda j: (0, j)),
                  pl.BlockSpec((1, tn), lambda j: (0, j))],
        out_specs=pl.BlockSpec((rows, tn), lambda j: (0, j)),
        compiler_params=pltpu.CompilerParams(dimension_semantics=("arbitrary",)),
        name="ada",
    )(cc, w, b)


def _store_chunk_major(u, perm_ref, up_out):
    r = jnp.dot(perm_ref[...], u.astype(BF16), preferred_element_type=F32).astype(BF16)
    rows = up_out.shape[0]
    for t in range(S5_CHUNK):
        for blk in range(N_BLOCKS):
            up_out[:, blk * BLOCK_W + t * LANES: blk * BLOCK_W + (t + 1) * LANES] = (
                r[t * rows:(t + 1) * rows, blk * LANES:(blk + 1) * LANES])


def _pre_lat_kernel(x_ref, mod_ref, ng_ref, w_ref, onesq_ref, onesk_ref, qg_ref, kg_ref, cos_ref, sin_ref, perm_ref,
                    q_out, k_out, v_out, up_out):
    bsz, tt, _ = x_ref.shape
    xn = _modulated_norm(x_ref[...], mod_ref, ng_ref).reshape(bsz * tt, D_MODEL).astype(BF16)
    p = jnp.dot(xn, w_ref[...], preferred_element_type=F32)
    o1, o2, o3 = ATTN_WIDTH, ATTN_WIDTH + KV_WIDTH, ATTN_WIDTH + 2 * KV_WIDTH
    cos = jnp.concatenate([cos_ref[...]] * bsz, axis=0)
    sin = jnp.concatenate([sin_ref[...]] * bsz, axis=0)
    q = _rope(_head_rmsnorm(p[:, 0:o1], onesq_ref, qg_ref), cos, sin)
    k = _rope(_head_rmsnorm(p[:, o1:o2], onesk_ref, kg_ref), cos, sin)
    q_out[...] = (q * ATTN_SCALE).astype(BF16).reshape(bsz, tt, ATTN_WIDTH)
    k_out[...] = k.astype(BF16).reshape(bsz, tt, KV_WIDTH)
    v_out[...] = p[:, o2:o3].astype(BF16).reshape(bsz, tt, KV_WIDTH)
    _store_chunk_major(p[:, o3:], perm_ref, up_out)


def _pre_ctx_kernel(x_ref, mod_ref, ng_ref, w_ref, onesk_ref, kg_ref, perm_ref, up_in, k_out, v_out, up_out):
    del up_in
    bsz, tt, _ = x_ref.shape
    xn = _modulated_norm(x_ref[...], mod_ref, ng_ref).reshape(bsz * tt, D_MODEL).astype(BF16)
    p = jnp.dot(xn, w_ref[...], preferred_element_type=F32)
    k_out[...] = _head_rmsnorm(p[:, 0:KV_WIDTH], onesk_ref, kg_ref).astype(BF16).reshape(bsz, tt, KV_WIDTH)
    v_out[...] = p[:, KV_WIDTH:2 * KV_WIDTH].astype(BF16).reshape(bsz, tt, KV_WIDTH)
    _store_chunk_major(p[:, 2 * KV_WIDTH:], perm_ref, up_out)


def _pre_lat_call(x, mod3, ng, w, onesq, onesk, qg, kg, cos, sin, perm, n_ctx):
    bsz, n, _ = x.shape
    tt = TOK_TILE
    up_rows = tt // S5_CHUNK * bsz
    ctx_steps = n_ctx // tt
    total_rows = (n + n_ctx) // S5_CHUNK * bsz
    tok = lambda width: pl.BlockSpec((bsz, tt, width), lambda i: (0, i, 0))
    c = lambda shape: _const_spec(shape, 1)
    return pl.pallas_call(
        _pre_lat_kernel,
        out_shape=(jax.ShapeDtypeStruct((bsz, n, ATTN_WIDTH), BF16),
                   jax.ShapeDtypeStruct((bsz, n, KV_WIDTH), BF16),
                   jax.ShapeDtypeStruct((bsz, n, KV_WIDTH), BF16),
                   jax.ShapeDtypeStruct((total_rows, N_BLOCKS * BLOCK_W), BF16)),
        grid=(n // tt,),
        in_specs=[tok(D_MODEL),
                  pl.BlockSpec((bsz, 1, 3 * D_MODEL), lambda i: (0, 0, 0)),
                  c((1, D_MODEL)), c(w.shape), c(onesq.shape), c(onesk.shape),
                  c((1, ATTN_WIDTH)), c((1, KV_WIDTH)),
                  pl.BlockSpec((tt, LANES), lambda i: (i, 0)),
                  pl.BlockSpec((tt, LANES), lambda i: (i, 0)),
                  c(perm.shape)],
        out_specs=(tok(ATTN_WIDTH), tok(KV_WIDTH), tok(KV_WIDTH),
                   pl.BlockSpec((up_rows, N_BLOCKS * BLOCK_W), lambda i: (i + ctx_steps, 0))),
        compiler_params=pltpu.CompilerParams(dimension_semantics=("arbitrary",), vmem_limit_bytes=VMEM_LIMIT),
        name="pre_lat",
    )(x, mod3, ng, w, onesq, onesk, qg, kg, cos, sin, perm)


def _pre_ctx_call(ctx, mod_ctx3, ng, w, onesk, kg, perm, up):
    bsz, n, _ = ctx.shape
    tt = TOK_TILE
    up_rows = tt // S5_CHUNK * bsz
    tok = lambda width: pl.BlockSpec((bsz, tt, width), lambda i: (0, i, 0))
    c = lambda shape: _const_spec(shape, 1)
    return pl.pallas_call(
        _pre_ctx_kernel,
        out_shape=(jax.ShapeDtypeStruct((bsz, n, KV_WIDTH), BF16),
                   jax.ShapeDtypeStruct((bsz, n, KV_WIDTH), BF16),
                   jax.ShapeDtypeStruct(up.shape, up.dtype)),
        grid=(n // tt,),
        in_specs=[tok(D_MODEL), c((1, 1, 3 * D_MODEL)), c((1, D_MODEL)), c(w.shape), c(onesk.shape),
                  c((1, KV_WIDTH)), c(perm.shape),
                  pl.BlockSpec(memory_space=pl.ANY)],
        out_specs=(tok(KV_WIDTH), tok(KV_WIDTH),
                   pl.BlockSpec((up_rows, N_BLOCKS * BLOCK_W), lambda i: (i, 0))),
        input_output_aliases={7: 2},
        compiler_params=pltpu.CompilerParams(dimension_semantics=("arbitrary",), vmem_limit_bytes=VMEM_LIMIT),
        name="pre_ctx",
    )(ctx, mod_ctx3, ng, w, onesk, kg, perm, up)


def _attn_kernel(q_ref, k_ref, v_ref, e_ref, o_ref, k4t_ref, v4_ref):
    @pl.when(pl.program_id(1) == 0)
    def _():
        k4 = jnp.dot(k_ref[...], e_ref[...], preferred_element_type=F32)
        k4t = k4.T
        v4 = jnp.dot(v_ref[...], e_ref[...], preferred_element_type=F32)
        for g in range(N_KV_HEADS):
            k4t_ref[g] = k4t[g * REP_W:(g + 1) * REP_W].astype(BF16)
            v4_ref[g] = v4[:, g * REP_W:(g + 1) * REP_W].astype(BF16)

    tq = q_ref.shape[0]
    lane = lax.broadcasted_iota(jnp.int32, (tq, REP_W), 1)
    for g in range(N_KV_HEADS):
        qg = q_ref[:, g * REP_W:(g + 1) * REP_W]
        acc = jnp.zeros((tq, REP_W), F32)
        for r in range(GQA_REP):
            in_head = (lane >= r * HEAD_DIM) & (lane < (r + 1) * HEAD_DIM)
            qr = jnp.where(in_head, qg, jnp.zeros_like(qg))
            s = jnp.dot(qr, k4t_ref[g], preferred_element_type=F32)
            m = jnp.max(s, axis=1, keepdims=True)
            p = jnp.exp(s - m)
            l = jnp.sum(p, axis=1, keepdims=True)
            o = jnp.dot(p.astype(BF16), v4_ref[g], preferred_element_type=F32)
            acc = jnp.where(in_head, o / l, acc)
        o_ref[:, g * REP_W:(g + 1) * REP_W] = acc.astype(o_ref.dtype)


def _attn_call(q, k_all, v_all, expand, tq):
    bsz, n, _ = q.shape
    nk = k_all.shape[1]
    return pl.pallas_call(
        _attn_kernel,
        out_shape=jax.ShapeDtypeStruct((bsz, n, ATTN_WIDTH), BF16),
        grid=(bsz, n // tq),
        in_specs=[pl.BlockSpec((None, tq, ATTN_WIDTH), lambda b, i: (b, i, 0)),
                  pl.BlockSpec((None, nk, KV_WIDTH), lambda b, i: (b, 0, 0)),
                  pl.BlockSpec((None, nk, KV_WIDTH), lambda b, i: (b, 0, 0)),
                  _const_spec(expand.shape, 2)],
        out_specs=pl.BlockSpec((None, tq, ATTN_WIDTH), lambda b, i: (b, i, 0)),
        scratch_shapes=[pltpu.VMEM((N_KV_HEADS, REP_W, nk), BF16),
                        pltpu.VMEM((N_KV_HEADS, nk, REP_W), BF16)],
        compiler_params=pltpu.CompilerParams(dimension_semantics=("arbitrary", "arbitrary"),
                                             vmem_limit_bytes=VMEM_LIMIT),
        name="attn",
    )(q, k_all, v_all, expand)


def _perm_in_kernel(s_ref, p_ref, o_ref):
    r = jnp.dot(s_ref[...], p_ref[...], preferred_element_type=F32)
    for j in range(GROUPS_PER_BLOCK):
        o_ref[j] = r[:, j * S5_CW:(j + 1) * S5_CW].astype(o_ref.dtype)


def _perm_in_call(up, pmat, tr):
    rows = up.shape[0]
    return pl.pallas_call(
        _perm_in_kernel,
        out_shape=jax.ShapeDtypeStruct((S5_GROUPS, rows, S5_CW), BF16),
        grid=(N_BLOCKS, rows // tr),
        in_specs=[pl.BlockSpec((tr, BLOCK_W), lambda s, i: (i, s)),
                  _const_spec(pmat.shape, 2)],
        out_specs=pl.BlockSpec((GROUPS_PER_BLOCK, tr, S5_CW), lambda s, i: (s, i, 0)),
        compiler_params=pltpu.CompilerParams(dimension_semantics=("arbitrary", "arbitrary"),
                                             vmem_limit_bytes=VMEM_LIMIT),
        name="perm_in",
    )(up, pmat)


def _perm_out_kernel(y_ref, q_ref, o_ref):
    ycat = jnp.concatenate([y_ref[j] for j in range(GROUPS_PER_BLOCK)], axis=1)
    o_ref[...] = jnp.dot(ycat, q_ref[...], preferred_element_type=F32).astype(o_ref.dtype)


def _perm_out_call(yg, qmat, tr):
    rows = yg.shape[1]
    return pl.pallas_call(
        _perm_out_kernel,
        out_shape=jax.ShapeDtypeStruct((rows, N_BLOCKS * BLOCK_W), BF16),
        grid=(N_BLOCKS, rows // tr),
        in_specs=[pl.BlockSpec((GROUPS_PER_BLOCK, tr, S5_CW), lambda s, i: (s, i, 0)),
                  _const_spec(qmat.shape, 2)],
        out_specs=pl.BlockSpec((tr, BLOCK_W), lambda s, i: (i, s)),
        compiler_params=pltpu.CompilerParams(dimension_semantics=("arbitrary", "arbitrary"),
                                             vmem_limit_bytes=VMEM_LIMIT),
        name="perm_out",
    )(yg, qmat)


def _s5_kernel(u_ref, m_ref, sin_ref, g_ref, a_ref, y_ref, buf_ref, *, n_ctx_chunks, n_chunks):
    half = S5_STATE
    gb = u_ref.shape[0]
    for j in range(gb):
        buf_ref[j] = jnp.dot(u_ref[j], sin_ref[j], preferred_element_type=F32)
    a_re = [jnp.broadcast_to(a_ref[j, 0:1, :], (SUBLANES, LANES)) for j in range(gb)]
    a_im = [jnp.broadcast_to(a_ref[j, 1:2, :], (SUBLANES, LANES)) for j in range(gb)]
    is_fwd = lax.broadcasted_iota(jnp.int32, (SUBLANES, LANES), 1) < half

    def step(k, carry):
        pos_b = jnp.where(k < n_ctx_chunks, n_ctx_chunks - 1 - k, n_chunks + n_ctx_chunks - 1 - k)
        rf = pl.multiple_of(k * SUBLANES, SUBLANES)
        rb = pl.multiple_of(pos_b * SUBLANES, SUBLANES)
        out = []
        for j in range(gb):
            h_re, h_im = carry[2 * j], carry[2 * j + 1]
            xf = buf_ref[j, pl.ds(rf, SUBLANES), :]
            xb = buf_ref[j, pl.ds(rb, SUBLANES), :]
            buf_ref[j, pl.ds(rf, SUBLANES), 0:half] = h_re[:, 0:half]
            buf_ref[j, pl.ds(rf, SUBLANES), 2 * half:3 * half] = h_im[:, 0:half]
            buf_ref[j, pl.ds(rb, SUBLANES), half:2 * half] = h_re[:, half:]
            buf_ref[j, pl.ds(rb, SUBLANES), 3 * half:] = h_im[:, half:]
            x_re = jnp.where(is_fwd, xf[:, 0:LANES], xb[:, 0:LANES])
            x_im = jnp.where(is_fwd, xf[:, LANES:], xb[:, LANES:])
            out.append(a_re[j] * h_re - a_im[j] * h_im + x_re)
            out.append(a_re[j] * h_im + a_im[j] * h_re + x_im)
        return tuple(out)

    zero = jnp.zeros((SUBLANES, LANES), F32)
    lax.fori_loop(0, n_chunks, step, (zero,) * (2 * gb))

    r0 = n_ctx_chunks * SUBLANES
    for j in range(gb):
        y = jnp.dot(u_ref[j, r0:, :], m_ref[j], preferred_element_type=F32)
        y = y + jnp.dot(buf_ref[j, r0:, :].astype(BF16), g_ref[j], preferred_element_type=F32)
        y_ref[j] = y.astype(y_ref.dtype)


def _s5_call(ug, m, sin, gmat, a16, n_ctx_chunks, gb):
    groups, rows, _ = ug.shape
    n_chunks = rows // SUBLANES
    out_rows = rows - n_ctx_chunks * SUBLANES
    mat = lambda: pl.BlockSpec((gb, S5_CW, S5_CW), lambda g: (g, 0, 0))
    return pl.pallas_call(
        functools.partial(_s5_kernel, n_ctx_chunks=n_ctx_chunks, n_chunks=n_chunks),
        out_shape=jax.ShapeDtypeStruct((groups, out_rows, S5_CW), BF16),
        grid=(groups // gb,),
        in_specs=[pl.BlockSpec((gb, rows, S5_CW), lambda g: (g, 0, 0)),
                  mat(), mat(), mat(),
                  pl.BlockSpec((gb, 2, LANES), lambda g: (g, 0, 0))],
        out_specs=pl.BlockSpec((gb, out_rows, S5_CW), lambda g: (g, 0, 0)),
        scratch_shapes=[pltpu.VMEM((gb, rows, S5_CW), F32)],
        compiler_params=pltpu.CompilerParams(dimension_semantics=("arbitrary",),
                                             vmem_limit_bytes=VMEM_LIMIT),
        name="s5",
    )(ug, m, sin, gmat, a16)


def _s5_operators(lam_re, lam_im, log_dt, b_re, b_im, c_re, c_im, d_skip):
    hi = lax.Precision.HIGHEST
    T, H, G, P = S5_CHUNK, S5_GROUP, S5_GROUPS, S5_STATE
    lr = jnp.minimum(lam_re.astype(F32), -1e-4)
    li = lam_im.astype(F32)
    dt = jnp.exp(log_dt.astype(F32))[..., None]
    taus = jnp.arange(T + 1, dtype=F32)[:, None, None, None]
    mag = jnp.exp(lr * dt * taus)
    pw_r = mag * jnp.cos(li * dt * taus)
    pw_i = mag * jnp.sin(li * dt * taus)
    nr, ni = pw_r[1] - 1.0, pw_i[1]
    den = lr * lr + li * li
    cf_r = ((nr * lr + ni * li) / den)[..., None]
    cf_i = ((ni * lr - nr * li) / den)[..., None]
    br, bi = b_re.astype(F32), b_im.astype(F32)
    bb_r = cf_r * br - cf_i * bi
    bb_i = cf_r * bi + cf_i * br
    cr, ci = c_re.astype(F32), c_im.astype(F32)

    cp_r = cr[None] * pw_r[:T, :, :, None, :] - ci[None] * pw_i[:T, :, :, None, :]
    cp_i = cr[None] * pw_i[:T, :, :, None, :] + ci[None] * pw_r[:T, :, :, None, :]
    kern = (jnp.einsum('tdgkp,dgph->tdgkh', cp_r, bb_r, precision=hi)
            - jnp.einsum('tdgkp,dgph->tdgkh', cp_i, bb_i, precision=hi))
    s_idx = jnp.arange(T)[:, None]
    t_idx = jnp.arange(T)[None, :]
    lag = jnp.arange(T)[:, None, None]
    sel_f = (t_idx - s_idx == lag).astype(F32)
    sel_b = (s_idx - t_idx == lag).astype(F32)
    m = (jnp.einsum('ast,agkh->gshtk', sel_f, kern[:, 0], precision=hi)
         + jnp.einsum('ast,agkh->gshtk', sel_b, kern[:, 1], precision=hi))
    skip = d_skip.astype(F32).reshape(G, H)
    eye_t = jnp.eye(T, dtype=F32)
    eye_h = jnp.eye(H, dtype=F32)
    m = m + skip[:, None, :, None, None] * eye_t[None, :, None, :, None] * eye_h[None, None, :, None, :]
    m = m.reshape(G, S5_CW, S5_CW)

    def inc(pw_sel_r, pw_sel_i, d):
        re = pw_sel_r[..., None] * bb_r[d][None] - pw_sel_i[..., None] * bb_i[d][None]
        im = pw_sel_r[..., None] * bb_i[d][None] + pw_sel_i[..., None] * bb_r[d][None]
        to_rows = lambda w: w.transpose(1, 0, 3, 2).reshape(G, S5_CW, P)
        return to_rows(re), to_rows(im)
    inc_f_r, inc_f_i = inc(pw_r[T - 1::-1, 0][:T], pw_i[T - 1::-1, 0][:T], 0)
    inc_b_r, inc_b_i = inc(pw_r[:T, 1], pw_i[:T, 1], 1)
    sin = jnp.concatenate([inc_f_r, inc_b_r, inc_f_i, inc_b_i], axis=-1)

    def read(pw_sel_r, pw_sel_i, d):
        re = cr[d][None] * pw_sel_r[:, :, None, :] - ci[d][None] * pw_sel_i[:, :, None, :]
        im = cr[d][None] * pw_sel_i[:, :, None, :] + ci[d][None] * pw_sel_r[:, :, None, :]
        to_cols = lambda w: w.transpose(1, 3, 0, 2).reshape(G, P, S5_CW)
        return to_cols(re), to_cols(-im)
    out_f_r, out_f_i = read(pw_r[1:, 0], pw_i[1:, 0], 0)
    out_b_r, out_b_i = read(pw_r[T:0:-1, 1], pw_i[T:0:-1, 1], 1)
    gmat = jnp.concatenate([out_f_r, out_b_r, out_f_i, out_b_i], axis=1)

    a16 = jnp.stack([jnp.concatenate([pw_r[T, 0], pw_r[T, 1]], axis=-1),
                     jnp.concatenate([pw_i[T, 0], pw_i[T, 1]], axis=-1)], axis=1)
    return m.astype(BF16), sin.astype(BF16), gmat.astype(BF16), a16


def _final_kernel(x_ref, mod_ref, ng_ref, wg_ref, ya_ref, yp_ref, permt_ref, wglu_ref, bglu_ref, wa_ref, wb_ref,
                  wo_ref, fg_ref, o_ref):
    bsz, tt, _ = x_ref.shape
    rows = bsz * tt
    x3 = x_ref[...]
    xn = _modulated_norm(x3, mod_ref, ng_ref).reshape(rows, D_MODEL).astype(BF16)
    gates = jnp.dot(xn, wg_ref[...], preferred_element_type=F32)
    o1, o2, o3 = ATTN_WIDTH, ATTN_WIDTH + S5_WIDTH, ATTN_WIDTH + S5_WIDTH + D_MODEL
    slabs = [jnp.concatenate([yp_ref[:, blk * BLOCK_W + t * LANES: blk * BLOCK_W + (t + 1) * LANES]
                              for blk in range(N_BLOCKS)], axis=1) for t in range(S5_CHUNK)]
    y = jnp.dot(permt_ref[...], jnp.concatenate(slabs, axis=0), preferred_element_type=F32)
    z = y * (0.5 * (1.0 + jnp.tanh(0.7978845608028654 * (y + 0.044715 * (y * y * y)))))
    zz = z * jax.nn.sigmoid(jnp.dot(z.astype(BF16), wglu_ref[...], preferred_element_type=F32) + bglu_ref[...])
    ya = ya_ref[...].reshape(rows, ATTN_WIDTH).astype(F32)
    ta = (ya * _silu(gates[:, 0:o1])).astype(BF16)
    tb = (zz * _silu(gates[:, o1:o2])).astype(BF16)
    pa = jnp.dot(ta, wa_ref[...], preferred_element_type=F32)
    pb = jnp.dot(tb, wb_ref[...], preferred_element_type=F32)
    mix = jax.nn.sigmoid(gates[:, o2:o3]) * pa + jax.nn.sigmoid(gates[:, o3:]) * pb
    o = jnp.dot(mix.astype(BF16), wo_ref[...], preferred_element_type=F32).reshape(bsz, tt, D_MODEL)
    h = x3 + mod_ref[:, :, 2 * D_MODEL:] * o
    ms = jnp.mean(h * h, axis=-1, keepdims=True)
    o_ref[...] = h * lax.rsqrt(ms + EPS) * fg_ref[...]


def _final_call(x, mod3, ng, wg, ya, yp, permt, wglu, bglu, wa, wb, wo, fg):
    bsz, n, _ = x.shape
    tt = TOK_TILE
    up_rows = tt // S5_CHUNK * bsz
    tok = lambda width: pl.BlockSpec((bsz, tt, width), lambda i: (0, i, 0))
    c = lambda shape: _const_spec(shape, 1)
    return pl.pallas_call(
        _final_kernel,
        out_shape=jax.ShapeDtypeStruct((bsz, n, D_MODEL), F32),
        grid=(n // tt,),
        in_specs=[tok(D_MODEL),
                  pl.BlockSpec((bsz, 1, 3 * D_MODEL), lambda i: (0, 0, 0)),
                  c((1, D_MODEL)), c(wg.shape),
                  tok(ATTN_WIDTH),
                  pl.BlockSpec((up_rows, N_BLOCKS * BLOCK_W), lambda i: (i, 0)),
                  c(permt.shape), c(wglu.shape), c((1, S5_WIDTH)), c(wa.shape), c(wb.shape), c(wo.shape),
                  c((1, D_MODEL))],
        out_specs=tok(D_MODEL),
        compiler_params=pltpu.CompilerParams(dimension_semantics=("arbitrary",), vmem_limit_bytes=VMEM_LIMIT),
        name="final",
    )(x, mod3, ng, wg, ya, yp, permt, wglu, bglu, wa, wb, wo, fg)


def _rope_tables(n):
    rows = n // GRID_W
    row_ids = jnp.repeat(jnp.arange(rows, dtype=F32), GRID_W)
    col_ids = jnp.tile(jnp.arange(GRID_W, dtype=F32), rows)
    freqs = ROPE_THETA ** (-jnp.arange(ROPE_FREQS, dtype=F32) / ROPE_FREQS)
    ang_r, ang_c = row_ids[:, None] * freqs, col_ids[:, None] * freqs
    cos = jnp.concatenate([jnp.cos(ang_r)] * 2 + [jnp.cos(ang_c)] * 2, axis=1)
    sin = jnp.concatenate([-jnp.sin(ang_r), jnp.sin(ang_r), -jnp.sin(ang_c), jnp.sin(ang_c)], axis=1)
    reps = LANES // HEAD_DIM
    return jnp.tile(cos, (1, reps)), jnp.tile(sin, (1, reps))


def _block_ones(width):
    idx = jnp.arange(width) // HEAD_DIM
    return (idx[:, None] == idx[None, :]).astype(BF16)


def _row_perm(bsz, tt):
    chunks = tt // S5_CHUNK
    r = jnp.arange(bsz * tt)
    t, pc, b = r // (chunks * bsz), (r // bsz) % chunks, r % bsz
    src = b * tt + pc * S5_CHUNK + t
    return (src[:, None] == jnp.arange(bsz * tt)[None, :]).astype(BF16)


def _lane_perm():
    r = jnp.arange(BLOCK_W)
    t, j, h = r // LANES, (r % LANES) // S5_GROUP, r % S5_GROUP
    dst = j * S5_CW + t * S5_GROUP + h
    return (dst[:, None] == jnp.arange(BLOCK_W)[None, :]).astype(BF16)


def kernel(x, c, ctx, c_ctx, norm_g, w_ada, b_ada, w_in, q_norm_g, k_norm_g, s5_lam_re, s5_lam_im, s5_log_dt,
           s5_b_re, s5_b_im, s5_c_re, s5_c_im, s5_d, w_glu, b_glu, w_branch_attn, w_branch_s5, w_out,
           final_norm_g):
    assert w_in.shape[0] == 1, "single-layer block"
    bsz, n, _ = x.shape
    n_ctx = ctx.shape[1]
    assert n % TOK_TILE == 0 and n_ctx % TOK_TILE == 0 and bsz == SUBLANES

    ada_rows = 2 * SUBLANES
    cc = jnp.concatenate([c, c_ctx[None], jnp.zeros((ada_rows - bsz - 1, D_MODEL), F32)], axis=0)
    mod = _ada_call(cc, w_ada[0], b_ada[0][None])
    mod3 = mod[:bsz].reshape(bsz, 1, 3 * D_MODEL)
    mod_ctx3 = mod[bsz:bsz + 1].reshape(1, 1, 3 * D_MODEL)

    offs = [0]
    for s in IN_SIZES:
        offs.append(offs[-1] + s)
    w_bf = w_in[0].astype(BF16)
    w_pre = jnp.concatenate([w_bf[:, offs[0]:offs[3]], w_bf[:, offs[4]:offs[5]]], axis=1)
    w_pre_ctx = w_pre[:, ATTN_WIDTH:]
    w_gates = jnp.concatenate([w_bf[:, offs[3]:offs[4]], w_bf[:, offs[5]:]], axis=1)

    ng = norm_g[0][None]
    qg = jnp.tile(q_norm_g[0], N_HEADS)[None]
    kg = jnp.tile(k_norm_g[0], N_KV_HEADS)[None]
    onesq, onesk = _block_ones(ATTN_WIDTH), _block_ones(KV_WIDTH)
    cos, sin = _rope_tables(n)
    row_perm = _row_perm(bsz, TOK_TILE)
    lane_perm = _lane_perm()

    q, k_lat, v_lat, up = _pre_lat_call(x, mod3, ng, w_pre, onesq, onesk, qg, kg, cos, sin, row_perm, n_ctx)
    k_ctx, v_ctx, up = _pre_ctx_call(ctx, mod_ctx3, ng, w_pre_ctx, onesk, kg, row_perm, up)

    k_all = jnp.concatenate([k_ctx, k_lat], axis=1)
    v_all = jnp.concatenate([v_ctx, v_lat], axis=1)
    col = jnp.arange(N_KV_HEADS * REP_W)
    src = (col // REP_W) * HEAD_DIM + col % HEAD_DIM
    expand = (jnp.arange(KV_WIDTH)[:, None] == src[None, :]).astype(BF16)
    y_attn = _attn_call(q, k_all, v_all, expand, tq=256)

    ug = _perm_in_call(up, lane_perm, tr=384)
    m, s_in, gmat, a16 = _s5_operators(s5_lam_re[0], s5_lam_im[0], s5_log_dt[0], s5_b_re[0], s5_b_im[0],
                                       s5_c_re[0], s5_c_im[0], s5_d[0])
    yg = _s5_call(ug, m, s_in, gmat, a16, n_ctx // S5_CHUNK, gb=GROUPS_PER_BLOCK)
    yp = _perm_out_call(yg, lane_perm.T, tr=256)

    return _final_call(x, mod3, ng, w_gates, y_attn, yp, row_perm.T, w_glu[0].astype(BF16), b_glu[0][None],
                       w_branch_attn[0].astype(BF16), w_branch_s5[0].astype(BF16), w_out[0].astype(BF16),
                       final_norm_g[None])
```

```python
import functools

import numpy as np
import jax
import jax.numpy as jnp
from jax import lax
from jax.experimental import pallas as pl
from jax.experimental.pallas import tpu as pltpu

D_MODEL = 1024
GRID_W = 64
N_HEADS = 8
N_KV_HEADS = 2
HEAD_DIM = 64
GQA_REP = N_HEADS // N_KV_HEADS
ATTN_WIDTH = N_HEADS * HEAD_DIM
KV_WIDTH = N_KV_HEADS * HEAD_DIM
ATTN_SCALE = HEAD_DIM ** -0.5
ROPE_THETA = 10000.0
ROPE_FREQS = HEAD_DIM // 4
S5_WIDTH = 512
S5_GROUP = 16
S5_GROUPS = S5_WIDTH // S5_GROUP
S5_STATE = 64
EPS = 1e-6
IN_SIZES = (ATTN_WIDTH, KV_WIDTH, KV_WIDTH, ATTN_WIDTH, S5_WIDTH, S5_WIDTH, D_MODEL, D_MODEL)

LANES = 128
SUBLANES = 8
S5_CHUNK = 16
S5_CW = S5_CHUNK * S5_GROUP
GROUPS_PER_BLOCK = LANES // S5_GROUP
N_BLOCKS = S5_WIDTH // LANES
BLOCK_W = S5_CHUNK * LANES
TOK_TILE = 64
REP_W = GQA_REP * HEAD_DIM
VMEM_LIMIT = 56 * 1024 * 1024

F32 = jnp.float32
BF16 = jnp.bfloat16


def _silu(t):
    return t * jax.nn.sigmoid(t)


def _modulated_norm(x3, mod_ref, ng_ref):
    ms = jnp.mean(x3 * x3, axis=-1, keepdims=True)
    y = x3 * lax.rsqrt(ms + EPS) * ng_ref[...]
    return y * (1.0 + mod_ref[:, :, D_MODEL:2 * D_MODEL]) + mod_ref[:, :, 0:D_MODEL]


def _head_rmsnorm(t, ones_ref, g_ref):
    ss = jnp.dot((t * t).astype(BF16), ones_ref[...], preferred_element_type=F32)
    return t * lax.rsqrt(ss * (1.0 / HEAD_DIM) + EPS) * g_ref[...]


def _rope(t, cos, sin_signed):
    rows = t.shape[0]
    lane = lax.broadcasted_iota(jnp.int32, (rows, LANES), 1)
    first = (lane & ROPE_FREQS) == 0
    outs = []
    for j in range(t.shape[1] // LANES):
        blk = t[:, j * LANES:(j + 1) * LANES]
        partner = jnp.where(first, pltpu.roll(blk, LANES - ROPE_FREQS, 1), pltpu.roll(blk, ROPE_FREQS, 1))
        outs.append(blk * cos + partner * sin_signed)
    return outs[0] if len(outs) == 1 else jnp.concatenate(outs, axis=1)


def _const_spec(shape, grid_rank):
    zeros = (0,) * len(shape)
    return pl.BlockSpec(shape, lambda *_: zeros)


def _ada_kernel(c_ref, w_ref, b_ref, o_ref):
    s = _silu(c_ref[...])
    o_ref[...] = jnp.dot(s, w_ref[...], preferred_element_type=F32,
                         precision=lax.Precision.HIGHEST) + b_ref[...]


def _ada_call(cc, w, b):
    rows, n = cc.shape[0], w.shape[1]
    tn = 512
    return pl.pallas_call(
        _ada_kernel,
        out_shape=jax.ShapeDtypeStruct((rows, n), F32),
        grid=(n // tn,),
        in_specs=[pl.BlockSpec((rows, D_MODEL), lambda j: (0, 0)),
                  pl.BlockSpec((D_MODEL, tn), lambda j: (0, j)),
                  pl.BlockSpec((1, tn), lambda j: (0, j))],
        out_specs=pl.BlockSpec((rows, tn), lambda j: (0, j)),
        compiler_params=pltpu.CompilerParams(dimension_semantics=("arbitrary",)),
        name="ada",
    )(cc, w, b)


def _store_chunk_major(u, perm_ref, up_out):
    r = jnp.dot(perm_ref[...], u.astype(BF16), preferred_element_type=F32).astype(BF16)
    rows = up_out.shape[0]
    for t in range(S5_CHUNK):
        for blk in range(N_BLOCKS):
            up_out[:, blk * BLOCK_W + t * LANES: blk * BLOCK_W + (t + 1) * LANES] = (
                r[t * rows:(t + 1) * rows, blk * LANES:(blk + 1) * LANES])


def _pre_lat_kernel(x_ref, mod_ref, ng_ref, w_ref, onesq_ref, onesk_ref, qg_ref, kg_ref, cos_ref, sin_ref, perm_ref,
                    q_out, k_out, v_out, up_out):
    bsz, tt, _ = x_ref.shape
    xn = _modulated_norm(x_ref[...], mod_ref, ng_ref).reshape(bsz * tt, D_MODEL).astype(BF16)
    p = jnp.dot(xn, w_ref[...], preferred_element_type=F32)
    o1, o2, o3 = ATTN_WIDTH, ATTN_WIDTH + KV_WIDTH, ATTN_WIDTH + 2 * KV_WIDTH
    cos = jnp.concatenate([cos_ref[...]] * bsz, axis=0)
    sin = jnp.concatenate([sin_ref[...]] * bsz, axis=0)
    q = _rope(_head_rmsnorm(p[:, 0:o1], onesq_ref, qg_ref), cos, sin)
    k = _rope(_head_rmsnorm(p[:, o1:o2], onesk_ref, kg_ref), cos, sin)
    q_out[...] = (q * ATTN_SCALE).astype(BF16).reshape(bsz, tt, ATTN_WIDTH)
    k_out[...] = k.astype(BF16).reshape(bsz, tt, KV_WIDTH)
    v_out[...] = p[:, o2:o3].astype(BF16).reshape(bsz, tt, KV_WIDTH)
    _store_chunk_major(p[:, o3:], perm_ref, up_out)


def _pre_ctx_kernel(x_ref, mod_ref, ng_ref, w_ref, onesk_ref, kg_ref, perm_ref, up_in, k_out, v_out, up_out):
    del up_in
    bsz, tt, _ = x_ref.shape
    xn = _modulated_norm(x_ref[...], mod_ref, ng_ref).reshape(bsz * tt, D_MODEL).astype(BF16)
    p = jnp.dot(xn, w_ref[...], preferred_element_type=F32)
    k_out[...] = _head_rmsnorm(p[:, 0:KV_WIDTH], onesk_ref, kg_ref).astype(BF16).reshape(bsz, tt, KV_WIDTH)
    v_out[...] = p[:, KV_WIDTH:2 * KV_WIDTH].astype(BF16).reshape(bsz, tt, KV_WIDTH)
    _store_chunk_major(p[:, 2 * KV_WIDTH:], perm_ref, up_out)


def _pre_lat_call(x, mod3, ng, w, onesq, onesk, qg, kg, cos, sin, perm, n_ctx):
    bsz, n, _ = x.shape
    tt = TOK_TILE
    up_rows = tt // S5_CHUNK * bsz
    ctx_steps = n_ctx // tt
    total_rows = (n + n_ctx) // S5_CHUNK * bsz
    tok = lambda width: pl.BlockSpec((bsz, tt, width), lambda i: (0, i, 0))
    c = lambda shape: _const_spec(shape, 1)
    return pl.pallas_call(
        _pre_lat_kernel,
        out_shape=(jax.ShapeDtypeStruct((bsz, n, ATTN_WIDTH), BF16),
                   jax.ShapeDtypeStruct((bsz, n, KV_WIDTH), BF16),
                   jax.ShapeDtypeStruct((bsz, n, KV_WIDTH), BF16),
                   jax.ShapeDtypeStruct((total_rows, N_BLOCKS * BLOCK_W), BF16)),
        grid=(n // tt,),
        in_specs=[tok(D_MODEL),
                  pl.BlockSpec((bsz, 1, 3 * D_MODEL), lambda i: (0, 0, 0)),
                  c((1, D_MODEL)), c(w.shape), c(onesq.shape), c(onesk.shape),
                  c((1, ATTN_WIDTH)), c((1, KV_WIDTH)),
                  pl.BlockSpec((tt, LANES), lambda i: (i, 0)),
                  pl.BlockSpec((tt, LANES), lambda i: (i, 0)),
                  c(perm.shape)],
        out_specs=(tok(ATTN_WIDTH), tok(KV_WIDTH), tok(KV_WIDTH),
                   pl.BlockSpec((up_rows, N_BLOCKS * BLOCK_W), lambda i: (i + ctx_steps, 0))),
        compiler_params=pltpu.CompilerParams(dimension_semantics=("arbitrary",), vmem_limit_bytes=VMEM_LIMIT),
        name="pre_lat",
    )(x, mod3, ng, w, onesq, onesk, qg, kg, cos, sin, perm)


def _pre_ctx_call(ctx, mod_ctx3, ng, w, onesk, kg, perm, up):
    bsz, n, _ = ctx.shape
    tt = TOK_TILE
    up_rows = tt // S5_CHUNK * bsz
    tok = lambda width: pl.BlockSpec((bsz, tt, width), lambda i: (0, i, 0))
    c = lambda shape: _const_spec(shape, 1)
    return pl.pallas_call(
        _pre_ctx_kernel,
        out_shape=(jax.ShapeDtypeStruct((bsz, n, KV_WIDTH), BF16),
                   jax.ShapeDtypeStruct((bsz, n, KV_WIDTH), BF16),
                   jax.ShapeDtypeStruct(up.shape, up.dtype)),
        grid=(n // tt,),
        in_specs=[tok(D_MODEL), c((1, 1, 3 * D_MODEL)), c((1, D_MODEL)), c(w.shape), c(onesk.shape),
                  c((1, KV_WIDTH)), c(perm.shape),
                  pl.BlockSpec(memory_space=pl.ANY)],
        out_specs=(tok(KV_WIDTH), tok(KV_WIDTH),
                   pl.BlockSpec((up_rows, N_BLOCKS * BLOCK_W), lambda i: (i, 0))),
        input_output_aliases={7: 2},
        compiler_params=pltpu.CompilerParams(dimension_semantics=("arbitrary",), vmem_limit_bytes=VMEM_LIMIT),
        name="pre_ctx",
    )(ctx, mod_ctx3, ng, w, onesk, kg, perm, up)


def _attn_kernel(q_ref, k_ref, v_ref, e_ref, o_ref, k4t_ref, v4_ref):
    @pl.when(pl.program_id(1) == 0)
    def _():
        k4 = jnp.dot(k_ref[...], e_ref[...], preferred_element_type=F32)
        k4t = k4.T
        v4 = jnp.dot(v_ref[...], e_ref[...], preferred_element_type=F32)
        for g in range(N_KV_HEADS):
            k4t_ref[g] = k4t[g * REP_W:(g + 1) * REP_W].astype(BF16)
            v4_ref[g] = v4[:, g * REP_W:(g + 1) * REP_W].astype(BF16)

    tq = q_ref.shape[0]
    lane = lax.broadcasted_iota(jnp.int32, (tq, REP_W), 1)
    for g in range(N_KV_HEADS):
        qg = q_ref[:, g * REP_W:(g + 1) * REP_W]
        acc = jnp.zeros((tq, REP_W), F32)
        for r in range(GQA_REP):
            in_head = (lane >= r * HEAD_DIM) & (lane < (r + 1) * HEAD_DIM)
            qr = jnp.where(in_head, qg, jnp.zeros_like(qg))
            s = jnp.dot(qr, k4t_ref[g], preferred_element_type=F32)
            m = jnp.max(s, axis=1, keepdims=True)
            p = jnp.exp(s - m)
            l = jnp.sum(p, axis=1, keepdims=True)
            o = jnp.dot(p.astype(BF16), v4_ref[g], preferred_element_type=F32)
            acc = jnp.where(in_head, o / l, acc)
        o_ref[:, g * REP_W:(g + 1) * REP_W] = acc.astype(o_ref.dtype)


def _attn_call(q, k_all, v_all, expand, tq):
    bsz, n, _ = q.shape
    nk = k_all.shape[1]
    return pl.pallas_call(
        _attn_kernel,
        out_shape=jax.ShapeDtypeStruct((bsz, n, ATTN_WIDTH), BF16),
        grid=(bsz, n // tq),
        in_specs=[pl.BlockSpec((None, tq, ATTN_WIDTH), lambda b, i: (b, i, 0)),
                  pl.BlockSpec((None, nk, KV_WIDTH), lambda b, i: (b, 0, 0)),
                  pl.BlockSpec((None, nk, KV_WIDTH), lambda b, i: (b, 0, 0)),
                  _const_spec(expand.shape, 2)],
        out_specs=pl.BlockSpec((None, tq, ATTN_WIDTH), lambda b, i: (b, i, 0)),
        scratch_shapes=[pltpu.VMEM((N_KV_HEADS, REP_W, nk), BF16),
                        pltpu.VMEM((N_KV_HEADS, nk, REP_W), BF16)],
        compiler_params=pltpu.CompilerParams(dimension_semantics=("arbitrary", "arbitrary"),
                                             vmem_limit_bytes=VMEM_LIMIT),
        name="attn",
    )(q, k_all, v_all, expand)


def _perm_in_kernel(s_ref, p_ref, o_ref):
    r = jnp.dot(s_ref[...], p_ref[...], preferred_element_type=F32)
    for j in range(GROUPS_PER_BLOCK):
        o_ref[j] = r[:, j * S5_CW:(j + 1) * S5_CW].astype(o_ref.dtype)


def _perm_in_call(up, pmat, tr):
    rows = up.shape[0]
    return pl.pallas_call(
        _perm_in_kernel,
        out_shape=jax.ShapeDtypeStruct((S5_GROUPS, rows, S5_CW), BF16),
        grid=(N_BLOCKS, rows // tr),
        in_specs=[pl.BlockSpec((tr, BLOCK_W), lambda s, i: (i, s)),
                  _const_spec(pmat.shape, 2)],
        out_specs=pl.BlockSpec((GROUPS_PER_BLOCK, tr, S5_CW), lambda s, i: (s, i, 0)),
        compiler_params=pltpu.CompilerParams(dimension_semantics=("arbitrary", "arbitrary"),
                                             vmem_limit_bytes=VMEM_LIMIT),
        name="perm_in",
    )(up, pmat)


def _perm_out_kernel(y_ref, q_ref, o_ref):
    ycat = jnp.concatenate([y_ref[j] for j in range(GROUPS_PER_BLOCK)], axis=1)
    o_ref[...] = jnp.dot(ycat, q_ref[...], preferred_element_type=F32).astype(o_ref.dtype)


def _perm_out_call(yg, qmat, tr):
    rows = yg.shape[1]
    return pl.pallas_call(
        _perm_out_kernel,
        out_shape=jax.ShapeDtypeStruct((rows, N_BLOCKS * BLOCK_W), BF16),
        grid=(N_BLOCKS, rows // tr),
        in_specs=[pl.BlockSpec((GROUPS_PER_BLOCK, tr, S5_CW), lambda s, i: (s, i, 0)),
                  _const_spec(qmat.shape, 2)],
        out_specs=pl.BlockSpec((tr, BLOCK_W), lambda s, i: (i, s)),
        compiler_params=pltpu.CompilerParams(dimension_semantics=("arbitrary", "arbitrary"),
                                             vmem_limit_bytes=VMEM_LIMIT),
        name="perm_out",
    )(yg, qmat)


def _s5_kernel(u_ref, m_ref, sin_ref, g_ref, a_ref, y_ref, buf_ref, *, n_ctx_chunks, n_chunks):
    half = S5_STATE
    gb = u_ref.shape[0]
    for j in range(gb):
        buf_ref[j] = jnp.dot(u_ref[j], sin_ref[j], preferred_element_type=F32)
    a_re = [jnp.broadcast_to(a_ref[j, 0:1, :], (SUBLANES, LANES)) for j in range(gb)]
    a_im = [jnp.broadcast_to(a_ref[j, 1:2, :], (SUBLANES, LANES)) for j in range(gb)]
    is_fwd = lax.broadcasted_iota(jnp.int32, (SUBLANES, LANES), 1) < half

    def step(k, carry):
        pos_b = jnp.where(k < n_ctx_chunks, n_ctx_chunks - 1 - k, n_chunks + n_ctx_chunks - 1 - k)
        rf = pl.multiple_of(k * SUBLANES, SUBLANES)
        rb = pl.multiple_of(pos_b * SUBLANES, SUBLANES)
        out = []
        for j in range(gb):
            h_re, h_im = carry[2 * j], carry[2 * j + 1]
            xf = buf_ref[j, pl.ds(rf, SUBLANES), :]
            xb = buf_ref[j, pl.ds(rb, SUBLANES), :]
            buf_ref[j, pl.ds(rf, SUBLANES), 0:half] = h_re[:, 0:half]
            buf_ref[j, pl.ds(rf, SUBLANES), 2 * half:3 * half] = h_im[:, 0:half]
            buf_ref[j, pl.ds(rb, SUBLANES), half:2 * half] = h_re[:, half:]
            buf_ref[j, pl.ds(rb, SUBLANES), 3 * half:] = h_im[:, half:]
            x_re = jnp.where(is_fwd, xf[:, 0:LANES], xb[:, 0:LANES])
            x_im = jnp.where(is_fwd, xf[:, LANES:], xb[:, LANES:])
            out.append(a_re[j] * h_re - a_im[j] * h_im + x_re)
            out.append(a_re[j] * h_im + a_im[j] * h_re + x_im)
        return tuple(out)

    zero = jnp.zeros((SUBLANES, LANES), F32)
    lax.fori_loop(0, n_chunks, step, (zero,) * (2 * gb))

    r0 = n_ctx_chunks * SUBLANES
    for j in range(gb):
        y = jnp.dot(u_ref[j, r0:, :], m_ref[j], preferred_element_type=F32)
        y = y + jnp.dot(buf_ref[j, r0:, :].astype(BF16), g_ref[j], preferred_element_type=F32)
        y_ref[j] = y.astype(y_ref.dtype)


def _s5_call(ug, m, sin, gmat, a16, n_ctx_chunks, gb):
    groups, rows, _ = ug.shape
    n_chunks = rows // SUBLANES
    out_rows = rows - n_ctx_chunks * SUBLANES
    mat = lambda: pl.BlockSpec((gb, S5_CW, S5_CW), lambda g: (g, 0, 0))
    return pl.pallas_call(
        functools.partial(_s5_kernel, n_ctx_chunks=n_ctx_chunks, n_chunks=n_chunks),
        out_shape=jax.ShapeDtypeStruct((groups, out_rows, S5_CW), BF16),
        grid=(groups // gb,),
        in_specs=[pl.BlockSpec((gb, rows, S5_CW), lambda g: (g, 0, 0)),
                  mat(), mat(), mat(),
                  pl.BlockSpec((gb, 2, LANES), lambda g: (g, 0, 0))],
        out_specs=pl.BlockSpec((gb, out_rows, S5_CW), lambda g: (g, 0, 0)),
        scratch_shapes=[pltpu.VMEM((gb, rows, S5_CW), F32)],
        compiler_params=pltpu.CompilerParams(dimension_semantics=("arbitrary",),
                                             vmem_limit_bytes=VMEM_LIMIT),
        name="s5",
    )(ug, m, sin, gmat, a16)


def _s5_ops_kernel(lam_ref, ldt_ref, bt_ref, ct_ref, d_ref, m_ref, sin_ref, g_ref, a_ref):
    T, H = S5_CHUNK, S5_GROUP
    lr = jnp.minimum(lam_ref[0:1, :], -1e-4)
    li = lam_ref[1:2, :]
    dt = jnp.exp(ldt_ref[...])
    taus = lax.broadcasted_iota(jnp.int32, (3 * SUBLANES, LANES), 0).astype(F32)
    mag = jnp.exp(lr * dt * taus)
    pw_r = mag * jnp.cos(li * dt * taus)
    pw_i = mag * jnp.sin(li * dt * taus)
    nr, ni = pw_r[1:2] - 1.0, pw_i[1:2]
    den = lr * lr + li * li
    cf_r = (nr * lr + ni * li) / den
    cf_i = (ni * lr - nr * li) / den
    bb_r = cf_r * bt_ref[0] - cf_i * bt_ref[1]
    bb_i = cf_r * bt_ref[1] + cf_i * bt_ref[0]
    is_fwd = lax.broadcasted_iota(jnp.int32, (H, LANES), 1) < S5_STATE

    def powers(tau_f, tau_b):
        pick = lambda pw, s: jnp.where(is_fwd, jnp.broadcast_to(pw[tau_f(s):tau_f(s) + 1], (H, LANES)),
                                       jnp.broadcast_to(pw[tau_b(s):tau_b(s) + 1], (H, LANES)))
        return (jnp.concatenate([pick(pw_r, s) for s in range(T)], axis=0),
                jnp.concatenate([pick(pw_i, s) for s in range(T)], axis=0))

    tile = lambda a: jnp.concatenate([a] * T, axis=0)
    bbr, bbi, cr, ci = tile(bb_r), tile(bb_i), tile(ct_ref[0]), tile(ct_ref[1])

    er, ei = powers(lambda s: T - 1 - s, lambda s: s)
    sin_ref[:, 0:LANES] = (er * bbr - ei * bbi).astype(sin_ref.dtype)
    sin_ref[:, LANES:] = (er * bbi + ei * bbr).astype(sin_ref.dtype)

    er, ei = powers(lambda t: t + 1, lambda t: T - t)
    gt = jnp.concatenate([er * cr - ei * ci, -(er * ci + ei * cr)], axis=1)
    g_ref[...] = gt.T.astype(g_ref.dtype)

    er, ei = powers(lambda a: a, lambda a: T - 1 - a)
    cp = jnp.concatenate([er * cr - ei * ci, er * ci + ei * cr], axis=1)
    zero = jnp.zeros_like(bb_r)
    lhs = jnp.concatenate([jnp.concatenate([jnp.where(is_fwd, bb_r, zero), jnp.where(is_fwd, -bb_i, zero)], axis=1),
                           jnp.concatenate([jnp.where(is_fwd, zero, bb_r), jnp.where(is_fwd, zero, -bb_i)], axis=1)],
                          axis=0)
    kr = lax.dot_general(lhs, cp, (((1,), (1,)), ((), ())), preferred_element_type=F32,
                         precision=lax.Precision.HIGHEST)
    pad = jnp.zeros((H, S5_CW), F32)
    wide_f = jnp.concatenate([pad, kr[0:H]], axis=1)
    wide_b = jnp.concatenate([kr[H:], pad], axis=1)
    lane = lax.broadcasted_iota(jnp.int32, (H, S5_CW), 1)
    row = lax.broadcasted_iota(jnp.int32, (H, S5_CW), 0)
    skip = jnp.broadcast_to(d_ref[...], (H, S5_CW))
    for s in range(T):
        blk_f = pltpu.roll(wide_f, H * s, 1)[:, S5_CW:] if s else wide_f[:, S5_CW:]
        shift_b = (2 * S5_CW - H * (T - 1 - s)) % (2 * S5_CW)
        blk_b = (pltpu.roll(wide_b, shift_b, 1) if shift_b else wide_b)[:, :S5_CW]
        diag = jnp.where(lane == H * s + row, skip, 0.0)
        m_ref[H * s:H * (s + 1), :] = (blk_f + blk_b + diag).astype(m_ref.dtype)
    a_ref[0:1, :] = pw_r[T:T + 1]
    a_ref[1:2, :] = pw_i[T:T + 1]


def _s5_ops_call(lam, ldt, bt, ct, dt_tiled):
    groups = lam.shape[0]
    mat = lambda: pl.BlockSpec((None, S5_CW, S5_CW), lambda g: (g, 0, 0))
    vec = lambda a: pl.BlockSpec((None,) + a.shape[1:], lambda g: (g,) + (0,) * (a.ndim - 1))
    mshape = jax.ShapeDtypeStruct((groups, S5_CW, S5_CW), BF16)
    return pl.pallas_call(
        _s5_ops_kernel,
        out_shape=(mshape, mshape, mshape, jax.ShapeDtypeStruct((groups, 2, LANES), F32)),
        grid=(groups,),
        in_specs=[vec(lam), vec(ldt), vec(bt), vec(ct), vec(dt_tiled)],
        out_specs=(mat(), mat(), mat(), pl.BlockSpec((None, 2, LANES), lambda g: (g, 0, 0))),
        compiler_params=pltpu.CompilerParams(dimension_semantics=("arbitrary",)),
        name="s5_ops",
    )(lam, ldt, bt, ct, dt_tiled)


def _s5_operators(lam_re, lam_im, log_dt, b_re, b_im, c_re, c_im, d_skip):
    G, P, H = S5_GROUPS, S5_STATE, S5_GROUP
    lam = jnp.stack([lam_re, lam_im]).astype(F32).transpose(2, 0, 1, 3).reshape(G, 2, 2 * P)
    ldt = jnp.repeat(log_dt.astype(F32).T, P, axis=1).reshape(G, 1, 2 * P)
    bt = jnp.stack([b_re, b_im]).astype(F32).transpose(2, 0, 4, 1, 3).reshape(G, 2, H, 2 * P)
    ct = jnp.stack([c_re, c_im]).astype(F32).transpose(2, 0, 3, 1, 4).reshape(G, 2, H, 2 * P)
    dt_tiled = jnp.tile(d_skip.astype(F32).reshape(G, 1, H), (1, 1, S5_CHUNK))
    return _s5_ops_call(lam, ldt, bt, ct, dt_tiled)


def _final_kernel(x_ref, mod_ref, ng_ref, wg_ref, ya_ref, yp_ref, permt_ref, wglu_ref, bglu_ref, wa_ref, wb_ref,
                  wo_ref, fg_ref, o_ref):
    bsz, tt, _ = x_ref.shape
    rows = bsz * tt
    x3 = x_ref[...]
    xn = _modulated_norm(x3, mod_ref, ng_ref).reshape(rows, D_MODEL).astype(BF16)
    gates = jnp.dot(xn, wg_ref[...], preferred_element_type=F32)
    o1, o2, o3 = ATTN_WIDTH, ATTN_WIDTH + S5_WIDTH, ATTN_WIDTH + S5_WIDTH + D_MODEL
    slabs = [jnp.concatenate([yp_ref[:, blk * BLOCK_W + t * LANES: blk * BLOCK_W + (t + 1) * LANES]
                              for blk in range(N_BLOCKS)], axis=1) for t in range(S5_CHUNK)]
    y = jnp.dot(permt_ref[...], jnp.concatenate(slabs, axis=0), preferred_element_type=F32)
    z = y * (0.5 * (1.0 + jnp.tanh(0.7978845608028654 * (y + 0.044715 * (y * y * y)))))
    zz = z * jax.nn.sigmoid(jnp.dot(z.astype(BF16), wglu_ref[...], preferred_element_type=F32) + bglu_ref[...])
    ya = ya_ref[...].reshape(rows, ATTN_WIDTH).astype(F32)
    ta = (ya * _silu(gates[:, 0:o1])).astype(BF16)
    tb = (zz * _silu(gates[:, o1:o2])).astype(BF16)
    pa = jnp.dot(ta, wa_ref[...], preferred_element_type=F32)
    pb = jnp.dot(tb, wb_ref[...], preferred_element_type=F32)
    mix = jax.nn.sigmoid(gates[:, o2:o3]) * pa + jax.nn.sigmoid(gates[:, o3:]) * pb
    o = jnp.dot(mix.astype(BF16), wo_ref[...], preferred_element_type=F32).reshape(bsz, tt, D_MODEL)
    h = x3 + mod_ref[:, :, 2 * D_MODEL:] * o
    ms = jnp.mean(h * h, axis=-1, keepdims=True)
    o_ref[...] = h * lax.rsqrt(ms + EPS) * fg_ref[...]


def _final_call(x, mod3, ng, wg, ya, yp, permt, wglu, bglu, wa, wb, wo, fg):
    bsz, n, _ = x.shape
    tt = TOK_TILE
    up_rows = tt // S5_CHUNK * bsz
    tok = lambda width: pl.BlockSpec((bsz, tt, width), lambda i: (0, i, 0))
    c = lambda shape: _const_spec(shape, 1)
    return pl.pallas_call(
        _final_kernel,
        out_shape=jax.ShapeDtypeStruct((bsz, n, D_MODEL), F32),
        grid=(n // tt,),
        in_specs=[tok(D_MODEL),
                  pl.BlockSpec((bsz, 1, 3 * D_MODEL), lambda i: (0, 0, 0)),
                  c((1, D_MODEL)), c(wg.shape),
                  tok(ATTN_WIDTH),
                  pl.BlockSpec((up_rows, N_BLOCKS * BLOCK_W), lambda i: (i, 0)),
                  c(permt.shape), c(wglu.shape), c((1, S5_WIDTH)), c(wa.shape), c(wb.shape), c(wo.shape),
                  c((1, D_MODEL))],
        out_specs=tok(D_MODEL),
        compiler_params=pltpu.CompilerParams(dimension_semantics=("arbitrary",), vmem_limit_bytes=VMEM_LIMIT),
        name="final",
    )(x, mod3, ng, wg, ya, yp, permt, wglu, bglu, wa, wb, wo, fg)


def _rope_tables(n):
    rows = n // GRID_W
    row_ids = np.repeat(np.arange(rows, dtype=np.float64), GRID_W)
    col_ids = np.tile(np.arange(GRID_W, dtype=np.float64), rows)
    freqs = ROPE_THETA ** (-np.arange(ROPE_FREQS, dtype=np.float64) / ROPE_FREQS)
    ang_r, ang_c = row_ids[:, None] * freqs, col_ids[:, None] * freqs
    cos = np.concatenate([np.cos(ang_r)] * 2 + [np.cos(ang_c)] * 2, axis=1)
    sin = np.concatenate([-np.sin(ang_r), np.sin(ang_r), -np.sin(ang_c), np.sin(ang_c)], axis=1)
    reps = LANES // HEAD_DIM
    return (jnp.asarray(np.tile(cos, (1, reps)).astype(np.float32)),
            jnp.asarray(np.tile(sin, (1, reps)).astype(np.float32)))


def _one_hot(match):
    return jnp.asarray(np.ascontiguousarray(match).astype(BF16))


def _block_ones(width):
    idx = np.arange(width) // HEAD_DIM
    return _one_hot(idx[:, None] == idx[None, :])


def _row_perm(bsz, tt, transpose=False):
    chunks = tt // S5_CHUNK
    r = np.arange(bsz * tt)
    t, pc, b = r // (chunks * bsz), (r // bsz) % chunks, r % bsz
    src = b * tt + pc * S5_CHUNK + t
    match = src[:, None] == np.arange(bsz * tt)[None, :]
    return _one_hot(match.T if transpose else match)


def _lane_perm(transpose=False):
    r = np.arange(BLOCK_W)
    t, j, h = r // LANES, (r % LANES) // S5_GROUP, r % S5_GROUP
    dst = j * S5_CW + t * S5_GROUP + h
    match = dst[:, None] == np.arange(BLOCK_W)[None, :]
    return _one_hot(match.T if transpose else match)


def _kv_expand():
    col = np.arange(N_KV_HEADS * REP_W)
    src = (col // REP_W) * HEAD_DIM + col % HEAD_DIM
    return _one_hot(np.arange(KV_WIDTH)[:, None] == src[None, :])


def kernel(x, c, ctx, c_ctx, norm_g, w_ada, b_ada, w_in, q_norm_g, k_norm_g, s5_lam_re, s5_lam_im, s5_log_dt,
           s5_b_re, s5_b_im, s5_c_re, s5_c_im, s5_d, w_glu, b_glu, w_branch_attn, w_branch_s5, w_out,
           final_norm_g):
    assert w_in.shape[0] == 1, "single-layer block"
    bsz, n, _ = x.shape
    n_ctx = ctx.shape[1]
    assert n % TOK_TILE == 0 and n_ctx % TOK_TILE == 0 and bsz == SUBLANES

    ada_rows = 2 * SUBLANES
    cc = jnp.concatenate([c, c_ctx[None], jnp.zeros((ada_rows - bsz - 1, D_MODEL), F32)], axis=0)
    mod = _ada_call(cc, w_ada[0], b_ada[0][None])
    mod3 = mod[:bsz].reshape(bsz, 1, 3 * D_MODEL)
    mod_ctx3 = mod[bsz:bsz + 1].reshape(1, 1, 3 * D_MODEL)

    offs = [0]
    for s in IN_SIZES:
        offs.append(offs[-1] + s)
    w_bf = w_in[0].astype(BF16)
    w_pre = jnp.concatenate([w_bf[:, offs[0]:offs[3]], w_bf[:, offs[4]:offs[5]]], axis=1)
    w_pre_ctx = w_pre[:, ATTN_WIDTH:]
    w_gates = jnp.concatenate([w_bf[:, offs[3]:offs[4]], w_bf[:, offs[5]:]], axis=1)

    ng = norm_g[0][None]
    qg = jnp.tile(q_norm_g[0], N_HEADS)[None]
    kg = jnp.tile(k_norm_g[0], N_KV_HEADS)[None]
    onesq, onesk = _block_ones(ATTN_WIDTH), _block_ones(KV_WIDTH)
    cos, sin = _rope_tables(n)
    row_perm = _row_perm(bsz, TOK_TILE)
    lane_perm = _lane_perm()

    q, k_lat, v_lat, up = _pre_lat_call(x, mod3, ng, w_pre, onesq, onesk, qg, kg, cos, sin, row_perm, n_ctx)
    k_ctx, v_ctx, up = _pre_ctx_call(ctx, mod_ctx3, ng, w_pre_ctx, onesk, kg, row_perm, up)

    k_all = jnp.concatenate([k_ctx, k_lat], axis=1)
    v_all = jnp.concatenate([v_ctx, v_lat], axis=1)
    y_attn = _attn_call(q, k_all, v_all, _kv_expand(), tq=256)

    ug = _perm_in_call(up, lane_perm, tr=384)
    m, s_in, gmat, a16 = _s5_operators(s5_lam_re[0], s5_lam_im[0], s5_log_dt[0], s5_b_re[0], s5_b_im[0],
                                       s5_c_re[0], s5_c_im[0], s5_d[0])
    yg = _s5_call(ug, m, s_in, gmat, a16, n_ctx // S5_CHUNK, gb=GROUPS_PER_BLOCK)
    yp = _perm_out_call(yg, _lane_perm(transpose=True), tr=256)

    return _final_call(x, mod3, ng, w_gates, y_attn, yp, _row_perm(bsz, TOK_TILE, transpose=True),
                       w_glu[0].astype(BF16), b_glu[0][None],
                       w_branch_attn[0].astype(BF16), w_branch_s5[0].astype(BF16), w_out[0].astype(BF16),
                       final_norm_g[None])
```

```python
import functools

import numpy as np
import jax
import jax.numpy as jnp
from jax import lax
from jax.experimental import pallas as pl
from jax.experimental.pallas import tpu as pltpu

D_MODEL = 1024
GRID_W = 64
N_HEADS = 8
N_KV_HEADS = 2
HEAD_DIM = 64
GQA_REP = N_HEADS // N_KV_HEADS
ATTN_WIDTH = N_HEADS * HEAD_DIM
KV_WIDTH = N_KV_HEADS * HEAD_DIM
ATTN_SCALE = HEAD_DIM ** -0.5
ROPE_THETA = 10000.0
ROPE_FREQS = HEAD_DIM // 4
S5_WIDTH = 512
S5_GROUP = 16
S5_GROUPS = S5_WIDTH // S5_GROUP
S5_STATE = 64
EPS = 1e-6
IN_SIZES = (ATTN_WIDTH, KV_WIDTH, KV_WIDTH, ATTN_WIDTH, S5_WIDTH, S5_WIDTH, D_MODEL, D_MODEL)

LANES = 128
SUBLANES = 8
S5_CHUNK = 16
S5_CW = S5_CHUNK * S5_GROUP
GROUPS_PER_BLOCK = LANES // S5_GROUP
N_BLOCKS = S5_WIDTH // LANES
BLOCK_W = S5_CHUNK * LANES
TOK_TILE = 64
REP_W = GQA_REP * HEAD_DIM
VMEM_LIMIT = 56 * 1024 * 1024

F32 = jnp.float32
BF16 = jnp.bfloat16


def _silu(t):
    return t * jax.nn.sigmoid(t)


def _modulated_norm(x3, mod_ref, ng_ref):
    ms = jnp.mean(x3 * x3, axis=-1, keepdims=True)
    y = x3 * lax.rsqrt(ms + EPS) * ng_ref[...]
    return y * (1.0 + mod_ref[:, :, D_MODEL:2 * D_MODEL]) + mod_ref[:, :, 0:D_MODEL]


def _head_rmsnorm(t, ones_ref, g_ref):
    ss = jnp.dot((t * t).astype(BF16), ones_ref[...], preferred_element_type=F32)
    return t * lax.rsqrt(ss * (1.0 / HEAD_DIM) + EPS) * g_ref[...]


def _rope(t, cos, sin_signed):
    rows = t.shape[0]
    lane = lax.broadcasted_iota(jnp.int32, (rows, LANES), 1)
    first = (lane & ROPE_FREQS) == 0
    outs = []
    for j in range(t.shape[1] // LANES):
        blk = t[:, j * LANES:(j + 1) * LANES]
        partner = jnp.where(first, pltpu.roll(blk, LANES - ROPE_FREQS, 1), pltpu.roll(blk, ROPE_FREQS, 1))
        outs.append(blk * cos + partner * sin_signed)
    return outs[0] if len(outs) == 1 else jnp.concatenate(outs, axis=1)


def _const_spec(shape, grid_rank):
    zeros = (0,) * len(shape)
    return pl.BlockSpec(shape, lambda *_: zeros)


def _ada_kernel(c_ref, w_ref, b_ref, o_ref):
    s = _silu(c_ref[...])
    o_ref[...] = jnp.dot(s, w_ref[...], preferred_element_type=F32,
                         precision=lax.Precision.HIGHEST) + b_ref[...]


def _ada_call(cc, w, b):
    rows, n = cc.shape[0], w.shape[1]
    tn = 512
    return pl.pallas_call(
        _ada_kernel,
        out_shape=jax.ShapeDtypeStruct((rows, n), F32),
        grid=(n // tn,),
        in_specs=[pl.BlockSpec((rows, D_MODEL), lambda j: (0, 0)),
                  pl.BlockSpec((D_MODEL, tn), lambda j: (0, j)),
                  pl.BlockSpec((1, tn), lambda j: (0, j))],
        out_specs=pl.BlockSpec((rows, tn), lambda j: (0, j)),
        compiler_params=pltpu.CompilerParams(dimension_semantics=("arbitrary",)),
        name="ada",
    )(cc, w, b)


def _store_chunk_major(u, perm_ref, up_out):
    r = jnp.dot(perm_ref[...], u.astype(BF16), preferred_element_type=F32).astype(BF16)
    rows = up_out.shape[0]
    for t in range(S5_CHUNK):
        for blk in range(N_BLOCKS):
            up_out[:, blk * BLOCK_W + t * LANES: blk * BLOCK_W + (t + 1) * LANES] = (
                r[t * rows:(t + 1) * rows, blk * LANES:(blk + 1) * LANES])


def _pre_kernel(x_ref, c_ref, mod_ref, modc_ref, ng_ref, w_ref, onesq_ref, onesk_ref, qg_ref, kg_ref, cos_ref, sin_ref,
                perm_ref, q_out, k_out, v_out, up_out, *, ctx_steps):
    bsz, tt, _ = x_ref.shape
    o1, o2, o3 = ATTN_WIDTH, ATTN_WIDTH + KV_WIDTH, ATTN_WIDTH + 2 * KV_WIDTH
    step = pl.program_id(0)

    @pl.when(step < ctx_steps)
    def _():
        xn = _modulated_norm(c_ref[...], modc_ref, ng_ref).reshape(bsz * tt, D_MODEL).astype(BF16)
        p = jnp.dot(xn, w_ref[:, o1:], preferred_element_type=F32)
        k = _head_rmsnorm(p[:, 0:KV_WIDTH], onesk_ref, kg_ref)
        k_out[...] = k.astype(BF16).reshape(bsz, tt, KV_WIDTH)
        v_out[...] = p[:, KV_WIDTH:2 * KV_WIDTH].astype(BF16).reshape(bsz, tt, KV_WIDTH)
        _store_chunk_major(p[:, 2 * KV_WIDTH:], perm_ref, up_out)

    @pl.when(step >= ctx_steps)
    def _():
        xn = _modulated_norm(x_ref[...], mod_ref, ng_ref).reshape(bsz * tt, D_MODEL).astype(BF16)
        p = jnp.dot(xn, w_ref[...], preferred_element_type=F32)
        cos = jnp.concatenate([cos_ref[...]] * bsz, axis=0)
        sin = jnp.concatenate([sin_ref[...]] * bsz, axis=0)
        q = _rope(_head_rmsnorm(p[:, 0:o1], onesq_ref, qg_ref), cos, sin)
        k = _rope(_head_rmsnorm(p[:, o1:o2], onesk_ref, kg_ref), cos, sin)
        q_out[...] = (q * ATTN_SCALE).astype(BF16).reshape(bsz, tt, ATTN_WIDTH)
        k_out[...] = k.astype(BF16).reshape(bsz, tt, KV_WIDTH)
        v_out[...] = p[:, o2:o3].astype(BF16).reshape(bsz, tt, KV_WIDTH)
        _store_chunk_major(p[:, o3:], perm_ref, up_out)


def _pre_call(x, ctx, mod3, mod_ctx3, ng, w, onesq, onesk, qg, kg, cos, sin, perm):
    bsz, n, _ = x.shape
    n_ctx = ctx.shape[1]
    tt = TOK_TILE
    up_rows = tt // S5_CHUNK * bsz
    ctx_steps = n_ctx // tt
    lat = lambda i: jnp.maximum(i - ctx_steps, 0)
    c = lambda shape: _const_spec(shape, 1)
    return pl.pallas_call(
        functools.partial(_pre_kernel, ctx_steps=ctx_steps),
        out_shape=(jax.ShapeDtypeStruct((bsz, n, ATTN_WIDTH), BF16),
                   jax.ShapeDtypeStruct((bsz, n_ctx + n, KV_WIDTH), BF16),
                   jax.ShapeDtypeStruct((bsz, n_ctx + n, KV_WIDTH), BF16),
                   jax.ShapeDtypeStruct(((n_ctx + n) // S5_CHUNK * bsz, N_BLOCKS * BLOCK_W), BF16)),
        grid=(ctx_steps + n // tt,),
        in_specs=[pl.BlockSpec((bsz, tt, D_MODEL), lambda i: (0, lat(i), 0)),
                  pl.BlockSpec((bsz, tt, D_MODEL), lambda i: (0, jnp.minimum(i, ctx_steps - 1), 0)),
                  c((bsz, 1, 3 * D_MODEL)), c((1, 1, 3 * D_MODEL)),
                  c((1, D_MODEL)), c(w.shape), c(onesq.shape), c(onesk.shape),
                  c((1, ATTN_WIDTH)), c((1, KV_WIDTH)),
                  pl.BlockSpec((tt, LANES), lambda i: (lat(i), 0)),
                  pl.BlockSpec((tt, LANES), lambda i: (lat(i), 0)),
                  c(perm.shape)],
        out_specs=(pl.BlockSpec((bsz, tt, ATTN_WIDTH), lambda i: (0, lat(i), 0)),
                   pl.BlockSpec((bsz, tt, KV_WIDTH), lambda i: (0, i, 0)),
                   pl.BlockSpec((bsz, tt, KV_WIDTH), lambda i: (0, i, 0)),
                   pl.BlockSpec((up_rows, N_BLOCKS * BLOCK_W), lambda i: (i, 0))),
        compiler_params=pltpu.CompilerParams(dimension_semantics=("arbitrary",), vmem_limit_bytes=VMEM_LIMIT),
        name="pre",
    )(x, ctx, mod3, mod_ctx3, ng, w, onesq, onesk, qg, kg, cos, sin, perm)


ONES_ROWS = 16


def _attn_kernel(q_ref, k_ref, v_ref, o_ref, vt_ref):
    nk = k_ref.shape[0]

    @pl.when(pl.program_id(1) == 0)
    def _():
        vt = v_ref[...].astype(F32).T
        ones = jnp.ones((ONES_ROWS, nk), F32)
        for g in range(N_KV_HEADS):
            vt_ref[g] = jnp.concatenate([vt[g * HEAD_DIM:(g + 1) * HEAD_DIM], ones], axis=0).astype(BF16)

    tq = q_ref.shape[0]
    qt = q_ref[...].astype(F32).T.astype(BF16)
    zeros = jnp.zeros((HEAD_DIM, tq), BF16)
    keys = k_ref[...]
    outs = []
    for h in range(N_HEADS):
        g = h // GQA_REP
        qh = qt[h * HEAD_DIM:(h + 1) * HEAD_DIM]
        wq = jnp.concatenate([qh, zeros] if g == 0 else [zeros, qh], axis=0)
        st = jnp.dot(keys, wq, preferred_element_type=F32)
        m = jnp.max(st, axis=0, keepdims=True)
        pt = jnp.exp(st - m).astype(BF16)
        oa = jnp.dot(vt_ref[g], pt, preferred_element_type=F32)
        outs.append(oa[0:HEAD_DIM] / oa[HEAD_DIM:HEAD_DIM + 1])
    o_ref[...] = jnp.concatenate(outs, axis=0).T.astype(o_ref.dtype)


def _attn_call(q, k_all, v_all, tq):
    assert N_KV_HEADS == 2
    bsz, n, _ = q.shape
    nk = k_all.shape[1]
    return pl.pallas_call(
        _attn_kernel,
        out_shape=jax.ShapeDtypeStruct((bsz, n, ATTN_WIDTH), BF16),
        grid=(bsz, n // tq),
        in_specs=[pl.BlockSpec((None, tq, ATTN_WIDTH), lambda b, i: (b, i, 0)),
                  pl.BlockSpec((None, nk, KV_WIDTH), lambda b, i: (b, 0, 0)),
                  pl.BlockSpec((None, nk, KV_WIDTH), lambda b, i: (b, 0, 0))],
        out_specs=pl.BlockSpec((None, tq, ATTN_WIDTH), lambda b, i: (b, i, 0)),
        scratch_shapes=[pltpu.VMEM((N_KV_HEADS, HEAD_DIM + ONES_ROWS, nk), BF16)],
        compiler_params=pltpu.CompilerParams(dimension_semantics=("arbitrary", "arbitrary"),
                                             vmem_limit_bytes=VMEM_LIMIT),
        name="attn",
    )(q, k_all, v_all)


def _perm_in_kernel(s_ref, p_ref, o_ref):
    r = jnp.dot(s_ref[...], p_ref[...], preferred_element_type=F32)
    for j in range(GROUPS_PER_BLOCK):
        o_ref[j] = r[:, j * S5_CW:(j + 1) * S5_CW].astype(o_ref.dtype)


def _perm_in_call(up, pmat, tr):
    rows = up.shape[0]
    return pl.pallas_call(
        _perm_in_kernel,
        out_shape=jax.ShapeDtypeStruct((S5_GROUPS, rows, S5_CW), BF16),
        grid=(N_BLOCKS, rows // tr),
        in_specs=[pl.BlockSpec((tr, BLOCK_W), lambda s, i: (i, s)),
                  _const_spec(pmat.shape, 2)],
        out_specs=pl.BlockSpec((GROUPS_PER_BLOCK, tr, S5_CW), lambda s, i: (s, i, 0)),
        compiler_params=pltpu.CompilerParams(dimension_semantics=("arbitrary", "arbitrary"),
                                             vmem_limit_bytes=VMEM_LIMIT),
        name="perm_in",
    )(up, pmat)


def _perm_out_kernel(y_ref, q_ref, o_ref):
    ycat = jnp.concatenate([y_ref[j] for j in range(GROUPS_PER_BLOCK)], axis=1)
    o_ref[...] = jnp.dot(ycat, q_ref[...], preferred_element_type=F32).astype(o_ref.dtype)


def _perm_out_call(yg, qmat, tr):
    rows = yg.shape[1]
    return pl.pallas_call(
        _perm_out_kernel,
        out_shape=jax.ShapeDtypeStruct((rows, N_BLOCKS * BLOCK_W), BF16),
        grid=(N_BLOCKS, rows // tr),
        in_specs=[pl.BlockSpec((GROUPS_PER_BLOCK, tr, S5_CW), lambda s, i: (s, i, 0)),
                  _const_spec(qmat.shape, 2)],
        out_specs=pl.BlockSpec((tr, BLOCK_W), lambda s, i: (i, s)),
        compiler_params=pltpu.CompilerParams(dimension_semantics=("arbitrary", "arbitrary"),
                                             vmem_limit_bytes=VMEM_LIMIT),
        name="perm_out",
    )(yg, qmat)


def _s5_kernel(u_ref, m_ref, sin_ref, g_ref, a_ref, y_ref, buf_ref, *, n_ctx_chunks, n_chunks):
    half = S5_STATE
    gb = u_ref.shape[0]
    for j in range(gb):
        buf_ref[j] = jnp.dot(u_ref[j], sin_ref[j], preferred_element_type=F32)
    a_re = [jnp.broadcast_to(a_ref[j, 0:1, :], (SUBLANES, LANES)) for j in range(gb)]
    a_im = [jnp.broadcast_to(a_ref[j, 1:2, :], (SUBLANES, LANES)) for j in range(gb)]
    is_fwd = lax.broadcasted_iota(jnp.int32, (SUBLANES, LANES), 1) < half

    def step(k, carry):
        pos_b = jnp.where(k < n_ctx_chunks, n_ctx_chunks - 1 - k, n_chunks + n_ctx_chunks - 1 - k)
        rf = pl.multiple_of(k * SUBLANES, SUBLANES)
        rb = pl.multiple_of(pos_b * SUBLANES, SUBLANES)
        out = []
        for j in range(gb):
            h_re, h_im = carry[2 * j], carry[2 * j + 1]
            xf = buf_ref[j, pl.ds(rf, SUBLANES), :]
            xb = buf_ref[j, pl.ds(rb, SUBLANES), :]
            buf_ref[j, pl.ds(rf, SUBLANES), 0:half] = h_re[:, 0:half]
            buf_ref[j, pl.ds(rf, SUBLANES), 2 * half:3 * half] = h_im[:, 0:half]
            buf_ref[j, pl.ds(rb, SUBLANES), half:2 * half] = h_re[:, half:]
            buf_ref[j, pl.ds(rb, SUBLANES), 3 * half:] = h_im[:, half:]
            x_re = jnp.where(is_fwd, xf[:, 0:LANES], xb[:, 0:LANES])
            x_im = jnp.where(is_fwd, xf[:, LANES:], xb[:, LANES:])
            out.append(a_re[j] * h_re - a_im[j] * h_im + x_re)
            out.append(a_re[j] * h_im + a_im[j] * h_re + x_im)
        return tuple(out)

    zero = jnp.zeros((SUBLANES, LANES), F32)
    lax.fori_loop(0, n_chunks, step, (zero,) * (2 * gb))

    r0 = n_ctx_chunks * SUBLANES
    for j in range(gb):
        y = jnp.dot(u_ref[j, r0:, :], m_ref[j], preferred_element_type=F32)
        y = y + jnp.dot(buf_ref[j, r0:, :].astype(BF16), g_ref[j], preferred_element_type=F32)
        y_ref[j] = y.astype(y_ref.dtype)


def _s5_call(ug, m, sin, gmat, a16, n_ctx_chunks, gb):
    groups, rows, _ = ug.shape
    n_chunks = rows // SUBLANES
    out_rows = rows - n_ctx_chunks * SUBLANES
    mat = lambda: pl.BlockSpec((gb, S5_CW, S5_CW), lambda g: (g, 0, 0))
    return pl.pallas_call(
        functools.partial(_s5_kernel, n_ctx_chunks=n_ctx_chunks, n_chunks=n_chunks),
        out_shape=jax.ShapeDtypeStruct((groups, out_rows, S5_CW), BF16),
        grid=(groups // gb,),
        in_specs=[pl.BlockSpec((gb, rows, S5_CW), lambda g: (g, 0, 0)),
                  mat(), mat(), mat(),
                  pl.BlockSpec((gb, 2, LANES), lambda g: (g, 0, 0))],
        out_specs=pl.BlockSpec((gb, out_rows, S5_CW), lambda g: (g, 0, 0)),
        scratch_shapes=[pltpu.VMEM((gb, rows, S5_CW), F32)],
        compiler_params=pltpu.CompilerParams(dimension_semantics=("arbitrary",),
                                             vmem_limit_bytes=VMEM_LIMIT),
        name="s5",
    )(ug, m, sin, gmat, a16)


def _s5_ops_kernel(lam_ref, ldt_ref, bt_ref, ct_ref, d_ref, m_ref, sin_ref, g_ref, a_ref):
    T, H = S5_CHUNK, S5_GROUP
    lr = jnp.minimum(lam_ref[0:1, :], -1e-4)
    li = lam_ref[1:2, :]
    dt = jnp.exp(ldt_ref[...])
    taus = lax.broadcasted_iota(jnp.int32, (3 * SUBLANES, LANES), 0).astype(F32)
    mag = jnp.exp(lr * dt * taus)
    pw_r = mag * jnp.cos(li * dt * taus)
    pw_i = mag * jnp.sin(li * dt * taus)
    nr, ni = pw_r[1:2] - 1.0, pw_i[1:2]
    den = lr * lr + li * li
    cf_r = (nr * lr + ni * li) / den
    cf_i = (ni * lr - nr * li) / den
    bb_r = cf_r * bt_ref[0] - cf_i * bt_ref[1]
    bb_i = cf_r * bt_ref[1] + cf_i * bt_ref[0]
    is_fwd = lax.broadcasted_iota(jnp.int32, (H, LANES), 1) < S5_STATE

    def powers(tau_f, tau_b):
        pick = lambda pw, s: jnp.where(is_fwd, jnp.broadcast_to(pw[tau_f(s):tau_f(s) + 1], (H, LANES)),
                                       jnp.broadcast_to(pw[tau_b(s):tau_b(s) + 1], (H, LANES)))
        return (jnp.concatenate([pick(pw_r, s) for s in range(T)], axis=0),
                jnp.concatenate([pick(pw_i, s) for s in range(T)], axis=0))

    tile = lambda a: jnp.concatenate([a] * T, axis=0)
    bbr, bbi, cr, ci = tile(bb_r), tile(bb_i), tile(ct_ref[0]), tile(ct_ref[1])

    er, ei = powers(lambda s: T - 1 - s, lambda s: s)
    sin_ref[:, 0:LANES] = (er * bbr - ei * bbi).astype(sin_ref.dtype)
    sin_ref[:, LANES:] = (er * bbi + ei * bbr).astype(sin_ref.dtype)

    er, ei = powers(lambda t: t + 1, lambda t: T - t)
    gt = jnp.concatenate([er * cr - ei * ci, -(er * ci + ei * cr)], axis=1)
    g_ref[...] = gt.T.astype(g_ref.dtype)

    er, ei = powers(lambda a: a, lambda a: T - 1 - a)
    cp = jnp.concatenate([er * cr - ei * ci, er * ci + ei * cr], axis=1)
    zero = jnp.zeros_like(bb_r)
    lhs = jnp.concatenate([jnp.concatenate([jnp.where(is_fwd, bb_r, zero), jnp.where(is_fwd, -bb_i, zero)], axis=1),
                           jnp.concatenate([jnp.where(is_fwd, zero, bb_r), jnp.where(is_fwd, zero, -bb_i)], axis=1)],
                          axis=0)
    kr = lax.dot_general(lhs, cp, (((1,), (1,)), ((), ())), preferred_element_type=F32,
                         precision=lax.Precision.HIGHEST)
    pad = jnp.zeros((H, S5_CW), F32)
    wide_f = jnp.concatenate([pad, kr[0:H]], axis=1)
    wide_b = jnp.concatenate([kr[H:], pad], axis=1)
    lane = lax.broadcasted_iota(jnp.int32, (H, S5_CW), 1)
    row = lax.broadcasted_iota(jnp.int32, (H, S5_CW), 0)
    skip = jnp.broadcast_to(d_ref[...], (H, S5_CW))
    for s in range(T):
        blk_f = pltpu.roll(wide_f, H * s, 1)[:, S5_CW:] if s else wide_f[:, S5_CW:]
        shift_b = (2 * S5_CW - H * (T - 1 - s)) % (2 * S5_CW)
        blk_b = (pltpu.roll(wide_b, shift_b, 1) if shift_b else wide_b)[:, :S5_CW]
        diag = jnp.where(lane == H * s + row, skip, 0.0)
        m_ref[H * s:H * (s + 1), :] = (blk_f + blk_b + diag).astype(m_ref.dtype)
    a_ref[0:1, :] = pw_r[T:T + 1]
    a_ref[1:2, :] = pw_i[T:T + 1]


def _s5_ops_call(lam, ldt, bt, ct, dt_tiled):
    groups = lam.shape[0]
    mat = lambda: pl.BlockSpec((None, S5_CW, S5_CW), lambda g: (g, 0, 0))
    vec = lambda a: pl.BlockSpec((None,) + a.shape[1:], lambda g: (g,) + (0,) * (a.ndim - 1))
    mshape = jax.ShapeDtypeStruct((groups, S5_CW, S5_CW), BF16)
    return pl.pallas_call(
        _s5_ops_kernel,
        out_shape=(mshape, mshape, mshape, jax.ShapeDtypeStruct((groups, 2, LANES), F32)),
        grid=(groups,),
        in_specs=[vec(lam), vec(ldt), vec(bt), vec(ct), vec(dt_tiled)],
        out_specs=(mat(), mat(), mat(), pl.BlockSpec((None, 2, LANES), lambda g: (g, 0, 0))),
        compiler_params=pltpu.CompilerParams(dimension_semantics=("arbitrary",)),
        name="s5_ops",
    )(lam, ldt, bt, ct, dt_tiled)


def _s5_operators(lam_re, lam_im, log_dt, b_re, b_im, c_re, c_im, d_skip):
    G, P, H = S5_GROUPS, S5_STATE, S5_GROUP
    lam = jnp.stack([lam_re, lam_im]).astype(F32).transpose(2, 0, 1, 3).reshape(G, 2, 2 * P)
    ldt = jnp.repeat(log_dt.astype(F32).T, P, axis=1).reshape(G, 1, 2 * P)
    bt = jnp.stack([b_re, b_im]).astype(F32).transpose(2, 0, 4, 1, 3).reshape(G, 2, H, 2 * P)
    ct = jnp.stack([c_re, c_im]).astype(F32).transpose(2, 0, 3, 1, 4).reshape(G, 2, H, 2 * P)
    dt_tiled = jnp.tile(d_skip.astype(F32).reshape(G, 1, H), (1, 1, S5_CHUNK))
    return _s5_ops_call(lam, ldt, bt, ct, dt_tiled)


def _final_kernel(x_ref, mod_ref, ng_ref, wg_ref, ya_ref, yp_ref, permt_ref, wglu_ref, bglu_ref, wa_ref, wb_ref,
                  wo_ref, fg_ref, o_ref):
    bsz, tt, _ = x_ref.shape
    rows = bsz * tt
    x3 = x_ref[...]
    xn = _modulated_norm(x3, mod_ref, ng_ref).reshape(rows, D_MODEL).astype(BF16)
    gates = jnp.dot(xn, wg_ref[...], preferred_element_type=F32)
    o1, o2, o3 = ATTN_WIDTH, ATTN_WIDTH + S5_WIDTH, ATTN_WIDTH + S5_WIDTH + D_MODEL
    slabs = [jnp.concatenate([yp_ref[:, blk * BLOCK_W + t * LANES: blk * BLOCK_W + (t + 1) * LANES]
                              for blk in range(N_BLOCKS)], axis=1) for t in range(S5_CHUNK)]
    y = jnp.dot(permt_ref[...], jnp.concatenate(slabs, axis=0), preferred_element_type=F32)
    z = y * (0.5 * (1.0 + jnp.tanh(0.7978845608028654 * (y + 0.044715 * (y * y * y)))))
    zz = z * jax.nn.sigmoid(jnp.dot(z.astype(BF16), wglu_ref[...], preferred_element_type=F32) + bglu_ref[...])
    ya = ya_ref[...].reshape(rows, ATTN_WIDTH).astype(F32)
    ta = (ya * _silu(gates[:, 0:o1])).astype(BF16)
    tb = (zz * _silu(gates[:, o1:o2])).astype(BF16)
    pa = jnp.dot(ta, wa_ref[...], preferred_element_type=F32)
    pb = jnp.dot(tb, wb_ref[...], preferred_element_type=F32)
    mix = jax.nn.sigmoid(gates[:, o2:o3]) * pa + jax.nn.sigmoid(gates[:, o3:]) * pb
    o = jnp.dot(mix.astype(BF16), wo_ref[...], preferred_element_type=F32).reshape(bsz, tt, D_MODEL)
    h = x3 + mod_ref[:, :, 2 * D_MODEL:] * o
    ms = jnp.mean(h * h, axis=-1, keepdims=True)
    o_ref[...] = h * lax.rsqrt(ms + EPS) * fg_ref[...]


def _final_call(x, mod3, ng, wg, ya, yp, permt, wglu, bglu, wa, wb, wo, fg):
    bsz, n, _ = x.shape
    tt = TOK_TILE
    up_rows = tt // S5_CHUNK * bsz
    tok = lambda width: pl.BlockSpec((bsz, tt, width), lambda i: (0, i, 0))
    c = lambda shape: _const_spec(shape, 1)
    return pl.pallas_call(
        _final_kernel,
        out_shape=jax.ShapeDtypeStruct((bsz, n, D_MODEL), F32),
        grid=(n // tt,),
        in_specs=[tok(D_MODEL),
                  pl.BlockSpec((bsz, 1, 3 * D_MODEL), lambda i: (0, 0, 0)),
                  c((1, D_MODEL)), c(wg.shape),
                  tok(ATTN_WIDTH),
                  pl.BlockSpec((up_rows, N_BLOCKS * BLOCK_W), lambda i: (i, 0)),
                  c(permt.shape), c(wglu.shape), c((1, S5_WIDTH)), c(wa.shape), c(wb.shape), c(wo.shape),
                  c((1, D_MODEL))],
        out_specs=tok(D_MODEL),
        compiler_params=pltpu.CompilerParams(dimension_semantics=("arbitrary",), vmem_limit_bytes=VMEM_LIMIT),
        name="final",
    )(x, mod3, ng, wg, ya, yp, permt, wglu, bglu, wa, wb, wo, fg)


def _rope_tables(n):
    rows = n // GRID_W
    row_ids = np.repeat(np.arange(rows, dtype=np.float64), GRID_W)
    col_ids = np.tile(np.arange(GRID_W, dtype=np.float64), rows)
    freqs = ROPE_THETA ** (-np.arange(ROPE_FREQS, dtype=np.float64) / ROPE_FREQS)
    ang_r, ang_c = row_ids[:, None] * freqs, col_ids[:, None] * freqs
    cos = np.concatenate([np.cos(ang_r)] * 2 + [np.cos(ang_c)] * 2, axis=1)
    sin = np.concatenate([-np.sin(ang_r), np.sin(ang_r), -np.sin(ang_c), np.sin(ang_c)], axis=1)
    reps = LANES // HEAD_DIM
    return (jnp.asarray(np.tile(cos, (1, reps)).astype(np.float32)),
            jnp.asarray(np.tile(sin, (1, reps)).astype(np.float32)))


def _one_hot(match):
    return jnp.asarray(np.ascontiguousarray(match).astype(BF16))


def _block_ones(width):
    idx = np.arange(width) // HEAD_DIM
    return _one_hot(idx[:, None] == idx[None, :])


def _row_perm(bsz, tt, transpose=False):
    chunks = tt // S5_CHUNK
    r = np.arange(bsz * tt)
    t, pc, b = r // (chunks * bsz), (r // bsz) % chunks, r % bsz
    src = b * tt + pc * S5_CHUNK + t
    match = src[:, None] == np.arange(bsz * tt)[None, :]
    return _one_hot(match.T if transpose else match)


def _lane_perm(transpose=False):
    r = np.arange(BLOCK_W)
    t, j, h = r // LANES, (r % LANES) // S5_GROUP, r % S5_GROUP
    dst = j * S5_CW + t * S5_GROUP + h
    match = dst[:, None] == np.arange(BLOCK_W)[None, :]
    return _one_hot(match.T if transpose else match)


def kernel(x, c, ctx, c_ctx, norm_g, w_ada, b_ada, w_in, q_norm_g, k_norm_g, s5_lam_re, s5_lam_im, s5_log_dt,
           s5_b_re, s5_b_im, s5_c_re, s5_c_im, s5_d, w_glu, b_glu, w_branch_attn, w_branch_s5, w_out,
           final_norm_g):
    assert w_in.shape[0] == 1, "single-layer block"
    bsz, n, _ = x.shape
    n_ctx = ctx.shape[1]
    assert n % TOK_TILE == 0 and n_ctx % TOK_TILE == 0 and bsz == SUBLANES

    ada_rows = 2 * SUBLANES
    cc = jnp.concatenate([c, c_ctx[None], jnp.zeros((ada_rows - bsz - 1, D_MODEL), F32)], axis=0)
    mod = _ada_call(cc, w_ada[0], b_ada[0][None])
    mod3 = mod[:bsz].reshape(bsz, 1, 3 * D_MODEL)
    mod_ctx3 = mod[bsz:bsz + 1].reshape(1, 1, 3 * D_MODEL)

    offs = [0]
    for s in IN_SIZES:
        offs.append(offs[-1] + s)
    w_bf = w_in[0].astype(BF16)
    w_pre = jnp.concatenate([w_bf[:, offs[0]:offs[3]], w_bf[:, offs[4]:offs[5]]], axis=1)
    w_gates = jnp.concatenate([w_bf[:, offs[3]:offs[4]], w_bf[:, offs[5]:]], axis=1)

    ng = norm_g[0][None]
    qg = jnp.tile(q_norm_g[0], N_HEADS)[None]
    kg = jnp.tile(k_norm_g[0], N_KV_HEADS)[None]
    onesq, onesk = _block_ones(ATTN_WIDTH), _block_ones(KV_WIDTH)
    cos, sin = _rope_tables(n)
    row_perm = _row_perm(bsz, TOK_TILE)
    lane_perm = _lane_perm()

    q, k_all, v_all, up = _pre_call(x, ctx, mod3, mod_ctx3, ng, w_pre, onesq, onesk, qg, kg, cos, sin, row_perm)

    y_attn = _attn_call(q, k_all, v_all, tq=256)

    ug = _perm_in_call(up, lane_perm, tr=384)
    m, s_in, gmat, a16 = _s5_operators(s5_lam_re[0], s5_lam_im[0], s5_log_dt[0], s5_b_re[0], s5_b_im[0],
                                       s5_c_re[0], s5_c_im[0], s5_d[0])
    yg = _s5_call(ug, m, s_in, gmat, a16, n_ctx // S5_CHUNK, gb=GROUPS_PER_BLOCK)
    yp = _perm_out_call(yg, _lane_perm(transpose=True), tr=256)

    return _final_call(x, mod3, ng, w_gates, y_attn, yp, _row_perm(bsz, TOK_TILE, transpose=True),
                       w_glu[0].astype(BF16), b_glu[0][None],
                       w_branch_attn[0].astype(BF16), w_branch_s5[0].astype(BF16), w_out[0].astype(BF16),
                       final_norm_g[None])
```

```python
import functools

import numpy as np
import jax
import jax.numpy as jnp
from jax import lax
from jax.experimental import pallas as pl
from jax.experimental.pallas import tpu as pltpu

D_MODEL = 1024
GRID_W = 64
N_HEADS = 8
N_KV_HEADS = 2
HEAD_DIM = 64
GQA_REP = N_HEADS // N_KV_HEADS
ATTN_WIDTH = N_HEADS * HEAD_DIM
KV_WIDTH = N_KV_HEADS * HEAD_DIM
ATTN_SCALE = HEAD_DIM ** -0.5
ROPE_THETA = 10000.0
ROPE_FREQS = HEAD_DIM // 4
S5_WIDTH = 512
S5_GROUP = 16
S5_GROUPS = S5_WIDTH // S5_GROUP
S5_STATE = 64
EPS = 1e-6
IN_SIZES = (ATTN_WIDTH, KV_WIDTH, KV_WIDTH, ATTN_WIDTH, S5_WIDTH, S5_WIDTH, D_MODEL, D_MODEL)

LANES = 128
SUBLANES = 8
S5_CHUNK = 16
S5_CW = S5_CHUNK * S5_GROUP
GROUPS_PER_BLOCK = LANES // S5_GROUP
N_BLOCKS = S5_WIDTH // LANES
BLOCK_W = S5_CHUNK * LANES
HALF_W = BLOCK_W // 2
TOK_TILE = 64
REP_W = GQA_REP * HEAD_DIM
VMEM_LIMIT = 56 * 1024 * 1024

F32 = jnp.float32
BF16 = jnp.bfloat16


def _silu(t):
    return t * jax.nn.sigmoid(t)


def _modulated_norm(x3, mod_ref, ng_ref):
    ms = jnp.mean(x3 * x3, axis=-1, keepdims=True)
    y = x3 * lax.rsqrt(ms + EPS) * ng_ref[...]
    return y * (1.0 + mod_ref[:, :, D_MODEL:2 * D_MODEL]) + mod_ref[:, :, 0:D_MODEL]


def _head_rmsnorm(t, ones_ref, g_ref):
    ss = jnp.dot((t * t).astype(BF16), ones_ref[...], preferred_element_type=F32)
    return t * lax.rsqrt(ss * (1.0 / HEAD_DIM) + EPS) * g_ref[...]


def _rope(t, cos, sin_signed):
    rows = t.shape[0]
    lane = lax.broadcasted_iota(jnp.int32, (rows, LANES), 1)
    first = (lane & ROPE_FREQS) == 0
    outs = []
    for j in range(t.shape[1] // LANES):
        blk = t[:, j * LANES:(j + 1) * LANES]
        partner = jnp.where(first, pltpu.roll(blk, LANES - ROPE_FREQS, 1), pltpu.roll(blk, ROPE_FREQS, 1))
        outs.append(blk * cos + partner * sin_signed)
    return outs[0] if len(outs) == 1 else jnp.concatenate(outs, axis=1)


def _const_spec(shape, grid_rank):
    zeros = (0,) * len(shape)
    return pl.BlockSpec(shape, lambda *_: zeros)


def _ada_kernel(c_ref, w_ref, b_ref, o_ref):
    s = _silu(c_ref[...])
    o_ref[...] = jnp.dot(s, w_ref[...], preferred_element_type=F32,
                         precision=lax.Precision.HIGHEST) + b_ref[...]


def _ada_call(cc, w, b):
    rows, n = cc.shape[0], w.shape[1]
    tn = 512
    return pl.pallas_call(
        _ada_kernel,
        out_shape=jax.ShapeDtypeStruct((rows, n), F32),
        grid=(n // tn,),
        in_specs=[pl.BlockSpec((rows, D_MODEL), lambda j: (0, 0)),
                  pl.BlockSpec((D_MODEL, tn), lambda j: (0, j)),
                  pl.BlockSpec((1, tn), lambda j: (0, j))],
        out_specs=pl.BlockSpec((rows, tn), lambda j: (0, j)),
        compiler_params=pltpu.CompilerParams(dimension_semantics=("arbitrary",)),
        name="ada",
    )(cc, w, b)


def _store_chunk_major(u, perm_ref, up_out):
    r = jnp.dot(perm_ref[...], u.astype(BF16), preferred_element_type=F32).astype(BF16)
    rows = up_out.shape[0]
    for t in range(S5_CHUNK):
        for blk in range(N_BLOCKS):
            up_out[:, blk * BLOCK_W + t * LANES: blk * BLOCK_W + (t + 1) * LANES] = (
                r[t * rows:(t + 1) * rows, blk * LANES:(blk + 1) * LANES])


def _pre_kernel(x_ref, c_ref, mod_ref, modc_ref, ng_ref, w_ref, onesq_ref, onesk_ref, qg_ref, kg_ref, cos_ref, sin_ref,
                perm_ref, q_out, k_out, v_out, up_out, *, ctx_steps):
    bsz, tt, _ = x_ref.shape
    o1, o2, o3 = ATTN_WIDTH, ATTN_WIDTH + KV_WIDTH, ATTN_WIDTH + 2 * KV_WIDTH
    step = pl.program_id(0)

    @pl.when(step < ctx_steps)
    def _():
        xn = _modulated_norm(c_ref[...], modc_ref, ng_ref).reshape(bsz * tt, D_MODEL).astype(BF16)
        p = jnp.dot(xn, w_ref[:, o1:], preferred_element_type=F32)
        k = _head_rmsnorm(p[:, 0:KV_WIDTH], onesk_ref, kg_ref)
        k_out[...] = k.astype(BF16).reshape(bsz, tt, KV_WIDTH)
        v_out[...] = p[:, KV_WIDTH:2 * KV_WIDTH].astype(BF16).reshape(bsz, tt, KV_WIDTH)
        _store_chunk_major(p[:, 2 * KV_WIDTH:], perm_ref, up_out)

    @pl.when(step >= ctx_steps)
    def _():
        xn = _modulated_norm(x_ref[...], mod_ref, ng_ref).reshape(bsz * tt, D_MODEL).astype(BF16)
        p = jnp.dot(xn, w_ref[...], preferred_element_type=F32)
        cos = jnp.concatenate([cos_ref[...]] * bsz, axis=0)
        sin = jnp.concatenate([sin_ref[...]] * bsz, axis=0)
        q = _rope(_head_rmsnorm(p[:, 0:o1], onesq_ref, qg_ref), cos, sin)
        k = _rope(_head_rmsnorm(p[:, o1:o2], onesk_ref, kg_ref), cos, sin)
        q_out[...] = (q * ATTN_SCALE).astype(BF16).reshape(bsz, tt, ATTN_WIDTH)
        k_out[...] = k.astype(BF16).reshape(bsz, tt, KV_WIDTH)
        v_out[...] = p[:, o2:o3].astype(BF16).reshape(bsz, tt, KV_WIDTH)
        _store_chunk_major(p[:, o3:], perm_ref, up_out)


def _pre_call(x, ctx, mod3, mod_ctx3, ng, w, onesq, onesk, qg, kg, cos, sin, perm):
    bsz, n, _ = x.shape
    n_ctx = ctx.shape[1]
    tt = TOK_TILE
    up_rows = tt // S5_CHUNK * bsz
    ctx_steps = n_ctx // tt
    lat = lambda i: jnp.maximum(i - ctx_steps, 0)
    c = lambda shape: _const_spec(shape, 1)
    return pl.pallas_call(
        functools.partial(_pre_kernel, ctx_steps=ctx_steps),
        out_shape=(jax.ShapeDtypeStruct((bsz, n, ATTN_WIDTH), BF16),
                   jax.ShapeDtypeStruct((bsz, n_ctx + n, KV_WIDTH), BF16),
                   jax.ShapeDtypeStruct((bsz, n_ctx + n, KV_WIDTH), BF16),
                   jax.ShapeDtypeStruct(((n_ctx + n) // S5_CHUNK * bsz, N_BLOCKS * BLOCK_W), BF16)),
        grid=(ctx_steps + n // tt,),
        in_specs=[pl.BlockSpec((bsz, tt, D_MODEL), lambda i: (0, lat(i), 0)),
                  pl.BlockSpec((bsz, tt, D_MODEL), lambda i: (0, jnp.minimum(i, ctx_steps - 1), 0)),
                  c((bsz, 1, 3 * D_MODEL)), c((1, 1, 3 * D_MODEL)),
                  c((1, D_MODEL)), c(w.shape), c(onesq.shape), c(onesk.shape),
                  c((1, ATTN_WIDTH)), c((1, KV_WIDTH)),
                  pl.BlockSpec((tt, LANES), lambda i: (lat(i), 0)),
                  pl.BlockSpec((tt, LANES), lambda i: (lat(i), 0)),
                  c(perm.shape)],
        out_specs=(pl.BlockSpec((bsz, tt, ATTN_WIDTH), lambda i: (0, lat(i), 0)),
                   pl.BlockSpec((bsz, tt, KV_WIDTH), lambda i: (0, i, 0)),
                   pl.BlockSpec((bsz, tt, KV_WIDTH), lambda i: (0, i, 0)),
                   pl.BlockSpec((up_rows, N_BLOCKS * BLOCK_W), lambda i: (i, 0))),
        compiler_params=pltpu.CompilerParams(dimension_semantics=("arbitrary",), vmem_limit_bytes=VMEM_LIMIT),
        name="pre",
    )(x, ctx, mod3, mod_ctx3, ng, w, onesq, onesk, qg, kg, cos, sin, perm)


def _attn_kernel(q_ref, k_ref, v_ref, e_ref, o_ref, k4t_ref, v4_ref):
    @pl.when(pl.program_id(1) == 0)
    def _():
        k4 = jnp.dot(k_ref[...], e_ref[...], preferred_element_type=F32)
        k4t = k4.T
        v4 = jnp.dot(v_ref[...], e_ref[...], preferred_element_type=F32)
        for g in range(N_KV_HEADS):
            k4t_ref[g] = k4t[g * REP_W:(g + 1) * REP_W].astype(BF16)
            v4_ref[g] = v4[:, g * REP_W:(g + 1) * REP_W].astype(BF16)

    tq = q_ref.shape[0]
    lane = lax.broadcasted_iota(jnp.int32, (tq, REP_W), 1)
    for g in range(N_KV_HEADS):
        qg = q_ref[:, g * REP_W:(g + 1) * REP_W]
        acc = jnp.zeros((tq, REP_W), F32)
        for r in range(GQA_REP):
            in_head = (lane >= r * HEAD_DIM) & (lane < (r + 1) * HEAD_DIM)
            qr = jnp.where(in_head, qg, jnp.zeros_like(qg))
            s = jnp.dot(qr, k4t_ref[g], preferred_element_type=F32)
            m = jnp.max(s, axis=1, keepdims=True)
            p = jnp.exp(s - m)
            l = jnp.sum(p, axis=1, keepdims=True)
            o = jnp.dot(p.astype(BF16), v4_ref[g], preferred_element_type=F32)
            acc = jnp.where(in_head, o / l, acc)
        o_ref[:, g * REP_W:(g + 1) * REP_W] = acc.astype(o_ref.dtype)


def _attn_call(q, k_all, v_all, expand, tq):
    bsz, n, _ = q.shape
    nk = k_all.shape[1]
    return pl.pallas_call(
        _attn_kernel,
        out_shape=jax.ShapeDtypeStruct((bsz, n, ATTN_WIDTH), BF16),
        grid=(bsz, n // tq),
        in_specs=[pl.BlockSpec((None, tq, ATTN_WIDTH), lambda b, i: (b, i, 0)),
                  pl.BlockSpec((None, nk, KV_WIDTH), lambda b, i: (b, 0, 0)),
                  pl.BlockSpec((None, nk, KV_WIDTH), lambda b, i: (b, 0, 0)),
                  _const_spec(expand.shape, 2)],
        out_specs=pl.BlockSpec((None, tq, ATTN_WIDTH), lambda b, i: (b, i, 0)),
        scratch_shapes=[pltpu.VMEM((N_KV_HEADS, REP_W, nk), BF16),
                        pltpu.VMEM((N_KV_HEADS, nk, REP_W), BF16)],
        compiler_params=pltpu.CompilerParams(dimension_semantics=("arbitrary", "arbitrary"),
                                             vmem_limit_bytes=VMEM_LIMIT),
        name="attn",
    )(q, k_all, v_all, expand)


def _perm_in_kernel(s_ref, p_ref, o_ref):
    halves = [jnp.dot(s_ref[:, hf * HALF_W:(hf + 1) * HALF_W], p_ref[...], preferred_element_type=F32)
              for hf in range(BLOCK_W // HALF_W)]
    for j in range(GROUPS_PER_BLOCK):
        o_ref[j] = jnp.concatenate([r[:, j * LANES:(j + 1) * LANES] for r in halves], axis=1).astype(o_ref.dtype)


def _perm_in_call(up, pmat, tr):
    rows = up.shape[0]
    return pl.pallas_call(
        _perm_in_kernel,
        out_shape=jax.ShapeDtypeStruct((S5_GROUPS, rows, S5_CW), BF16),
        grid=(N_BLOCKS, rows // tr),
        in_specs=[pl.BlockSpec((tr, BLOCK_W), lambda s, i: (i, s)),
                  _const_spec(pmat.shape, 2)],
        out_specs=pl.BlockSpec((GROUPS_PER_BLOCK, tr, S5_CW), lambda s, i: (s, i, 0)),
        compiler_params=pltpu.CompilerParams(dimension_semantics=("arbitrary", "arbitrary"),
                                             vmem_limit_bytes=VMEM_LIMIT),
        name="perm_in",
    )(up, pmat)


def _perm_out_kernel(y_ref, q_ref, o_ref):
    for hf in range(BLOCK_W // HALF_W):
        ycat = jnp.concatenate([y_ref[j, :, hf * LANES:(hf + 1) * LANES] for j in range(GROUPS_PER_BLOCK)], axis=1)
        o_ref[:, hf * HALF_W:(hf + 1) * HALF_W] = jnp.dot(ycat, q_ref[...],
                                                         preferred_element_type=F32).astype(o_ref.dtype)


def _perm_out_call(yg, qmat, tr):
    rows = yg.shape[1]
    return pl.pallas_call(
        _perm_out_kernel,
        out_shape=jax.ShapeDtypeStruct((rows, N_BLOCKS * BLOCK_W), BF16),
        grid=(N_BLOCKS, rows // tr),
        in_specs=[pl.BlockSpec((GROUPS_PER_BLOCK, tr, S5_CW), lambda s, i: (s, i, 0)),
                  _const_spec(qmat.shape, 2)],
        out_specs=pl.BlockSpec((tr, BLOCK_W), lambda s, i: (i, s)),
        compiler_params=pltpu.CompilerParams(dimension_semantics=("arbitrary", "arbitrary"),
                                             vmem_limit_bytes=VMEM_LIMIT),
        name="perm_out",
    )(yg, qmat)


def _s5_kernel(u_ref, m_ref, sin_ref, g_ref, a_ref, y_ref, buf_ref, *, n_ctx_chunks, n_chunks):
    half = S5_STATE
    gb = u_ref.shape[0]
    for j in range(gb):
        buf_ref[j] = jnp.dot(u_ref[j], sin_ref[j], preferred_element_type=F32)
    a_re = [jnp.broadcast_to(a_ref[j, 0:1, :], (SUBLANES, LANES)) for j in range(gb)]
    a_im = [jnp.broadcast_to(a_ref[j, 1:2, :], (SUBLANES, LANES)) for j in range(gb)]
    is_fwd = lax.broadcasted_iota(jnp.int32, (SUBLANES, LANES), 1) < half

    def step(k, carry):
        pos_b = jnp.where(k < n_ctx_chunks, n_ctx_chunks - 1 - k, n_chunks + n_ctx_chunks - 1 - k)
        rf = pl.multiple_of(k * SUBLANES, SUBLANES)
        rb = pl.multiple_of(pos_b * SUBLANES, SUBLANES)
        out = []
        for j in range(gb):
            h_re, h_im = carry[2 * j], carry[2 * j + 1]
            xf = buf_ref[j, pl.ds(rf, SUBLANES), :]
            xb = buf_ref[j, pl.ds(rb, SUBLANES), :]
            buf_ref[j, pl.ds(rf, SUBLANES), 0:half] = h_re[:, 0:half]
            buf_ref[j, pl.ds(rf, SUBLANES), 2 * half:3 * half] = h_im[:, 0:half]
            buf_ref[j, pl.ds(rb, SUBLANES), half:2 * half] = h_re[:, half:]
            buf_ref[j, pl.ds(rb, SUBLANES), 3 * half:] = h_im[:, half:]
            x_re = jnp.where(is_fwd, xf[:, 0:LANES], xb[:, 0:LANES])
            x_im = jnp.where(is_fwd, xf[:, LANES:], xb[:, LANES:])
            out.append(a_re[j] * h_re - a_im[j] * h_im + x_re)
            out.append(a_re[j] * h_im + a_im[j] * h_re + x_im)
        return tuple(out)

    zero = jnp.zeros((SUBLANES, LANES), F32)
    lax.fori_loop(0, n_chunks, step, (zero,) * (2 * gb))

    r0 = n_ctx_chunks * SUBLANES
    for j in range(gb):
        y = jnp.dot(u_ref[j, r0:, :], m_ref[j], preferred_element_type=F32)
        y = y + jnp.dot(buf_ref[j, r0:, :].astype(BF16), g_ref[j], preferred_element_type=F32)
        y_ref[j] = y.astype(y_ref.dtype)


def _s5_call(ug, m, sin, gmat, a16, n_ctx_chunks, gb):
    groups, rows, _ = ug.shape
    n_chunks = rows // SUBLANES
    out_rows = rows - n_ctx_chunks * SUBLANES
    mat = lambda: pl.BlockSpec((gb, S5_CW, S5_CW), lambda g: (g, 0, 0))
    return pl.pallas_call(
        functools.partial(_s5_kernel, n_ctx_chunks=n_ctx_chunks, n_chunks=n_chunks),
        out_shape=jax.ShapeDtypeStruct((groups, out_rows, S5_CW), BF16),
        grid=(groups // gb,),
        in_specs=[pl.BlockSpec((gb, rows, S5_CW), lambda g: (g, 0, 0)),
                  mat(), mat(), mat(),
                  pl.BlockSpec((gb, 2, LANES), lambda g: (g, 0, 0))],
        out_specs=pl.BlockSpec((gb, out_rows, S5_CW), lambda g: (g, 0, 0)),
        scratch_shapes=[pltpu.VMEM((gb, rows, S5_CW), F32)],
        compiler_params=pltpu.CompilerParams(dimension_semantics=("arbitrary",),
                                             vmem_limit_bytes=VMEM_LIMIT),
        name="s5",
    )(ug, m, sin, gmat, a16)


def _s5_ops_kernel(*refs):
    for j in range(refs[0].shape[0]):
        _s5_ops_group(*(r.at[j] for r in refs))


def _s5_ops_group(lam_ref, ldt_ref, bt_ref, ct_ref, d_ref, m_ref, sin_ref, g_ref, a_ref):
    T, H = S5_CHUNK, S5_GROUP
    lr = jnp.minimum(lam_ref[0:1, :], -1e-4)
    li = lam_ref[1:2, :]
    dt = jnp.exp(ldt_ref[...])
    taus = lax.broadcasted_iota(jnp.int32, (3 * SUBLANES, LANES), 0).astype(F32)
    mag = jnp.exp(lr * dt * taus)
    pw_r = mag * jnp.cos(li * dt * taus)
    pw_i = mag * jnp.sin(li * dt * taus)
    nr, ni = pw_r[1:2] - 1.0, pw_i[1:2]
    den = lr * lr + li * li
    cf_r = (nr * lr + ni * li) / den
    cf_i = (ni * lr - nr * li) / den
    bb_r = cf_r * bt_ref[0] - cf_i * bt_ref[1]
    bb_i = cf_r * bt_ref[1] + cf_i * bt_ref[0]
    is_fwd = lax.broadcasted_iota(jnp.int32, (H, LANES), 1) < S5_STATE

    def powers(tau_f, tau_b):
        pick = lambda pw, s: jnp.where(is_fwd, jnp.broadcast_to(pw[tau_f(s):tau_f(s) + 1], (H, LANES)),
                                       jnp.broadcast_to(pw[tau_b(s):tau_b(s) + 1], (H, LANES)))
        return (jnp.concatenate([pick(pw_r, s) for s in range(T)], axis=0),
                jnp.concatenate([pick(pw_i, s) for s in range(T)], axis=0))

    tile = lambda a: jnp.concatenate([a] * T, axis=0)
    bbr, bbi, cr, ci = tile(bb_r), tile(bb_i), tile(ct_ref[0]), tile(ct_ref[1])

    er, ei = powers(lambda s: T - 1 - s, lambda s: s)
    sin_ref[:, 0:LANES] = (er * bbr - ei * bbi).astype(sin_ref.dtype)
    sin_ref[:, LANES:] = (er * bbi + ei * bbr).astype(sin_ref.dtype)

    er, ei = powers(lambda t: t + 1, lambda t: T - t)
    gt = jnp.concatenate([er * cr - ei * ci, -(er * ci + ei * cr)], axis=1)
    g_ref[...] = gt.T.astype(g_ref.dtype)

    er, ei = powers(lambda a: a, lambda a: T - 1 - a)
    cp = jnp.concatenate([er * cr - ei * ci, er * ci + ei * cr], axis=1)
    zero = jnp.zeros_like(bb_r)
    lhs = jnp.concatenate([jnp.concatenate([jnp.where(is_fwd, bb_r, zero), jnp.where(is_fwd, -bb_i, zero)], axis=1),
                           jnp.concatenate([jnp.where(is_fwd, zero, bb_r), jnp.where(is_fwd, zero, -bb_i)], axis=1)],
                          axis=0)
    kr = lax.dot_general(lhs, cp, (((1,), (1,)), ((), ())), preferred_element_type=F32,
                         precision=lax.Precision.HIGHEST)
    pad = jnp.zeros((H, S5_CW), F32)
    wide_f = jnp.concatenate([pad, kr[0:H]], axis=1)
    wide_b = jnp.concatenate([kr[H:], pad], axis=1)
    lane = lax.broadcasted_iota(jnp.int32, (H, S5_CW), 1)
    row = lax.broadcasted_iota(jnp.int32, (H, S5_CW), 0)
    skip = jnp.broadcast_to(d_ref[...], (H, S5_CW))
    for s in range(T):
        blk_f = pltpu.roll(wide_f, H * s, 1)[:, S5_CW:] if s else wide_f[:, S5_CW:]
        shift_b = (2 * S5_CW - H * (T - 1 - s)) % (2 * S5_CW)
        blk_b = (pltpu.roll(wide_b, shift_b, 1) if shift_b else wide_b)[:, :S5_CW]
        diag = jnp.where(lane == H * s + row, skip, 0.0)
        m_ref[H * s:H * (s + 1), :] = (blk_f + blk_b + diag).astype(m_ref.dtype)
    a_ref[0:1, :] = pw_r[T:T + 1]
    a_ref[1:2, :] = pw_i[T:T + 1]


def _s5_ops_call(lam, ldt, bt, ct, dt_tiled):
    groups = lam.shape[0]
    gb = 4
    mat = lambda: pl.BlockSpec((gb, S5_CW, S5_CW), lambda g: (g, 0, 0))
    vec = lambda a: pl.BlockSpec((gb,) + a.shape[1:], lambda g: (g,) + (0,) * (a.ndim - 1))
    mshape = jax.ShapeDtypeStruct((groups, S5_CW, S5_CW), BF16)
    return pl.pallas_call(
        _s5_ops_kernel,
        out_shape=(mshape, mshape, mshape, jax.ShapeDtypeStruct((groups, 2, LANES), F32)),
        grid=(groups // gb,),
        in_specs=[vec(lam), vec(ldt), vec(bt), vec(ct), vec(dt_tiled)],
        out_specs=(mat(), mat(), mat(), pl.BlockSpec((gb, 2, LANES), lambda g: (g, 0, 0))),
        compiler_params=pltpu.CompilerParams(dimension_semantics=("arbitrary",)),
        name="s5_ops",
    )(lam, ldt, bt, ct, dt_tiled)


def _s5_operators(lam_re, lam_im, log_dt, b_re, b_im, c_re, c_im, d_skip):
    G, P, H = S5_GROUPS, S5_STATE, S5_GROUP
    lam = jnp.stack([lam_re, lam_im]).astype(F32).transpose(2, 0, 1, 3).reshape(G, 2, 2 * P)
    ldt = jnp.repeat(log_dt.astype(F32).T, P, axis=1).reshape(G, 1, 2 * P)
    bt = jnp.stack([b_re, b_im]).astype(F32).transpose(2, 0, 4, 1, 3).reshape(G, 2, H, 2 * P)
    ct = jnp.stack([c_re, c_im]).astype(F32).transpose(2, 0, 3, 1, 4).reshape(G, 2, H, 2 * P)
    dt_tiled = jnp.tile(d_skip.astype(F32).reshape(G, 1, H), (1, 1, S5_CHUNK))
    return _s5_ops_call(lam, ldt, bt, ct, dt_tiled)


def _final_kernel(x_ref, mod_ref, ng_ref, wg_ref, ya_ref, yp_ref, permt_ref, wglu_ref, bglu_ref, wa_ref, wb_ref,
                  wo_ref, fg_ref, o_ref):
    bsz, tt, _ = x_ref.shape
    rows = bsz * tt
    x3 = x_ref[...]
    xn = _modulated_norm(x3, mod_ref, ng_ref).reshape(rows, D_MODEL).astype(BF16)
    gates = jnp.dot(xn, wg_ref[...], preferred_element_type=F32)
    o1, o2, o3 = ATTN_WIDTH, ATTN_WIDTH + S5_WIDTH, ATTN_WIDTH + S5_WIDTH + D_MODEL
    slabs = [jnp.concatenate([yp_ref[:, blk * BLOCK_W + t * LANES: blk * BLOCK_W + (t + 1) * LANES]
                              for blk in range(N_BLOCKS)], axis=1) for t in range(S5_CHUNK)]
    y = jnp.dot(permt_ref[...], jnp.concatenate(slabs, axis=0), preferred_element_type=F32)
    z = y * (0.5 * (1.0 + jnp.tanh(0.7978845608028654 * (y + 0.044715 * (y * y * y)))))
    zz = z * jax.nn.sigmoid(jnp.dot(z.astype(BF16), wglu_ref[...], preferred_element_type=F32) + bglu_ref[...])
    ya = ya_ref[...].reshape(rows, ATTN_WIDTH).astype(F32)
    ta = (ya * _silu(gates[:, 0:o1])).astype(BF16)
    tb = (zz * _silu(gates[:, o1:o2])).astype(BF16)
    pa = jnp.dot(ta, wa_ref[...], preferred_element_type=F32)
    pb = jnp.dot(tb, wb_ref[...], preferred_element_type=F32)
    mix = jax.nn.sigmoid(gates[:, o2:o3]) * pa + jax.nn.sigmoid(gates[:, o3:]) * pb
    o = jnp.dot(mix.astype(BF16), wo_ref[...], preferred_element_type=F32).reshape(bsz, tt, D_MODEL)
    h = x3 + mod_ref[:, :, 2 * D_MODEL:] * o
    ms = jnp.mean(h * h, axis=-1, keepdims=True)
    o_ref[...] = h * lax.rsqrt(ms + EPS) * fg_ref[...]


def _final_call(x, mod3, ng, wg, ya, yp, permt, wglu, bglu, wa, wb, wo, fg):
    bsz, n, _ = x.shape
    tt = TOK_TILE
    up_rows = tt // S5_CHUNK * bsz
    tok = lambda width: pl.BlockSpec((bsz, tt, width), lambda i: (0, i, 0))
    c = lambda shape: _const_spec(shape, 1)
    return pl.pallas_call(
        _final_kernel,
        out_shape=jax.ShapeDtypeStruct((bsz, n, D_MODEL), F32),
        grid=(n // tt,),
        in_specs=[tok(D_MODEL),
                  pl.BlockSpec((bsz, 1, 3 * D_MODEL), lambda i: (0, 0, 0)),
                  c((1, D_MODEL)), c(wg.shape),
                  tok(ATTN_WIDTH),
                  pl.BlockSpec((up_rows, N_BLOCKS * BLOCK_W), lambda i: (i, 0)),
                  c(permt.shape), c(wglu.shape), c((1, S5_WIDTH)), c(wa.shape), c(wb.shape), c(wo.shape),
                  c((1, D_MODEL))],
        out_specs=tok(D_MODEL),
        compiler_params=pltpu.CompilerParams(dimension_semantics=("arbitrary",), vmem_limit_bytes=VMEM_LIMIT),
        name="final",
    )(x, mod3, ng, wg, ya, yp, permt, wglu, bglu, wa, wb, wo, fg)


def _rope_tables(n):
    rows = n // GRID_W
    row_ids = np.repeat(np.arange(rows, dtype=np.float64), GRID_W)
    col_ids = np.tile(np.arange(GRID_W, dtype=np.float64), rows)
    freqs = ROPE_THETA ** (-np.arange(ROPE_FREQS, dtype=np.float64) / ROPE_FREQS)
    ang_r, ang_c = row_ids[:, None] * freqs, col_ids[:, None] * freqs
    cos = np.concatenate([np.cos(ang_r)] * 2 + [np.cos(ang_c)] * 2, axis=1)
    sin = np.concatenate([-np.sin(ang_r), np.sin(ang_r), -np.sin(ang_c), np.sin(ang_c)], axis=1)
    reps = LANES // HEAD_DIM
    return (jnp.asarray(np.tile(cos, (1, reps)).astype(np.float32)),
            jnp.asarray(np.tile(sin, (1, reps)).astype(np.float32)))


def _one_hot(match):
    return jnp.asarray(np.ascontiguousarray(match).astype(BF16))


def _block_ones(width):
    idx = np.arange(width) // HEAD_DIM
    return _one_hot(idx[:, None] == idx[None, :])


def _row_perm(bsz, tt, transpose=False):
    chunks = tt // S5_CHUNK
    r = np.arange(bsz * tt)
    t, pc, b = r // (chunks * bsz), (r // bsz) % chunks, r % bsz
    src = b * tt + pc * S5_CHUNK + t
    match = src[:, None] == np.arange(bsz * tt)[None, :]
    return _one_hot(match.T if transpose else match)


def _lane_perm(transpose=False):
    r = np.arange(HALF_W)
    t, j, h = r // LANES, (r % LANES) // S5_GROUP, r % S5_GROUP
    dst = j * LANES + t * S5_GROUP + h
    match = dst[:, None] == np.arange(HALF_W)[None, :]
    return _one_hot(match.T if transpose else match)


def _kv_expand():
    col = np.arange(N_KV_HEADS * REP_W)
    src = (col // REP_W) * HEAD_DIM + col % HEAD_DIM
    return _one_hot(np.arange(KV_WIDTH)[:, None] == src[None, :])


def kernel(x, c, ctx, c_ctx, norm_g, w_ada, b_ada, w_in, q_norm_g, k_norm_g, s5_lam_re, s5_lam_im, s5_log_dt,
           s5_b_re, s5_b_im, s5_c_re, s5_c_im, s5_d, w_glu, b_glu, w_branch_attn, w_branch_s5, w_out,
           final_norm_g):
    assert w_in.shape[0] == 1, "single-layer block"
    bsz, n, _ = x.shape
    n_ctx = ctx.shape[1]
    assert n % TOK_TILE == 0 and n_ctx % TOK_TILE == 0 and bsz == SUBLANES

    ada_rows = 2 * SUBLANES
    cc = jnp.concatenate([c, c_ctx[None], jnp.zeros((ada_rows - bsz - 1, D_MODEL), F32)], axis=0)
    mod = _ada_call(cc, w_ada[0], b_ada[0][None])
    mod3 = mod[:bsz].reshape(bsz, 1, 3 * D_MODEL)
    mod_ctx3 = mod[bsz:bsz + 1].reshape(1, 1, 3 * D_MODEL)

    offs = [0]
    for s in IN_SIZES:
        offs.append(offs[-1] + s)
    w_bf = w_in[0].astype(BF16)
    w_pre = jnp.concatenate([w_bf[:, offs[0]:offs[3]], w_bf[:, offs[4]:offs[5]]], axis=1)
    w_gates = jnp.concatenate([w_bf[:, offs[3]:offs[4]], w_bf[:, offs[5]:]], axis=1)

    ng = norm_g[0][None]
    qg = jnp.tile(q_norm_g[0], N_HEADS)[None]
    kg = jnp.tile(k_norm_g[0], N_KV_HEADS)[None]
    onesq, onesk = _block_ones(ATTN_WIDTH), _block_ones(KV_WIDTH)
    cos, sin = _rope_tables(n)
    row_perm = _row_perm(bsz, TOK_TILE)
    lane_perm = _lane_perm()

    q, k_all, v_all, up = _pre_call(x, ctx, mod3, mod_ctx3, ng, w_pre, onesq, onesk, qg, kg, cos, sin, row_perm)

    y_attn = _attn_call(q, k_all, v_all, _kv_expand(), tq=512)

    ug = _perm_in_call(up, lane_perm, tr=384)
    m, s_in, gmat, a16 = _s5_operators(s5_lam_re[0], s5_lam_im[0], s5_log_dt[0], s5_b_re[0], s5_b_im[0],
                                       s5_c_re[0], s5_c_im[0], s5_d[0])
    yg = _s5_call(ug, m, s_in, gmat, a16, n_ctx // S5_CHUNK, gb=GROUPS_PER_BLOCK)
    yp = _perm_out_call(yg, _lane_perm(transpose=True), tr=256)

    return _final_call(x, mod3, ng, w_gates, y_attn, yp, _row_perm(bsz, TOK_TILE, transpose=True),
                       w_glu[0].astype(BF16), b_glu[0][None],
                       w_branch_attn[0].astype(BF16), w_branch_s5[0].astype(BF16), w_out[0].astype(BF16),
                       final_norm_g[None])
```

```python
import functools

import numpy as np
import jax
import jax.numpy as jnp
from jax import lax
from jax.experimental import pallas as pl
from jax.experimental.pallas import tpu as pltpu

D_MODEL = 1024
GRID_W = 64
N_HEADS = 8
N_KV_HEADS = 2
HEAD_DIM = 64
GQA_REP = N_HEADS // N_KV_HEADS
ATTN_WIDTH = N_HEADS * HEAD_DIM
KV_WIDTH = N_KV_HEADS * HEAD_DIM
ATTN_SCALE = HEAD_DIM ** -0.5
ROPE_THETA = 10000.0
ROPE_FREQS = HEAD_DIM // 4
S5_WIDTH = 512
S5_GROUP = 16
S5_GROUPS = S5_WIDTH // S5_GROUP
S5_STATE = 64
EPS = 1e-6
IN_SIZES = (ATTN_WIDTH, KV_WIDTH, KV_WIDTH, ATTN_WIDTH, S5_WIDTH, S5_WIDTH, D_MODEL, D_MODEL)

LANES = 128
SUBLANES = 8
S5_CHUNK = 16
S5_CW = S5_CHUNK * S5_GROUP
GROUPS_PER_BLOCK = LANES // S5_GROUP
N_BLOCKS = S5_WIDTH // LANES
BLOCK_W = S5_CHUNK * LANES
HALF_W = BLOCK_W // 2
TOK_TILE = 64
REP_W = GQA_REP * HEAD_DIM
VMEM_LIMIT = 56 * 1024 * 1024

F32 = jnp.float32
BF16 = jnp.bfloat16


def _silu(t):
    return t * jax.nn.sigmoid(t)


def _modulated_norm(x3, mod_ref, ng_ref):
    ms = jnp.mean(x3 * x3, axis=-1, keepdims=True)
    y = x3 * lax.rsqrt(ms + EPS) * ng_ref[...]
    return y * (1.0 + mod_ref[:, :, D_MODEL:2 * D_MODEL]) + mod_ref[:, :, 0:D_MODEL]


def _head_rmsnorm(t, ones_ref, g_ref):
    ss = jnp.dot((t * t).astype(BF16), ones_ref[...], preferred_element_type=F32)
    return t * lax.rsqrt(ss * (1.0 / HEAD_DIM) + EPS) * g_ref[...]


def _rope(t, cos, sin_signed):
    rows = t.shape[0]
    lane = lax.broadcasted_iota(jnp.int32, (rows, LANES), 1)
    first = (lane & ROPE_FREQS) == 0
    outs = []
    for j in range(t.shape[1] // LANES):
        blk = t[:, j * LANES:(j + 1) * LANES]
        partner = jnp.where(first, pltpu.roll(blk, LANES - ROPE_FREQS, 1), pltpu.roll(blk, ROPE_FREQS, 1))
        outs.append(blk * cos + partner * sin_signed)
    return outs[0] if len(outs) == 1 else jnp.concatenate(outs, axis=1)


def _const_spec(shape, grid_rank):
    zeros = (0,) * len(shape)
    return pl.BlockSpec(shape, lambda *_: zeros)


def _ada_kernel(c_ref, w_ref, b_ref, o_ref):
    s = _silu(c_ref[...])
    o_ref[...] = jnp.dot(s, w_ref[...], preferred_element_type=F32,
                         precision=lax.Precision.HIGHEST) + b_ref[...]


def _ada_call(cc, w, b):
    rows, n = cc.shape[0], w.shape[1]
    tn = 512
    return pl.pallas_call(
        _ada_kernel,
        out_shape=jax.ShapeDtypeStruct((rows, n), F32),
        grid=(n // tn,),
        in_specs=[pl.BlockSpec((rows, D_MODEL), lambda j: (0, 0)),
                  pl.BlockSpec((D_MODEL, tn), lambda j: (0, j)),
                  pl.BlockSpec((1, tn), lambda j: (0, j))],
        out_specs=pl.BlockSpec((rows, tn), lambda j: (0, j)),
        compiler_params=pltpu.CompilerParams(dimension_semantics=("arbitrary",)),
        name="ada",
    )(cc, w, b)


def _store_chunk_major(u, perm_ref, up_out):
    r = jnp.dot(perm_ref[...], u.astype(BF16), preferred_element_type=F32).astype(BF16)
    rows = up_out.shape[0]
    for t in range(S5_CHUNK):
        for blk in range(N_BLOCKS):
            up_out[:, blk * BLOCK_W + t * LANES: blk * BLOCK_W + (t + 1) * LANES] = (
                r[t * rows:(t + 1) * rows, blk * LANES:(blk + 1) * LANES])


def _pre_kernel(x_ref, c_ref, mod_ref, modc_ref, ng_ref, w_ref, onesq_ref, onesk_ref, qg_ref, kg_ref, cos_ref, sin_ref,
                perm_ref, q_out, k_out, v_out, up_out, *, ctx_steps):
    bsz, tt, _ = x_ref.shape
    o1, o2, o3 = ATTN_WIDTH, ATTN_WIDTH + KV_WIDTH, ATTN_WIDTH + 2 * KV_WIDTH
    step = pl.program_id(0)

    @pl.when(step < ctx_steps)
    def _():
        xn = _modulated_norm(c_ref[...], modc_ref, ng_ref).reshape(bsz * tt, D_MODEL).astype(BF16)
        p = jnp.dot(xn, w_ref[:, o1:], preferred_element_type=F32)
        k = _head_rmsnorm(p[:, 0:KV_WIDTH], onesk_ref, kg_ref)
        k_out[...] = k.astype(BF16).reshape(bsz, tt, KV_WIDTH)
        v_out[...] = p[:, KV_WIDTH:2 * KV_WIDTH].astype(BF16).reshape(bsz, tt, KV_WIDTH)
        _store_chunk_major(p[:, 2 * KV_WIDTH:], perm_ref, up_out)

    @pl.when(step >= ctx_steps)
    def _():
        xn = _modulated_norm(x_ref[...], mod_ref, ng_ref).reshape(bsz * tt, D_MODEL).astype(BF16)
        p = jnp.dot(xn, w_ref[...], preferred_element_type=F32)
        cos = jnp.concatenate([cos_ref[...]] * bsz, axis=0)
        sin = jnp.concatenate([sin_ref[...]] * bsz, axis=0)
        q = _rope(_head_rmsnorm(p[:, 0:o1], onesq_ref, qg_ref), cos, sin)
        k = _rope(_head_rmsnorm(p[:, o1:o2], onesk_ref, kg_ref), cos, sin)
        q_out[...] = (q * ATTN_SCALE).astype(BF16).reshape(bsz, tt, ATTN_WIDTH)
        k_out[...] = k.astype(BF16).reshape(bsz, tt, KV_WIDTH)
        v_out[...] = p[:, o2:o3].astype(BF16).reshape(bsz, tt, KV_WIDTH)
        _store_chunk_major(p[:, o3:], perm_ref, up_out)


def _pre_call(x, ctx, mod3, mod_ctx3, ng, w, onesq, onesk, qg, kg, cos, sin, perm):
    bsz, n, _ = x.shape
    n_ctx = ctx.shape[1]
    tt = TOK_TILE
    up_rows = tt // S5_CHUNK * bsz
    ctx_steps = n_ctx // tt
    lat = lambda i: jnp.maximum(i - ctx_steps, 0)
    c = lambda shape: _const_spec(shape, 1)
    return pl.pallas_call(
        functools.partial(_pre_kernel, ctx_steps=ctx_steps),
        out_shape=(jax.ShapeDtypeStruct((bsz, n, ATTN_WIDTH), BF16),
                   jax.ShapeDtypeStruct((bsz, n_ctx + n, KV_WIDTH), BF16),
                   jax.ShapeDtypeStruct((bsz, n_ctx + n, KV_WIDTH), BF16),
                   jax.ShapeDtypeStruct(((n_ctx + n) // S5_CHUNK * bsz, N_BLOCKS * BLOCK_W), BF16)),
        grid=(ctx_steps + n // tt,),
        in_specs=[pl.BlockSpec((bsz, tt, D_MODEL), lambda i: (0, lat(i), 0)),
                  pl.BlockSpec((bsz, tt, D_MODEL), lambda i: (0, jnp.minimum(i, ctx_steps - 1), 0)),
                  c((bsz, 1, 3 * D_MODEL)), c((1, 1, 3 * D_MODEL)),
                  c((1, D_MODEL)), c(w.shape), c(onesq.shape), c(onesk.shape),
                  c((1, ATTN_WIDTH)), c((1, KV_WIDTH)),
                  pl.BlockSpec((tt, LANES), lambda i: (lat(i), 0)),
                  pl.BlockSpec((tt, LANES), lambda i: (lat(i), 0)),
                  c(perm.shape)],
        out_specs=(pl.BlockSpec((bsz, tt, ATTN_WIDTH), lambda i: (0, lat(i), 0)),
                   pl.BlockSpec((bsz, tt, KV_WIDTH), lambda i: (0, i, 0)),
                   pl.BlockSpec((bsz, tt, KV_WIDTH), lambda i: (0, i, 0)),
                   pl.BlockSpec((up_rows, N_BLOCKS * BLOCK_W), lambda i: (i, 0))),
        compiler_params=pltpu.CompilerParams(dimension_semantics=("arbitrary",), vmem_limit_bytes=VMEM_LIMIT),
        name="pre",
    )(x, ctx, mod3, mod_ctx3, ng, w, onesq, onesk, qg, kg, cos, sin, perm)


ATTN_ROWS = 256


def _attn_kernel(q_ref, k_ref, v_ref, e_ref, o_ref, k4t_ref, v4_ref):
    @pl.when(pl.program_id(1) == 0)
    def _():
        k4 = jnp.dot(k_ref[...], e_ref[...], preferred_element_type=F32)
        k4t = k4.T
        v4 = jnp.dot(v_ref[...], e_ref[...], preferred_element_type=F32)
        for g in range(N_KV_HEADS):
            k4t_ref[g] = k4t[g * REP_W:(g + 1) * REP_W].astype(BF16)
            v4_ref[g] = v4[:, g * REP_W:(g + 1) * REP_W].astype(BF16)

    lane = lax.broadcasted_iota(jnp.int32, (ATTN_ROWS, REP_W), 1)
    for part in range(q_ref.shape[0] // ATTN_ROWS):
        rows = slice(part * ATTN_ROWS, (part + 1) * ATTN_ROWS)
        for g in range(N_KV_HEADS):
            qg = q_ref[rows, g * REP_W:(g + 1) * REP_W]
            acc = jnp.zeros((ATTN_ROWS, REP_W), F32)
            for r in range(GQA_REP):
                in_head = (lane >= r * HEAD_DIM) & (lane < (r + 1) * HEAD_DIM)
                qr = jnp.where(in_head, qg, jnp.zeros_like(qg))
                s = jnp.dot(qr, k4t_ref[g], preferred_element_type=F32)
                m = jnp.max(s, axis=1, keepdims=True)
                p = jnp.exp(s - m)
                l = jnp.sum(p, axis=1, keepdims=True)
                o = jnp.dot(p.astype(BF16), v4_ref[g], preferred_element_type=F32)
                acc = jnp.where(in_head, o / l, acc)
            o_ref[rows, g * REP_W:(g + 1) * REP_W] = acc.astype(o_ref.dtype)


def _attn_call(q, k_all, v_all, expand, tq):
    bsz, n, _ = q.shape
    nk = k_all.shape[1]
    return pl.pallas_call(
        _attn_kernel,
        out_shape=jax.ShapeDtypeStruct((bsz, n, ATTN_WIDTH), BF16),
        grid=(bsz, n // tq),
        in_specs=[pl.BlockSpec((None, tq, ATTN_WIDTH), lambda b, i: (b, i, 0)),
                  pl.BlockSpec((None, nk, KV_WIDTH), lambda b, i: (b, 0, 0)),
                  pl.BlockSpec((None, nk, KV_WIDTH), lambda b, i: (b, 0, 0)),
                  _const_spec(expand.shape, 2)],
        out_specs=pl.BlockSpec((None, tq, ATTN_WIDTH), lambda b, i: (b, i, 0)),
        scratch_shapes=[pltpu.VMEM((N_KV_HEADS, REP_W, nk), BF16),
                        pltpu.VMEM((N_KV_HEADS, nk, REP_W), BF16)],
        compiler_params=pltpu.CompilerParams(dimension_semantics=("arbitrary", "arbitrary"),
                                             vmem_limit_bytes=VMEM_LIMIT),
        name="attn",
    )(q, k_all, v_all, expand)


def _perm_in_kernel(s_ref, p_ref, o_ref):
    halves = [jnp.dot(s_ref[:, hf * HALF_W:(hf + 1) * HALF_W], p_ref[...], preferred_element_type=F32)
              for hf in range(BLOCK_W // HALF_W)]
    for j in range(GROUPS_PER_BLOCK):
        o_ref[j] = jnp.concatenate([r[:, j * LANES:(j + 1) * LANES] for r in halves], axis=1).astype(o_ref.dtype)


def _perm_in_call(up, pmat, tr):
    rows = up.shape[0]
    return pl.pallas_call(
        _perm_in_kernel,
        out_shape=jax.ShapeDtypeStruct((S5_GROUPS, rows, S5_CW), BF16),
        grid=(N_BLOCKS, rows // tr),
        in_specs=[pl.BlockSpec((tr, BLOCK_W), lambda s, i: (i, s)),
                  _const_spec(pmat.shape, 2)],
        out_specs=pl.BlockSpec((GROUPS_PER_BLOCK, tr, S5_CW), lambda s, i: (s, i, 0)),
        compiler_params=pltpu.CompilerParams(dimension_semantics=("arbitrary", "arbitrary"),
                                             vmem_limit_bytes=VMEM_LIMIT),
        name="perm_in",
    )(up, pmat)


def _perm_out_kernel(y_ref, q_ref, o_ref):
    for hf in range(BLOCK_W // HALF_W):
        ycat = jnp.concatenate([y_ref[j, :, hf * LANES:(hf + 1) * LANES] for j in range(GROUPS_PER_BLOCK)], axis=1)
        o_ref[:, hf * HALF_W:(hf + 1) * HALF_W] = jnp.dot(ycat, q_ref[...],
                                                         preferred_element_type=F32).astype(o_ref.dtype)


def _perm_out_call(yg, qmat, tr):
    rows = yg.shape[1]
    return pl.pallas_call(
        _perm_out_kernel,
        out_shape=jax.ShapeDtypeStruct((rows, N_BLOCKS * BLOCK_W), BF16),
        grid=(N_BLOCKS, rows // tr),
        in_specs=[pl.BlockSpec((GROUPS_PER_BLOCK, tr, S5_CW), lambda s, i: (s, i, 0)),
                  _const_spec(qmat.shape, 2)],
        out_specs=pl.BlockSpec((tr, BLOCK_W), lambda s, i: (i, s)),
        compiler_params=pltpu.CompilerParams(dimension_semantics=("arbitrary", "arbitrary"),
                                             vmem_limit_bytes=VMEM_LIMIT),
        name="perm_out",
    )(yg, qmat)


def _s5_kernel(u_ref, m_ref, sin_ref, g_ref, a_ref, y_ref, buf_ref, *, n_ctx_chunks, n_chunks):
    half = S5_STATE
    gb = u_ref.shape[0]
    for j in range(gb):
        buf_ref[j] = jnp.dot(u_ref[j], sin_ref[j], preferred_element_type=F32)
    a_re = [jnp.broadcast_to(a_ref[j, 0:1, :], (SUBLANES, LANES)) for j in range(gb)]
    a_im = [jnp.broadcast_to(a_ref[j, 1:2, :], (SUBLANES, LANES)) for j in range(gb)]
    is_fwd = lax.broadcasted_iota(jnp.int32, (SUBLANES, LANES), 1) < half

    def step(k, carry):
        pos_b = jnp.where(k < n_ctx_chunks, n_ctx_chunks - 1 - k, n_chunks + n_ctx_chunks - 1 - k)
        rf = pl.multiple_of(k * SUBLANES, SUBLANES)
        rb = pl.multiple_of(pos_b * SUBLANES, SUBLANES)
        out = []
        for j in range(gb):
            h_re, h_im = carry[2 * j], carry[2 * j + 1]
            xf = buf_ref[j, pl.ds(rf, SUBLANES), :]
            xb = buf_ref[j, pl.ds(rb, SUBLANES), :]
            buf_ref[j, pl.ds(rf, SUBLANES), 0:half] = h_re[:, 0:half]
            buf_ref[j, pl.ds(rf, SUBLANES), 2 * half:3 * half] = h_im[:, 0:half]
            buf_ref[j, pl.ds(rb, SUBLANES), half:2 * half] = h_re[:, half:]
            buf_ref[j, pl.ds(rb, SUBLANES), 3 * half:] = h_im[:, half:]
            x_re = jnp.where(is_fwd, xf[:, 0:LANES], xb[:, 0:LANES])
            x_im = jnp.where(is_fwd, xf[:, LANES:], xb[:, LANES:])
            out.append(a_re[j] * h_re - a_im[j] * h_im + x_re)
            out.append(a_re[j] * h_im + a_im[j] * h_re + x_im)
        return tuple(out)

    zero = jnp.zeros((SUBLANES, LANES), F32)
    lax.fori_loop(0, n_chunks, step, (zero,) * (2 * gb))

    r0 = n_ctx_chunks * SUBLANES
    for j in range(gb):
        y = jnp.dot(u_ref[j, r0:, :], m_ref[j], preferred_element_type=F32)
        y = y + jnp.dot(buf_ref[j, r0:, :].astype(BF16), g_ref[j], preferred_element_type=F32)
        y_ref[j] = y.astype(y_ref.dtype)


def _s5_call(ug, m, sin, gmat, a16, n_ctx_chunks, gb):
    groups, rows, _ = ug.shape
    n_chunks = rows // SUBLANES
    out_rows = rows - n_ctx_chunks * SUBLANES
    mat = lambda: pl.BlockSpec((gb, S5_CW, S5_CW), lambda g: (g, 0, 0))
    return pl.pallas_call(
        functools.partial(_s5_kernel, n_ctx_chunks=n_ctx_chunks, n_chunks=n_chunks),
        out_shape=jax.ShapeDtypeStruct((groups, out_rows, S5_CW), BF16),
        grid=(groups // gb,),
        in_specs=[pl.BlockSpec((gb, rows, S5_CW), lambda g: (g, 0, 0)),
                  mat(), mat(), mat(),
                  pl.BlockSpec((gb, 2, LANES), lambda g: (g, 0, 0))],
        out_specs=pl.BlockSpec((gb, out_rows, S5_CW), lambda g: (g, 0, 0)),
        scratch_shapes=[pltpu.VMEM((gb, rows, S5_CW), F32)],
        compiler_params=pltpu.CompilerParams(dimension_semantics=("arbitrary",),
                                             vmem_limit_bytes=VMEM_LIMIT),
        name="s5",
    )(ug, m, sin, gmat, a16)


def _s5_ops_kernel(*refs):
    for j in range(refs[0].shape[0]):
        _s5_ops_group(*(r.at[j] for r in refs))


def _s5_ops_group(lam_ref, ldt_ref, bt_ref, ct_ref, d_ref, m_ref, sin_ref, g_ref, a_ref):
    T, H = S5_CHUNK, S5_GROUP
    lr = jnp.minimum(lam_ref[0:1, :], -1e-4)
    li = lam_ref[1:2, :]
    dt = jnp.exp(ldt_ref[...])
    taus = lax.broadcasted_iota(jnp.int32, (3 * SUBLANES, LANES), 0).astype(F32)
    mag = jnp.exp(lr * dt * taus)
    pw_r = mag * jnp.cos(li * dt * taus)
    pw_i = mag * jnp.sin(li * dt * taus)
    nr, ni = pw_r[1:2] - 1.0, pw_i[1:2]
    den = lr * lr + li * li
    cf_r = (nr * lr + ni * li) / den
    cf_i = (ni * lr - nr * li) / den
    bb_r = cf_r * bt_ref[0] - cf_i * bt_ref[1]
    bb_i = cf_r * bt_ref[1] + cf_i * bt_ref[0]
    is_fwd = lax.broadcasted_iota(jnp.int32, (H, LANES), 1) < S5_STATE

    def powers(tau_f, tau_b):
        pick = lambda pw, s: jnp.where(is_fwd, jnp.broadcast_to(pw[tau_f(s):tau_f(s) + 1], (H, LANES)),
                                       jnp.broadcast_to(pw[tau_b(s):tau_b(s) + 1], (H, LANES)))
        return (jnp.concatenate([pick(pw_r, s) for s in range(T)], axis=0),
                jnp.concatenate([pick(pw_i, s) for s in range(T)], axis=0))

    tile = lambda a: jnp.concatenate([a] * T, axis=0)
    bbr, bbi, cr, ci = tile(bb_r), tile(bb_i), tile(ct_ref[0]), tile(ct_ref[1])

    er, ei = powers(lambda s: T - 1 - s, lambda s: s)
    sin_ref[:, 0:LANES] = (er * bbr - ei * bbi).astype(sin_ref.dtype)
    sin_ref[:, LANES:] = (er * bbi + ei * bbr).astype(sin_ref.dtype)

    er, ei = powers(lambda t: t + 1, lambda t: T - t)
    gt = jnp.concatenate([er * cr - ei * ci, -(er * ci + ei * cr)], axis=1)
    g_ref[...] = gt.T.astype(g_ref.dtype)

    er, ei = powers(lambda a: a, lambda a: T - 1 - a)
    cp = jnp.concatenate([er * cr - ei * ci, er * ci + ei * cr], axis=1)
    zero = jnp.zeros_like(bb_r)
    lhs = jnp.concatenate([jnp.concatenate([jnp.where(is_fwd, bb_r, zero), jnp.where(is_fwd, -bb_i, zero)], axis=1),
                           jnp.concatenate([jnp.where(is_fwd, zero, bb_r), jnp.where(is_fwd, zero, -bb_i)], axis=1)],
                          axis=0)
    kr = lax.dot_general(lhs, cp, (((1,), (1,)), ((), ())), preferred_element_type=F32,
                         precision=lax.Precision.HIGHEST)
    pad = jnp.zeros((H, S5_CW), F32)
    wide_f = jnp.concatenate([pad, kr[0:H]], axis=1)
    wide_b = jnp.concatenate([kr[H:], pad], axis=1)
    lane = lax.broadcasted_iota(jnp.int32, (H, S5_CW), 1)
    row = lax.broadcasted_iota(jnp.int32, (H, S5_CW), 0)
    skip = jnp.broadcast_to(d_ref[...], (H, S5_CW))
    for s in range(T):
        blk_f = pltpu.roll(wide_f, H * s, 1)[:, S5_CW:] if s else wide_f[:, S5_CW:]
        shift_b = (2 * S5_CW - H * (T - 1 - s)) % (2 * S5_CW)
        blk_b = (pltpu.roll(wide_b, shift_b, 1) if shift_b else wide_b)[:, :S5_CW]
        diag = jnp.where(lane == H * s + row, skip, 0.0)
        m_ref[H * s:H * (s + 1), :] = (blk_f + blk_b + diag).astype(m_ref.dtype)
    a_ref[0:1, :] = pw_r[T:T + 1]
    a_ref[1:2, :] = pw_i[T:T + 1]


def _s5_ops_call(lam, ldt, bt, ct, dt_tiled):
    groups = lam.shape[0]
    gb = 4
    mat = lambda: pl.BlockSpec((gb, S5_CW, S5_CW), lambda g: (g, 0, 0))
    vec = lambda a: pl.BlockSpec((gb,) + a.shape[1:], lambda g: (g,) + (0,) * (a.ndim - 1))
    mshape = jax.ShapeDtypeStruct((groups, S5_CW, S5_CW), BF16)
    return pl.pallas_call(
        _s5_ops_kernel,
        out_shape=(mshape, mshape, mshape, jax.ShapeDtypeStruct((groups, 2, LANES), F32)),
        grid=(groups // gb,),
        in_specs=[vec(lam), vec(ldt), vec(bt), vec(ct), vec(dt_tiled)],
        out_specs=(mat(), mat(), mat(), pl.BlockSpec((gb, 2, LANES), lambda g: (g, 0, 0))),
        compiler_params=pltpu.CompilerParams(dimension_semantics=("arbitrary",)),
        name="s5_ops",
    )(lam, ldt, bt, ct, dt_tiled)


def _s5_operators(lam_re, lam_im, log_dt, b_re, b_im, c_re, c_im, d_skip):
    G, P, H = S5_GROUPS, S5_STATE, S5_GROUP
    lam = jnp.stack([lam_re, lam_im]).astype(F32).transpose(2, 0, 1, 3).reshape(G, 2, 2 * P)
    ldt = jnp.repeat(log_dt.astype(F32).T, P, axis=1).reshape(G, 1, 2 * P)
    bt = jnp.stack([b_re, b_im]).astype(F32).transpose(2, 0, 4, 1, 3).reshape(G, 2, H, 2 * P)
    ct = jnp.stack([c_re, c_im]).astype(F32).transpose(2, 0, 3, 1, 4).reshape(G, 2, H, 2 * P)
    dt_tiled = jnp.tile(d_skip.astype(F32).reshape(G, 1, H), (1, 1, S5_CHUNK))
    return _s5_ops_call(lam, ldt, bt, ct, dt_tiled)


def _final_kernel(x_ref, mod_ref, ng_ref, wg_ref, ya_ref, yp_ref, permt_ref, wglu_ref, bglu_ref, wa_ref, wb_ref,
                  wo_ref, fg_ref, o_ref):
    bsz, tt, _ = x_ref.shape
    rows = bsz * tt
    x3 = x_ref[...]
    xn = _modulated_norm(x3, mod_ref, ng_ref).reshape(rows, D_MODEL).astype(BF16)
    gates = jnp.dot(xn, wg_ref[...], preferred_element_type=F32)
    o1, o2, o3 = ATTN_WIDTH, ATTN_WIDTH + S5_WIDTH, ATTN_WIDTH + S5_WIDTH + D_MODEL
    slabs = [jnp.concatenate([yp_ref[:, blk * BLOCK_W + t * LANES: blk * BLOCK_W + (t + 1) * LANES]
                              for blk in range(N_BLOCKS)], axis=1) for t in range(S5_CHUNK)]
    y = jnp.dot(permt_ref[...], jnp.concatenate(slabs, axis=0), preferred_element_type=F32)
    z = y * (0.5 * (1.0 + jnp.tanh(0.7978845608028654 * (y + 0.044715 * (y * y * y)))))
    zz = z * jax.nn.sigmoid(jnp.dot(z.astype(BF16), wglu_ref[...], preferred_element_type=F32) + bglu_ref[...])
    ya = ya_ref[...].reshape(rows, ATTN_WIDTH).astype(F32)
    ta = (ya * _silu(gates[:, 0:o1])).astype(BF16)
    tb = (zz * _silu(gates[:, o1:o2])).astype(BF16)
    pa = jnp.dot(ta, wa_ref[...], preferred_element_type=F32)
    pb = jnp.dot(tb, wb_ref[...], preferred_element_type=F32)
    mix = jax.nn.sigmoid(gates[:, o2:o3]) * pa + jax.nn.sigmoid(gates[:, o3:]) * pb
    o = jnp.dot(mix.astype(BF16), wo_ref[...], preferred_element_type=F32).reshape(bsz, tt, D_MODEL)
    h = x3 + mod_ref[:, :, 2 * D_MODEL:] * o
    ms = jnp.mean(h * h, axis=-1, keepdims=True)
    o_ref[...] = h * lax.rsqrt(ms + EPS) * fg_ref[...]


def _final_call(x, mod3, ng, wg, ya, yp, permt, wglu, bglu, wa, wb, wo, fg):
    bsz, n, _ = x.shape
    tt = TOK_TILE
    up_rows = tt // S5_CHUNK * bsz
    tok = lambda width: pl.BlockSpec((bsz, tt, width), lambda i: (0, i, 0))
    c = lambda shape: _const_spec(shape, 1)
    return pl.pallas_call(
        _final_kernel,
        out_shape=jax.ShapeDtypeStruct((bsz, n, D_MODEL), F32),
        grid=(n // tt,),
        in_specs=[tok(D_MODEL),
                  pl.BlockSpec((bsz, 1, 3 * D_MODEL), lambda i: (0, 0, 0)),
                  c((1, D_MODEL)), c(wg.shape),
                  tok(ATTN_WIDTH),
                  pl.BlockSpec((up_rows, N_BLOCKS * BLOCK_W), lambda i: (i, 0)),
                  c(permt.shape), c(wglu.shape), c((1, S5_WIDTH)), c(wa.shape), c(wb.shape), c(wo.shape),
                  c((1, D_MODEL))],
        out_specs=tok(D_MODEL),
        compiler_params=pltpu.CompilerParams(dimension_semantics=("arbitrary",), vmem_limit_bytes=VMEM_LIMIT),
        name="final",
    )(x, mod3, ng, wg, ya, yp, permt, wglu, bglu, wa, wb, wo, fg)


def _rope_tables(n):
    rows = n // GRID_W
    row_ids = np.repeat(np.arange(rows, dtype=np.float64), GRID_W)
    col_ids = np.tile(np.arange(GRID_W, dtype=np.float64), rows)
    freqs = ROPE_THETA ** (-np.arange(ROPE_FREQS, dtype=np.float64) / ROPE_FREQS)
    ang_r, ang_c = row_ids[:, None] * freqs, col_ids[:, None] * freqs
    cos = np.concatenate([np.cos(ang_r)] * 2 + [np.cos(ang_c)] * 2, axis=1)
    sin = np.concatenate([-np.sin(ang_r), np.sin(ang_r), -np.sin(ang_c), np.sin(ang_c)], axis=1)
    reps = LANES // HEAD_DIM
    return (jnp.asarray(np.tile(cos, (1, reps)).astype(np.float32)),
            jnp.asarray(np.tile(sin, (1, reps)).astype(np.float32)))


def _one_hot(match):
    return jnp.asarray(np.ascontiguousarray(match).astype(BF16))


def _block_ones(width):
    idx = np.arange(width) // HEAD_DIM
    return _one_hot(idx[:, None] == idx[None, :])


def _row_perm(bsz, tt, transpose=False):
    chunks = tt // S5_CHUNK
    r = np.arange(bsz * tt)
    t, pc, b = r // (chunks * bsz), (r // bsz) % chunks, r % bsz
    src = b * tt + pc * S5_CHUNK + t
    match = src[:, None] == np.arange(bsz * tt)[None, :]
    return _one_hot(match.T if transpose else match)


def _lane_perm(transpose=False):
    r = np.arange(HALF_W)
    t, j, h = r // LANES, (r % LANES) // S5_GROUP, r % S5_GROUP
    dst = j * LANES + t * S5_GROUP + h
    match = dst[:, None] == np.arange(HALF_W)[None, :]
    return _one_hot(match.T if transpose else match)


def _kv_expand():
    col = np.arange(N_KV_HEADS * REP_W)
    src = (col // REP_W) * HEAD_DIM + col % HEAD_DIM
    return _one_hot(np.arange(KV_WIDTH)[:, None] == src[None, :])


def kernel(x, c, ctx, c_ctx, norm_g, w_ada, b_ada, w_in, q_norm_g, k_norm_g, s5_lam_re, s5_lam_im, s5_log_dt,
           s5_b_re, s5_b_im, s5_c_re, s5_c_im, s5_d, w_glu, b_glu, w_branch_attn, w_branch_s5, w_out,
           final_norm_g):
    assert w_in.shape[0] == 1, "single-layer block"
    bsz, n, _ = x.shape
    n_ctx = ctx.shape[1]
    assert n % TOK_TILE == 0 and n_ctx % TOK_TILE == 0 and bsz == SUBLANES

    ada_rows = 2 * SUBLANES
    cc = jnp.concatenate([c, c_ctx[None], jnp.zeros((ada_rows - bsz - 1, D_MODEL), F32)], axis=0)
    mod = _ada_call(cc, w_ada[0], b_ada[0][None])
    mod3 = mod[:bsz].reshape(bsz, 1, 3 * D_MODEL)
    mod_ctx3 = mod[bsz:bsz + 1].reshape(1, 1, 3 * D_MODEL)

    offs = [0]
    for s in IN_SIZES:
        offs.append(offs[-1] + s)
    w_bf = w_in[0].astype(BF16)
    w_pre = jnp.concatenate([w_bf[:, offs[0]:offs[3]], w_bf[:, offs[4]:offs[5]]], axis=1)
    w_gates = jnp.concatenate([w_bf[:, offs[3]:offs[4]], w_bf[:, offs[5]:]], axis=1)

    ng = norm_g[0][None]
    qg = jnp.tile(q_norm_g[0], N_HEADS)[None]
    kg = jnp.tile(k_norm_g[0], N_KV_HEADS)[None]
    onesq, onesk = _block_ones(ATTN_WIDTH), _block_ones(KV_WIDTH)
    cos, sin = _rope_tables(n)
    row_perm = _row_perm(bsz, TOK_TILE)
    lane_perm = _lane_perm()

    q, k_all, v_all, up = _pre_call(x, ctx, mod3, mod_ctx3, ng, w_pre, onesq, onesk, qg, kg, cos, sin, row_perm)

    y_attn = _attn_call(q, k_all, v_all, _kv_expand(), tq=1024)

    ug = _perm_in_call(up, lane_perm, tr=384)
    m, s_in, gmat, a16 = _s5_operators(s5_lam_re[0], s5_lam_im[0], s5_log_dt[0], s5_b_re[0], s5_b_im[0],
                                       s5_c_re[0], s5_c_im[0], s5_d[0])
    yg = _s5_call(ug, m, s_in, gmat, a16, n_ctx // S5_CHUNK, gb=GROUPS_PER_BLOCK)
    yp = _perm_out_call(yg, _lane_perm(transpose=True), tr=256)

    return _final_call(x, mod3, ng, w_gates, y_attn, yp, _row_perm(bsz, TOK_TILE, transpose=True),
                       w_glu[0].astype(BF16), b_glu[0][None],
                       w_branch_attn[0].astype(BF16), w_branch_s5[0].astype(BF16), w_out[0].astype(BF16),
                       final_norm_g[None])
```

```python
import functools

import numpy as np
import jax
import jax.numpy as jnp
from jax import lax
from jax.experimental import pallas as pl
from jax.experimental.pallas import tpu as pltpu

D_MODEL = 1024
GRID_W = 64
N_HEADS = 8
N_KV_HEADS = 2
HEAD_DIM = 64
GQA_REP = N_HEADS // N_KV_HEADS
ATTN_WIDTH = N_HEADS * HEAD_DIM
KV_WIDTH = N_KV_HEADS * HEAD_DIM
ATTN_SCALE = HEAD_DIM ** -0.5
ROPE_THETA = 10000.0
ROPE_FREQS = HEAD_DIM // 4
S5_WIDTH = 512
S5_GROUP = 16
S5_GROUPS = S5_WIDTH // S5_GROUP
S5_STATE = 64
EPS = 1e-6
IN_SIZES = (ATTN_WIDTH, KV_WIDTH, KV_WIDTH, ATTN_WIDTH, S5_WIDTH, S5_WIDTH, D_MODEL, D_MODEL)

LANES = 128
SUBLANES = 8
S5_CHUNK = 16
S5_CW = S5_CHUNK * S5_GROUP
GROUPS_PER_BLOCK = LANES // S5_GROUP
N_BLOCKS = S5_WIDTH // LANES
BLOCK_W = S5_CHUNK * LANES
HALF_W = BLOCK_W // 2
TOK_TILE = 128
PERM_TOK = 64
REP_W = GQA_REP * HEAD_DIM
VMEM_LIMIT = 56 * 1024 * 1024

F32 = jnp.float32
BF16 = jnp.bfloat16


def _silu(t):
    return t * jax.nn.sigmoid(t)


def _modulated_norm(x3, mod_ref, ng_ref):
    ms = jnp.mean(x3 * x3, axis=-1, keepdims=True)
    y = x3 * lax.rsqrt(ms + EPS) * ng_ref[...]
    return y * (1.0 + mod_ref[:, :, D_MODEL:2 * D_MODEL]) + mod_ref[:, :, 0:D_MODEL]


def _head_rmsnorm(t, ones_ref, g_ref):
    ss = jnp.dot((t * t).astype(BF16), ones_ref[...], preferred_element_type=F32)
    return t * lax.rsqrt(ss * (1.0 / HEAD_DIM) + EPS) * g_ref[...]


def _rope(t, cos, sin_signed):
    rows = t.shape[0]
    lane = lax.broadcasted_iota(jnp.int32, (rows, LANES), 1)
    first = (lane & ROPE_FREQS) == 0
    outs = []
    for j in range(t.shape[1] // LANES):
        blk = t[:, j * LANES:(j + 1) * LANES]
        partner = jnp.where(first, pltpu.roll(blk, LANES - ROPE_FREQS, 1), pltpu.roll(blk, ROPE_FREQS, 1))
        outs.append(blk * cos + partner * sin_signed)
    return outs[0] if len(outs) == 1 else jnp.concatenate(outs, axis=1)


def _const_spec(shape, grid_rank):
    zeros = (0,) * len(shape)
    return pl.BlockSpec(shape, lambda *_: zeros, pipeline_mode=pl.Buffered(1))


def _ada_kernel(c_ref, w_ref, b_ref, o_ref):
    s = _silu(c_ref[...])
    w = w_ref[...]
    s_hi, w_hi = s.astype(BF16), w.astype(BF16)
    s_lo = (s - s_hi.astype(F32)).astype(BF16)
    w_lo = (w - w_hi.astype(F32)).astype(BF16)
    dot = functools.partial(jnp.dot, preferred_element_type=F32)
    o_ref[...] = dot(s_hi, w_hi) + dot(s_lo, w_hi) + dot(s_hi, w_lo) + b_ref[...]


def _ada_call(cc, w, b):
    rows, n = cc.shape[0], w.shape[1]
    tn = 512
    return pl.pallas_call(
        _ada_kernel,
        out_shape=jax.ShapeDtypeStruct((rows, n), F32),
        grid=(n // tn,),
        in_specs=[pl.BlockSpec((rows, D_MODEL), lambda j: (0, 0)),
                  pl.BlockSpec((D_MODEL, tn), lambda j: (0, j)),
                  pl.BlockSpec((1, tn), lambda j: (0, j))],
        out_specs=pl.BlockSpec((rows, tn), lambda j: (0, j)),
        compiler_params=pltpu.CompilerParams(dimension_semantics=("arbitrary",)),
        name="ada",
    )(cc, w, b)


def _store_chunk_major(u, perm_ref, up_out, bsz):
    tt = u.shape[0] // bsz
    u3 = u.astype(BF16).reshape(bsz, tt, S5_WIDTH)
    rows = PERM_TOK // S5_CHUNK * bsz
    for part in range(tt // PERM_TOK):
        up = u3[:, part * PERM_TOK:(part + 1) * PERM_TOK, :].reshape(bsz * PERM_TOK, S5_WIDTH)
        r = jnp.dot(perm_ref[...], up, preferred_element_type=F32).astype(BF16)
        for t in range(S5_CHUNK):
            for blk in range(N_BLOCKS):
                up_out[part * rows:(part + 1) * rows, blk * BLOCK_W + t * LANES: blk * BLOCK_W + (t + 1) * LANES] = (
                    r[t * rows:(t + 1) * rows, blk * LANES:(blk + 1) * LANES])


def _pre_kernel(x_ref, c_ref, mod_ref, modc_ref, ng_ref, w_ref, onesq_ref, onesk_ref, qg_ref, kg_ref, cos_ref, sin_ref,
                perm_ref, q_out, k_out, v_out, up_out, *, ctx_steps):
    bsz, tt, _ = x_ref.shape
    o1, o2, o3 = ATTN_WIDTH, ATTN_WIDTH + KV_WIDTH, ATTN_WIDTH + 2 * KV_WIDTH
    step = pl.program_id(0)

    @pl.when(step < ctx_steps)
    def _():
        xn = _modulated_norm(c_ref[...], modc_ref, ng_ref).reshape(bsz * tt, D_MODEL).astype(BF16)
        p = jnp.dot(xn, w_ref[:, o1:], preferred_element_type=F32)
        k = _head_rmsnorm(p[:, 0:KV_WIDTH], onesk_ref, kg_ref)
        k_out[...] = k.astype(BF16).reshape(bsz, tt, KV_WIDTH)
        v_out[...] = p[:, KV_WIDTH:2 * KV_WIDTH].astype(BF16).reshape(bsz, tt, KV_WIDTH)
        _store_chunk_major(p[:, 2 * KV_WIDTH:], perm_ref, up_out, bsz)

    @pl.when(step >= ctx_steps)
    def _():
        xn = _modulated_norm(x_ref[...], mod_ref, ng_ref).reshape(bsz * tt, D_MODEL).astype(BF16)
        p = jnp.dot(xn, w_ref[...], preferred_element_type=F32)
        cos = jnp.concatenate([cos_ref[...]] * bsz, axis=0)
        sin = jnp.concatenate([sin_ref[...]] * bsz, axis=0)
        q = _rope(_head_rmsnorm(p[:, 0:o1], onesq_ref, qg_ref), cos, sin)
        k = _rope(_head_rmsnorm(p[:, o1:o2], onesk_ref, kg_ref), cos, sin)
        q_out[...] = (q * ATTN_SCALE).astype(BF16).reshape(bsz, tt, ATTN_WIDTH)
        k_out[...] = k.astype(BF16).reshape(bsz, tt, KV_WIDTH)
        v_out[...] = p[:, o2:o3].astype(BF16).reshape(bsz, tt, KV_WIDTH)
        _store_chunk_major(p[:, o3:], perm_ref, up_out, bsz)


def _pre_call(x, ctx, mod3, mod_ctx3, ng, w, onesq, onesk, qg, kg, cos, sin, perm):
    bsz, n, _ = x.shape
    n_ctx = ctx.shape[1]
    tt = TOK_TILE
    up_rows = tt // S5_CHUNK * bsz
    ctx_steps = n_ctx // tt
    lat = lambda i: jnp.maximum(i - ctx_steps, 0)
    c = lambda shape: _const_spec(shape, 1)
    return pl.pallas_call(
        functools.partial(_pre_kernel, ctx_steps=ctx_steps),
        out_shape=(jax.ShapeDtypeStruct((bsz, n, ATTN_WIDTH), BF16),
                   jax.ShapeDtypeStruct((bsz, n_ctx + n, KV_WIDTH), BF16),
                   jax.ShapeDtypeStruct((bsz, n_ctx + n, KV_WIDTH), BF16),
                   jax.ShapeDtypeStruct(((n_ctx + n) // S5_CHUNK * bsz, N_BLOCKS * BLOCK_W), BF16)),
        grid=(ctx_steps + n // tt,),
        in_specs=[pl.BlockSpec((bsz, tt, D_MODEL), lambda i: (0, lat(i), 0)),
                  pl.BlockSpec((bsz, tt, D_MODEL), lambda i: (0, jnp.minimum(i, ctx_steps - 1), 0)),
                  c((bsz, 1, 3 * D_MODEL)), c((1, 1, 3 * D_MODEL)),
                  c((1, D_MODEL)), c(w.shape), c(onesq.shape), c(onesk.shape),
                  c((1, ATTN_WIDTH)), c((1, KV_WIDTH)),
                  pl.BlockSpec((tt, LANES), lambda i: (lat(i), 0)),
                  pl.BlockSpec((tt, LANES), lambda i: (lat(i), 0)),
                  c(perm.shape)],
        out_specs=(pl.BlockSpec((bsz, tt, ATTN_WIDTH), lambda i: (0, lat(i), 0)),
                   pl.BlockSpec((bsz, tt, KV_WIDTH), lambda i: (0, i, 0)),
                   pl.BlockSpec((bsz, tt, KV_WIDTH), lambda i: (0, i, 0)),
                   pl.BlockSpec((up_rows, N_BLOCKS * BLOCK_W), lambda i: (i, 0))),
        compiler_params=pltpu.CompilerParams(dimension_semantics=("arbitrary",), vmem_limit_bytes=VMEM_LIMIT),
        name="pre",
    )(x, ctx, mod3, mod_ctx3, ng, w, onesq, onesk, qg, kg, cos, sin, perm)


ATTN_ROWS = 256


def _attn_kernel(q_ref, k_ref, v_ref, e_ref, o_ref, k4t_ref, v4_ref):
    @pl.when(pl.program_id(1) == 0)
    def _():
        k4 = jnp.dot(k_ref[...], e_ref[...], preferred_element_type=F32)
        k4t = k4.T
        v4 = jnp.dot(v_ref[...], e_ref[...], preferred_element_type=F32)
        for g in range(N_KV_HEADS):
            k4t_ref[g] = k4t[g * REP_W:(g + 1) * REP_W].astype(BF16)
            v4_ref[g] = v4[:, g * REP_W:(g + 1) * REP_W].astype(BF16)

    lane = lax.broadcasted_iota(jnp.int32, (ATTN_ROWS, REP_W), 1)
    for part in range(q_ref.shape[0] // ATTN_ROWS):
        rows = slice(part * ATTN_ROWS, (part + 1) * ATTN_ROWS)
        for g in range(N_KV_HEADS):
            qg = q_ref[rows, g * REP_W:(g + 1) * REP_W]
            acc = jnp.zeros((ATTN_ROWS, REP_W), F32)
            for r in range(GQA_REP):
                in_head = (lane >= r * HEAD_DIM) & (lane < (r + 1) * HEAD_DIM)
                qr = jnp.where(in_head, qg, jnp.zeros_like(qg))
                s = jnp.dot(qr, k4t_ref[g], preferred_element_type=F32)
                m = jnp.max(s, axis=1, keepdims=True)
                p = jnp.exp(s - m)
                l = jnp.sum(p, axis=1, keepdims=True)
                o = jnp.dot(p.astype(BF16), v4_ref[g], preferred_element_type=F32)
                acc = jnp.where(in_head, o / l, acc)
            o_ref[rows, g * REP_W:(g + 1) * REP_W] = acc.astype(o_ref.dtype)


def _attn_call(q, k_all, v_all, expand, tq):
    bsz, n, _ = q.shape
    nk = k_all.shape[1]
    return pl.pallas_call(
        _attn_kernel,
        out_shape=jax.ShapeDtypeStruct((bsz, n, ATTN_WIDTH), BF16),
        grid=(bsz, n // tq),
        in_specs=[pl.BlockSpec((None, tq, ATTN_WIDTH), lambda b, i: (b, i, 0)),
                  pl.BlockSpec((None, nk, KV_WIDTH), lambda b, i: (b, 0, 0)),
                  pl.BlockSpec((None, nk, KV_WIDTH), lambda b, i: (b, 0, 0)),
                  _const_spec(expand.shape, 2)],
        out_specs=pl.BlockSpec((None, tq, ATTN_WIDTH), lambda b, i: (b, i, 0)),
        scratch_shapes=[pltpu.VMEM((N_KV_HEADS, REP_W, nk), BF16),
                        pltpu.VMEM((N_KV_HEADS, nk, REP_W), BF16)],
        compiler_params=pltpu.CompilerParams(dimension_semantics=("arbitrary", "arbitrary"),
                                             vmem_limit_bytes=VMEM_LIMIT),
        name="attn",
    )(q, k_all, v_all, expand)


def _perm_in_kernel(s_ref, p_ref, o_ref):
    halves = [jnp.dot(s_ref[:, hf * HALF_W:(hf + 1) * HALF_W], p_ref[...], preferred_element_type=F32)
              for hf in range(BLOCK_W // HALF_W)]
    for j in range(GROUPS_PER_BLOCK):
        o_ref[j] = jnp.concatenate([r[:, j * LANES:(j + 1) * LANES] for r in halves], axis=1).astype(o_ref.dtype)


def _perm_in_call(up, pmat, tr):
    rows = up.shape[0]
    return pl.pallas_call(
        _perm_in_kernel,
        out_shape=jax.ShapeDtypeStruct((S5_GROUPS, rows, S5_CW), BF16),
        grid=(N_BLOCKS, rows // tr),
        in_specs=[pl.BlockSpec((tr, BLOCK_W), lambda s, i: (i, s)),
                  _const_spec(pmat.shape, 2)],
        out_specs=pl.BlockSpec((GROUPS_PER_BLOCK, tr, S5_CW), lambda s, i: (s, i, 0)),
        compiler_params=pltpu.CompilerParams(dimension_semantics=("arbitrary", "arbitrary"),
                                             vmem_limit_bytes=VMEM_LIMIT),
        name="perm_in",
    )(up, pmat)


def _perm_out_kernel(y_ref, q_ref, o_ref):
    for hf in range(BLOCK_W // HALF_W):
        ycat = jnp.concatenate([y_ref[j, :, hf * LANES:(hf + 1) * LANES] for j in range(GROUPS_PER_BLOCK)], axis=1)
        o_ref[:, hf * HALF_W:(hf + 1) * HALF_W] = jnp.dot(ycat, q_ref[...],
                                                         preferred_element_type=F32).astype(o_ref.dtype)


def _perm_out_call(yg, qmat, tr):
    rows = yg.shape[1]
    return pl.pallas_call(
        _perm_out_kernel,
        out_shape=jax.ShapeDtypeStruct((rows, N_BLOCKS * BLOCK_W), BF16),
        grid=(N_BLOCKS, rows // tr),
        in_specs=[pl.BlockSpec((GROUPS_PER_BLOCK, tr, S5_CW), lambda s, i: (s, i, 0)),
                  _const_spec(qmat.shape, 2)],
        out_specs=pl.BlockSpec((tr, BLOCK_W), lambda s, i: (i, s)),
        compiler_params=pltpu.CompilerParams(dimension_semantics=("arbitrary", "arbitrary"),
                                             vmem_limit_bytes=VMEM_LIMIT),
        name="perm_out",
    )(yg, qmat)


def _s5_kernel(u_ref, m_ref, sin_ref, g_ref, a_ref, y_ref, buf_ref, *, n_ctx_chunks, n_chunks):
    half = S5_STATE
    gb = u_ref.shape[0]
    for j in range(gb):
        buf_ref[j] = jnp.dot(u_ref[j], sin_ref[j], preferred_element_type=F32)
    a_re = [jnp.broadcast_to(a_ref[j, 0:1, :], (SUBLANES, LANES)) for j in range(gb)]
    a_im = [jnp.broadcast_to(a_ref[j, 1:2, :], (SUBLANES, LANES)) for j in range(gb)]
    is_fwd = lax.broadcasted_iota(jnp.int32, (SUBLANES, LANES), 1) < half

    def step(k, carry):
        pos_b = jnp.where(k < n_ctx_chunks, n_ctx_chunks - 1 - k, n_chunks + n_ctx_chunks - 1 - k)
        rf = pl.multiple_of(k * SUBLANES, SUBLANES)
        rb = pl.multiple_of(pos_b * SUBLANES, SUBLANES)
        out = []
        for j in range(gb):
            h_re, h_im = carry[2 * j], carry[2 * j + 1]
            xf = buf_ref[j, pl.ds(rf, SUBLANES), :]
            xb = buf_ref[j, pl.ds(rb, SUBLANES), :]
            buf_ref[j, pl.ds(rf, SUBLANES), 0:half] = h_re[:, 0:half]
            buf_ref[j, pl.ds(rf, SUBLANES), 2 * half:3 * half] = h_im[:, 0:half]
            buf_ref[j, pl.ds(rb, SUBLANES), half:2 * half] = h_re[:, half:]
            buf_ref[j, pl.ds(rb, SUBLANES), 3 * half:] = h_im[:, half:]
            x_re = jnp.where(is_fwd, xf[:, 0:LANES], xb[:, 0:LANES])
            x_im = jnp.where(is_fwd, xf[:, LANES:], xb[:, LANES:])
            out.append(a_re[j] * h_re - a_im[j] * h_im + x_re)
            out.append(a_re[j] * h_im + a_im[j] * h_re + x_im)
        return tuple(out)

    zero = jnp.zeros((SUBLANES, LANES), F32)
    lax.fori_loop(0, n_chunks, step, (zero,) * (2 * gb))

    r0 = n_ctx_chunks * SUBLANES
    for j in range(gb):
        y = jnp.dot(u_ref[j, r0:, :], m_ref[j], preferred_element_type=F32)
        y = y + jnp.dot(buf_ref[j, r0:, :].astype(BF16), g_ref[j], preferred_element_type=F32)
        y_ref[j] = y.astype(y_ref.dtype)


def _s5_call(ug, m, sin, gmat, a16, n_ctx_chunks, gb):
    groups, rows, _ = ug.shape
    n_chunks = rows // SUBLANES
    out_rows = rows - n_ctx_chunks * SUBLANES
    mat = lambda: pl.BlockSpec((gb, S5_CW, S5_CW), lambda g: (g, 0, 0))
    return pl.pallas_call(
        functools.partial(_s5_kernel, n_ctx_chunks=n_ctx_chunks, n_chunks=n_chunks),
        out_shape=jax.ShapeDtypeStruct((groups, out_rows, S5_CW), BF16),
        grid=(groups // gb,),
        in_specs=[pl.BlockSpec((gb, rows, S5_CW), lambda g: (g, 0, 0)),
                  mat(), mat(), mat(),
                  pl.BlockSpec((gb, 2, LANES), lambda g: (g, 0, 0))],
        out_specs=pl.BlockSpec((gb, out_rows, S5_CW), lambda g: (g, 0, 0)),
        scratch_shapes=[pltpu.VMEM((gb, rows, S5_CW), F32)],
        compiler_params=pltpu.CompilerParams(dimension_semantics=("arbitrary",),
                                             vmem_limit_bytes=VMEM_LIMIT),
        name="s5",
    )(ug, m, sin, gmat, a16)


def _s5_ops_kernel(*refs):
    for j in range(refs[0].shape[0]):
        _s5_ops_group(*(r.at[j] for r in refs))


def _s5_ops_group(lam_ref, ldt_ref, bt_ref, ct_ref, d_ref, m_ref, sin_ref, g_ref, a_ref):
    T, H = S5_CHUNK, S5_GROUP
    lr = jnp.minimum(lam_ref[0:1, :], -1e-4)
    li = lam_ref[1:2, :]
    dt = jnp.exp(ldt_ref[...])
    taus = lax.broadcasted_iota(jnp.int32, (3 * SUBLANES, LANES), 0).astype(F32)
    mag = jnp.exp(lr * dt * taus)
    pw_r = mag * jnp.cos(li * dt * taus)
    pw_i = mag * jnp.sin(li * dt * taus)
    nr, ni = pw_r[1:2] - 1.0, pw_i[1:2]
    den = lr * lr + li * li
    cf_r = (nr * lr + ni * li) / den
    cf_i = (ni * lr - nr * li) / den
    bb_r = cf_r * bt_ref[0] - cf_i * bt_ref[1]
    bb_i = cf_r * bt_ref[1] + cf_i * bt_ref[0]
    is_fwd = lax.broadcasted_iota(jnp.int32, (H, LANES), 1) < S5_STATE

    def powers(tau_f, tau_b):
        pick = lambda pw, s: jnp.where(is_fwd, jnp.broadcast_to(pw[tau_f(s):tau_f(s) + 1], (H, LANES)),
                                       jnp.broadcast_to(pw[tau_b(s):tau_b(s) + 1], (H, LANES)))
        return (jnp.concatenate([pick(pw_r, s) for s in range(T)], axis=0),
                jnp.concatenate([pick(pw_i, s) for s in range(T)], axis=0))

    tile = lambda a: jnp.concatenate([a] * T, axis=0)
    bbr, bbi, cr, ci = tile(bb_r), tile(bb_i), tile(ct_ref[0]), tile(ct_ref[1])

    er, ei = powers(lambda s: T - 1 - s, lambda s: s)
    sin_ref[:, 0:LANES] = (er * bbr - ei * bbi).astype(sin_ref.dtype)
    sin_ref[:, LANES:] = (er * bbi + ei * bbr).astype(sin_ref.dtype)

    er, ei = powers(lambda t: t + 1, lambda t: T - t)
    gt = jnp.concatenate([er * cr - ei * ci, -(er * ci + ei * cr)], axis=1)
    g_ref[...] = gt.T.astype(g_ref.dtype)

    er, ei = powers(lambda a: a, lambda a: T - 1 - a)
    cp = jnp.concatenate([er * cr - ei * ci, er * ci + ei * cr], axis=1)
    zero = jnp.zeros_like(bb_r)
    lhs = jnp.concatenate([jnp.concatenate([jnp.where(is_fwd, bb_r, zero), jnp.where(is_fwd, -bb_i, zero)], axis=1),
                           jnp.concatenate([jnp.where(is_fwd, zero, bb_r), jnp.where(is_fwd, zero, -bb_i)], axis=1)],
                          axis=0)
    kr = lax.dot_general(lhs, cp, (((1,), (1,)), ((), ())), preferred_element_type=F32,
                         precision=lax.Precision.HIGHEST)
    pad = jnp.zeros((H, S5_CW), F32)
    wide_f = jnp.concatenate([pad, kr[0:H]], axis=1)
    wide_b = jnp.concatenate([kr[H:], pad], axis=1)
    lane = lax.broadcasted_iota(jnp.int32, (H, S5_CW), 1)
    row = lax.broadcasted_iota(jnp.int32, (H, S5_CW), 0)
    skip = jnp.broadcast_to(d_ref[...], (H, S5_CW))
    for s in range(T):
        blk_f = pltpu.roll(wide_f, H * s, 1)[:, S5_CW:] if s else wide_f[:, S5_CW:]
        shift_b = (2 * S5_CW - H * (T - 1 - s)) % (2 * S5_CW)
        blk_b = (pltpu.roll(wide_b, shift_b, 1) if shift_b else wide_b)[:, :S5_CW]
        diag = jnp.where(lane == H * s + row, skip, 0.0)
        m_ref[H * s:H * (s + 1), :] = (blk_f + blk_b + diag).astype(m_ref.dtype)
    a_ref[0:1, :] = pw_r[T:T + 1]
    a_ref[1:2, :] = pw_i[T:T + 1]


def _s5_ops_call(lam, ldt, bt, ct, dt_tiled):
    groups = lam.shape[0]
    gb = 4
    mat = lambda: pl.BlockSpec((gb, S5_CW, S5_CW), lambda g: (g, 0, 0))
    vec = lambda a: pl.BlockSpec((gb,) + a.shape[1:], lambda g: (g,) + (0,) * (a.ndim - 1))
    mshape = jax.ShapeDtypeStruct((groups, S5_CW, S5_CW), BF16)
    return pl.pallas_call(
        _s5_ops_kernel,
        out_shape=(mshape, mshape, mshape, jax.ShapeDtypeStruct((groups, 2, LANES), F32)),
        grid=(groups // gb,),
        in_specs=[vec(lam), vec(ldt), vec(bt), vec(ct), vec(dt_tiled)],
        out_specs=(mat(), mat(), mat(), pl.BlockSpec((gb, 2, LANES), lambda g: (g, 0, 0))),
        compiler_params=pltpu.CompilerParams(dimension_semantics=("arbitrary",)),
        name="s5_ops",
    )(lam, ldt, bt, ct, dt_tiled)


def _s5_operators(lam_re, lam_im, log_dt, b_re, b_im, c_re, c_im, d_skip):
    G, P, H = S5_GROUPS, S5_STATE, S5_GROUP
    lam = jnp.stack([lam_re, lam_im]).astype(F32).transpose(2, 0, 1, 3).reshape(G, 2, 2 * P)
    ldt = jnp.repeat(log_dt.astype(F32).T, P, axis=1).reshape(G, 1, 2 * P)
    bt = jnp.stack([b_re, b_im]).astype(F32).transpose(2, 0, 4, 1, 3).reshape(G, 2, H, 2 * P)
    ct = jnp.stack([c_re, c_im]).astype(F32).transpose(2, 0, 3, 1, 4).reshape(G, 2, H, 2 * P)
    dt_tiled = jnp.tile(d_skip.astype(F32).reshape(G, 1, H), (1, 1, S5_CHUNK))
    return _s5_ops_call(lam, ldt, bt, ct, dt_tiled)


def _final_kernel(x_ref, mod_ref, ng_ref, wg_ref, ya_ref, yp_ref, permt_ref, wglu_ref, bglu_ref, wa_ref, wb_ref,
                  wo_ref, fg_ref, o_ref):
    bsz, tt, _ = x_ref.shape
    rows = bsz * tt
    x3 = x_ref[...]
    xn = _modulated_norm(x3, mod_ref, ng_ref).reshape(rows, D_MODEL).astype(BF16)
    gates = jnp.dot(xn, wg_ref[...], preferred_element_type=F32)
    o1, o2, o3 = ATTN_WIDTH, ATTN_WIDTH + S5_WIDTH, ATTN_WIDTH + S5_WIDTH + D_MODEL
    prow = PERM_TOK // S5_CHUNK * bsz
    parts = []
    for part in range(tt // PERM_TOK):
        slabs = [jnp.concatenate([yp_ref[part * prow:(part + 1) * prow,
                                         blk * BLOCK_W + t * LANES: blk * BLOCK_W + (t + 1) * LANES]
                                  for blk in range(N_BLOCKS)], axis=1) for t in range(S5_CHUNK)]
        yb = jnp.dot(permt_ref[...], jnp.concatenate(slabs, axis=0), preferred_element_type=F32)
        parts.append(yb.reshape(bsz, PERM_TOK, S5_WIDTH))
    y = (parts[0] if len(parts) == 1 else jnp.concatenate(parts, axis=1)).reshape(rows, S5_WIDTH)
    z = y * (0.5 * (1.0 + jnp.tanh(0.7978845608028654 * (y + 0.044715 * (y * y * y)))))
    zz = z * jax.nn.sigmoid(jnp.dot(z.astype(BF16), wglu_ref[...], preferred_element_type=F32) + bglu_ref[...])
    ya = ya_ref[...].reshape(rows, ATTN_WIDTH).astype(F32)
    ta = (ya * _silu(gates[:, 0:o1])).astype(BF16)
    tb = (zz * _silu(gates[:, o1:o2])).astype(BF16)
    pa = jnp.dot(ta, wa_ref[...], preferred_element_type=F32)
    pb = jnp.dot(tb, wb_ref[...], preferred_element_type=F32)
    mix = jax.nn.sigmoid(gates[:, o2:o3]) * pa + jax.nn.sigmoid(gates[:, o3:]) * pb
    o = jnp.dot(mix.astype(BF16), wo_ref[...], preferred_element_type=F32).reshape(bsz, tt, D_MODEL)
    h = x3 + mod_ref[:, :, 2 * D_MODEL:] * o
    ms = jnp.mean(h * h, axis=-1, keepdims=True)
    o_ref[...] = h * lax.rsqrt(ms + EPS) * fg_ref[...]


def _final_call(x, mod3, ng, wg, ya, yp, permt, wglu, bglu, wa, wb, wo, fg):
    bsz, n, _ = x.shape
    tt = TOK_TILE
    up_rows = tt // S5_CHUNK * bsz
    tok = lambda width: pl.BlockSpec((bsz, tt, width), lambda i: (0, i, 0))
    c = lambda shape: _const_spec(shape, 1)
    return pl.pallas_call(
        _final_kernel,
        out_shape=jax.ShapeDtypeStruct((bsz, n, D_MODEL), F32),
        grid=(n // tt,),
        in_specs=[tok(D_MODEL),
                  pl.BlockSpec((bsz, 1, 3 * D_MODEL), lambda i: (0, 0, 0)),
                  c((1, D_MODEL)), c(wg.shape),
                  tok(ATTN_WIDTH),
                  pl.BlockSpec((up_rows, N_BLOCKS * BLOCK_W), lambda i: (i, 0)),
                  c(permt.shape), c(wglu.shape), c((1, S5_WIDTH)), c(wa.shape), c(wb.shape), c(wo.shape),
                  c((1, D_MODEL))],
        out_specs=tok(D_MODEL),
        compiler_params=pltpu.CompilerParams(dimension_semantics=("arbitrary",), vmem_limit_bytes=VMEM_LIMIT),
        name="final",
    )(x, mod3, ng, wg, ya, yp, permt, wglu, bglu, wa, wb, wo, fg)


def _rope_tables(n):
    rows = n // GRID_W
    row_ids = np.repeat(np.arange(rows, dtype=np.float64), GRID_W)
    col_ids = np.tile(np.arange(GRID_W, dtype=np.float64), rows)
    freqs = ROPE_THETA ** (-np.arange(ROPE_FREQS, dtype=np.float64) / ROPE_FREQS)
    ang_r, ang_c = row_ids[:, None] * freqs, col_ids[:, None] * freqs
    cos = np.concatenate([np.cos(ang_r)] * 2 + [np.cos(ang_c)] * 2, axis=1)
    sin = np.concatenate([-np.sin(ang_r), np.sin(ang_r), -np.sin(ang_c), np.sin(ang_c)], axis=1)
    reps = LANES // HEAD_DIM
    return (jnp.asarray(np.tile(cos, (1, reps)).astype(np.float32)),
            jnp.asarray(np.tile(sin, (1, reps)).astype(np.float32)))


def _one_hot(match):
    return jnp.asarray(np.ascontiguousarray(match).astype(BF16))


def _block_ones(width):
    idx = np.arange(width) // HEAD_DIM
    return _one_hot(idx[:, None] == idx[None, :])


def _row_perm(bsz, tt, transpose=False):
    chunks = tt // S5_CHUNK
    r = np.arange(bsz * tt)
    t, pc, b = r // (chunks * bsz), (r // bsz) % chunks, r % bsz
    src = b * tt + pc * S5_CHUNK + t
    match = src[:, None] == np.arange(bsz * tt)[None, :]
    return _one_hot(match.T if transpose else match)


def _lane_perm(transpose=False):
    r = np.arange(HALF_W)
    t, j, h = r // LANES, (r % LANES) // S5_GROUP, r % S5_GROUP
    dst = j * LANES + t * S5_GROUP + h
    match = dst[:, None] == np.arange(HALF_W)[None, :]
    return _one_hot(match.T if transpose else match)


def _kv_expand():
    col = np.arange(N_KV_HEADS * REP_W)
    src = (col // REP_W) * HEAD_DIM + col % HEAD_DIM
    return _one_hot(np.arange(KV_WIDTH)[:, None] == src[None, :])


def kernel(x, c, ctx, c_ctx, norm_g, w_ada, b_ada, w_in, q_norm_g, k_norm_g, s5_lam_re, s5_lam_im, s5_log_dt,
           s5_b_re, s5_b_im, s5_c_re, s5_c_im, s5_d, w_glu, b_glu, w_branch_attn, w_branch_s5, w_out,
           final_norm_g):
    assert w_in.shape[0] == 1, "single-layer block"
    bsz, n, _ = x.shape
    n_ctx = ctx.shape[1]
    assert n % TOK_TILE == 0 and n_ctx % TOK_TILE == 0 and bsz == SUBLANES

    ada_rows = 2 * SUBLANES
    cc = jnp.concatenate([c, c_ctx[None], jnp.zeros((ada_rows - bsz - 1, D_MODEL), F32)], axis=0)
    mod = _ada_call(cc, w_ada[0], b_ada[0][None])
    mod3 = mod[:bsz].reshape(bsz, 1, 3 * D_MODEL)
    mod_ctx3 = mod[bsz:bsz + 1].reshape(1, 1, 3 * D_MODEL)

    offs = [0]
    for s in IN_SIZES:
        offs.append(offs[-1] + s)
    w_bf = w_in[0].astype(BF16)
    w_pre = jnp.concatenate([w_bf[:, offs[0]:offs[3]], w_bf[:, offs[4]:offs[5]]], axis=1)
    w_gates = jnp.concatenate([w_bf[:, offs[3]:offs[4]], w_bf[:, offs[5]:]], axis=1)

    ng = norm_g[0][None]
    qg = jnp.tile(q_norm_g[0], N_HEADS)[None]
    kg = jnp.tile(k_norm_g[0], N_KV_HEADS)[None]
    onesq, onesk = _block_ones(ATTN_WIDTH), _block_ones(KV_WIDTH)
    cos, sin = _rope_tables(n)
    row_perm = _row_perm(bsz, PERM_TOK)
    lane_perm = _lane_perm()

    q, k_all, v_all, up = _pre_call(x, ctx, mod3, mod_ctx3, ng, w_pre, onesq, onesk, qg, kg, cos, sin, row_perm)

    y_attn = _attn_call(q, k_all, v_all, _kv_expand(), tq=1024)

    ug = _perm_in_call(up, lane_perm, tr=384)
    m, s_in, gmat, a16 = _s5_operators(s5_lam_re[0], s5_lam_im[0], s5_log_dt[0], s5_b_re[0], s5_b_im[0],
                                       s5_c_re[0], s5_c_im[0], s5_d[0])
    yg = _s5_call(ug, m, s_in, gmat, a16, n_ctx // S5_CHUNK, gb=GROUPS_PER_BLOCK)
    yp = _perm_out_call(yg, _lane_perm(transpose=True), tr=256)

    return _final_call(x, mod3, ng, w_gates, y_attn, yp, _row_perm(bsz, PERM_TOK, transpose=True),
                       w_glu[0].astype(BF16), b_glu[0][None],
                       w_branch_attn[0].astype(BF16), w_branch_s5[0].astype(BF16), w_out[0].astype(BF16),
                       final_norm_g[None])
```

```python
import functools

import numpy as np
import jax
import jax.numpy as jnp
from jax import lax
from jax.experimental import pallas as pl
from jax.experimental.pallas import tpu as pltpu

D_MODEL = 1024
GRID_W = 64
N_HEADS = 8
N_KV_HEADS = 2
HEAD_DIM = 64
GQA_REP = N_HEADS // N_KV_HEADS
ATTN_WIDTH = N_HEADS * HEAD_DIM
KV_WIDTH = N_KV_HEADS * HEAD_DIM
ATTN_SCALE = HEAD_DIM ** -0.5
ROPE_THETA = 10000.0
ROPE_FREQS = HEAD_DIM // 4
S5_WIDTH = 512
S5_GROUP = 16
S5_GROUPS = S5_WIDTH // S5_GROUP
S5_STATE = 64
EPS = 1e-6
IN_SIZES = (ATTN_WIDTH, KV_WIDTH, KV_WIDTH, ATTN_WIDTH, S5_WIDTH, S5_WIDTH, D_MODEL, D_MODEL)

LANES = 128
SUBLANES = 8
S5_CHUNK = 16
S5_CW = S5_CHUNK * S5_GROUP
GROUPS_PER_BLOCK = LANES // S5_GROUP
N_BLOCKS = S5_WIDTH // LANES
BLOCK_W = S5_CHUNK * LANES
HALF_W = BLOCK_W // 2
TOK_TILE = 128
PERM_TOK = 64
REP_W = GQA_REP * HEAD_DIM
VMEM_LIMIT = 56 * 1024 * 1024

F32 = jnp.float32
BF16 = jnp.bfloat16


def _silu(t):
    return t * jax.nn.sigmoid(t)


def _modulated_norm(x3, mod_ref, ng_ref):
    ms = jnp.mean(x3 * x3, axis=-1, keepdims=True)
    y = x3 * lax.rsqrt(ms + EPS) * ng_ref[...]
    return y * (1.0 + mod_ref[:, :, D_MODEL:2 * D_MODEL]) + mod_ref[:, :, 0:D_MODEL]


def _head_rmsnorm(t, ones_ref, g_ref):
    ss = jnp.dot((t * t).astype(BF16), ones_ref[...], preferred_element_type=F32)
    return t * lax.rsqrt(ss * (1.0 / HEAD_DIM) + EPS) * g_ref[...]


def _rope(t, cos, sin_signed):
    rows = t.shape[0]
    lane = lax.broadcasted_iota(jnp.int32, (rows, LANES), 1)
    first = (lane & ROPE_FREQS) == 0
    outs = []
    for j in range(t.shape[1] // LANES):
        blk = t[:, j * LANES:(j + 1) * LANES]
        partner = jnp.where(first, pltpu.roll(blk, LANES - ROPE_FREQS, 1), pltpu.roll(blk, ROPE_FREQS, 1))
        outs.append(blk * cos + partner * sin_signed)
    return outs[0] if len(outs) == 1 else jnp.concatenate(outs, axis=1)


def _const_spec(shape, grid_rank):
    zeros = (0,) * len(shape)
    return pl.BlockSpec(shape, lambda *_: zeros, pipeline_mode=pl.Buffered(1))


def _ada_kernel(c_ref, w_ref, b_ref, o_ref):
    s = _silu(c_ref[...])
    w = w_ref[...]
    s_hi, w_hi = s.astype(BF16), w.astype(BF16)
    s_lo = (s - s_hi.astype(F32)).astype(BF16)
    w_lo = (w - w_hi.astype(F32)).astype(BF16)
    dot = functools.partial(jnp.dot, preferred_element_type=F32)
    o_ref[...] = dot(s_hi, w_hi) + dot(s_lo, w_hi) + dot(s_hi, w_lo) + b_ref[...]


def _ada_call(cc, w, b):
    rows, n = cc.shape[0], w.shape[1]
    tn = 512
    return pl.pallas_call(
        _ada_kernel,
        out_shape=jax.ShapeDtypeStruct((rows, n), F32),
        grid=(n // tn,),
        in_specs=[pl.BlockSpec((rows, D_MODEL), lambda j: (0, 0)),
                  pl.BlockSpec((D_MODEL, tn), lambda j: (0, j)),
                  pl.BlockSpec((1, tn), lambda j: (0, j))],
        out_specs=pl.BlockSpec((rows, tn), lambda j: (0, j)),
        compiler_params=pltpu.CompilerParams(dimension_semantics=("arbitrary",)),
        name="ada",
    )(cc, w, b)


def _store_chunk_major(u, perm_ref, up_out, bsz):
    tt = u.shape[0] // bsz
    u3 = u.astype(BF16).reshape(bsz, tt, S5_WIDTH)
    rows = PERM_TOK // S5_CHUNK * bsz
    for part in range(tt // PERM_TOK):
        up = u3[:, part * PERM_TOK:(part + 1) * PERM_TOK, :].reshape(bsz * PERM_TOK, S5_WIDTH)
        r = jnp.dot(perm_ref[...], up, preferred_element_type=F32).astype(BF16)
        for t in range(S5_CHUNK):
            for blk in range(N_BLOCKS):
                up_out[part * rows:(part + 1) * rows, blk * BLOCK_W + t * LANES: blk * BLOCK_W + (t + 1) * LANES] = (
                    r[t * rows:(t + 1) * rows, blk * LANES:(blk + 1) * LANES])


def _pre_kernel(x_ref, c_ref, mod_ref, modc_ref, ng_ref, w_ref, onesq_ref, onesk_ref, qg_ref, kg_ref, cos_ref, sin_ref,
                perm_ref, q_out, k_out, v_out, up_out, *, ctx_steps):
    bsz, tt, _ = x_ref.shape
    o1, o2, o3 = ATTN_WIDTH, ATTN_WIDTH + KV_WIDTH, ATTN_WIDTH + 2 * KV_WIDTH
    step = pl.program_id(0)

    @pl.when(step < ctx_steps)
    def _():
        xn = _modulated_norm(c_ref[...], modc_ref, ng_ref).reshape(bsz * tt, D_MODEL).astype(BF16)
        p = jnp.dot(xn, w_ref[:, o1:], preferred_element_type=F32)
        k = _head_rmsnorm(p[:, 0:KV_WIDTH], onesk_ref, kg_ref)
        k_out[...] = k.astype(BF16).reshape(bsz, tt, KV_WIDTH)
        v_out[...] = p[:, KV_WIDTH:2 * KV_WIDTH].astype(BF16).reshape(bsz, tt, KV_WIDTH)
        _store_chunk_major(p[:, 2 * KV_WIDTH:], perm_ref, up_out, bsz)

    @pl.when(step >= ctx_steps)
    def _():
        xn = _modulated_norm(x_ref[...], mod_ref, ng_ref).reshape(bsz * tt, D_MODEL).astype(BF16)
        p = jnp.dot(xn, w_ref[...], preferred_element_type=F32)
        cos = jnp.concatenate([cos_ref[...]] * bsz, axis=0)
        sin = jnp.concatenate([sin_ref[...]] * bsz, axis=0)
        q = _rope(_head_rmsnorm(p[:, 0:o1], onesq_ref, qg_ref), cos, sin)
        k = _rope(_head_rmsnorm(p[:, o1:o2], onesk_ref, kg_ref), cos, sin)
        q_out[...] = (q * ATTN_SCALE).astype(BF16).reshape(bsz, tt, ATTN_WIDTH)
        k_out[...] = k.astype(BF16).reshape(bsz, tt, KV_WIDTH)
        v_out[...] = p[:, o2:o3].astype(BF16).reshape(bsz, tt, KV_WIDTH)
        _store_chunk_major(p[:, o3:], perm_ref, up_out, bsz)


def _pre_call(x, ctx, mod3, mod_ctx3, ng, w, onesq, onesk, qg, kg, cos, sin, perm):
    bsz, n, _ = x.shape
    n_ctx = ctx.shape[1]
    tt = TOK_TILE
    up_rows = tt // S5_CHUNK * bsz
    ctx_steps = n_ctx // tt
    lat = lambda i: jnp.maximum(i - ctx_steps, 0)
    c = lambda shape: _const_spec(shape, 1)
    return pl.pallas_call(
        functools.partial(_pre_kernel, ctx_steps=ctx_steps),
        out_shape=(jax.ShapeDtypeStruct((bsz, n, ATTN_WIDTH), BF16),
                   jax.ShapeDtypeStruct((bsz, n_ctx + n, KV_WIDTH), BF16),
                   jax.ShapeDtypeStruct((bsz, n_ctx + n, KV_WIDTH), BF16),
                   jax.ShapeDtypeStruct(((n_ctx + n) // S5_CHUNK * bsz, N_BLOCKS * BLOCK_W), BF16)),
        grid=(ctx_steps + n // tt,),
        in_specs=[pl.BlockSpec((bsz, tt, D_MODEL), lambda i: (0, lat(i), 0)),
                  pl.BlockSpec((bsz, tt, D_MODEL), lambda i: (0, jnp.minimum(i, ctx_steps - 1), 0)),
                  c((bsz, 1, 3 * D_MODEL)), c((1, 1, 3 * D_MODEL)),
                  c((1, D_MODEL)), c(w.shape), c(onesq.shape), c(onesk.shape),
                  c((1, ATTN_WIDTH)), c((1, KV_WIDTH)),
                  pl.BlockSpec((tt, LANES), lambda i: (lat(i), 0)),
                  pl.BlockSpec((tt, LANES), lambda i: (lat(i), 0)),
                  c(perm.shape)],
        out_specs=(pl.BlockSpec((bsz, tt, ATTN_WIDTH), lambda i: (0, lat(i), 0)),
                   pl.BlockSpec((bsz, tt, KV_WIDTH), lambda i: (0, i, 0)),
                   pl.BlockSpec((bsz, tt, KV_WIDTH), lambda i: (0, i, 0)),
                   pl.BlockSpec((up_rows, N_BLOCKS * BLOCK_W), lambda i: (i, 0))),
        compiler_params=pltpu.CompilerParams(dimension_semantics=("arbitrary",), vmem_limit_bytes=VMEM_LIMIT),
        name="pre",
    )(x, ctx, mod3, mod_ctx3, ng, w, onesq, onesk, qg, kg, cos, sin, perm)


ATTN_ROWS = 256


def _attn_kernel(q_ref, k_ref, v_ref, e_ref, o_ref, k4t_ref, v4_ref):
    @pl.when(pl.program_id(1) == 0)
    def _():
        k4 = jnp.dot(k_ref[...], e_ref[...], preferred_element_type=F32)
        k4t = k4.T
        v4 = jnp.dot(v_ref[...], e_ref[...], preferred_element_type=F32)
        for g in range(N_KV_HEADS):
            k4t_ref[g] = k4t[g * REP_W:(g + 1) * REP_W].astype(BF16)
            v4_ref[g] = v4[:, g * REP_W:(g + 1) * REP_W].astype(BF16)

    lane = lax.broadcasted_iota(jnp.int32, (ATTN_ROWS, REP_W), 1)
    for part in range(q_ref.shape[0] // ATTN_ROWS):
        rows = slice(part * ATTN_ROWS, (part + 1) * ATTN_ROWS)
        for g in range(N_KV_HEADS):
            qg = q_ref[rows, g * REP_W:(g + 1) * REP_W]
            acc = jnp.zeros((ATTN_ROWS, REP_W), F32)
            for r in range(GQA_REP):
                in_head = (lane >= r * HEAD_DIM) & (lane < (r + 1) * HEAD_DIM)
                qr = jnp.where(in_head, qg, jnp.zeros_like(qg))
                s = jnp.dot(qr, k4t_ref[g], preferred_element_type=F32)
                m = jnp.max(s, axis=1, keepdims=True)
                p = jnp.exp(s - m)
                l = jnp.sum(p, axis=1, keepdims=True)
                o = jnp.dot(p.astype(BF16), v4_ref[g], preferred_element_type=F32)
                acc = jnp.where(in_head, o / l, acc)
            o_ref[rows, g * REP_W:(g + 1) * REP_W] = acc.astype(o_ref.dtype)


def _attn_call(q, k_all, v_all, expand, tq):
    bsz, n, _ = q.shape
    nk = k_all.shape[1]
    return pl.pallas_call(
        _attn_kernel,
        out_shape=jax.ShapeDtypeStruct((bsz, n, ATTN_WIDTH), BF16),
        grid=(bsz, n // tq),
        in_specs=[pl.BlockSpec((None, tq, ATTN_WIDTH), lambda b, i: (b, i, 0)),
                  pl.BlockSpec((None, nk, KV_WIDTH), lambda b, i: (b, 0, 0)),
                  pl.BlockSpec((None, nk, KV_WIDTH), lambda b, i: (b, 0, 0)),
                  _const_spec(expand.shape, 2)],
        out_specs=pl.BlockSpec((None, tq, ATTN_WIDTH), lambda b, i: (b, i, 0)),
        scratch_shapes=[pltpu.VMEM((N_KV_HEADS, REP_W, nk), BF16),
                        pltpu.VMEM((N_KV_HEADS, nk, REP_W), BF16)],
        compiler_params=pltpu.CompilerParams(dimension_semantics=("arbitrary", "arbitrary"),
                                             vmem_limit_bytes=VMEM_LIMIT),
        name="attn",
    )(q, k_all, v_all, expand)


def _perm_in_kernel(s_ref, p_ref, o_ref):
    halves = [jnp.dot(s_ref[:, hf * HALF_W:(hf + 1) * HALF_W], p_ref[...], preferred_element_type=F32)
              for hf in range(BLOCK_W // HALF_W)]
    for j in range(GROUPS_PER_BLOCK):
        o_ref[j] = jnp.concatenate([r[:, j * LANES:(j + 1) * LANES] for r in halves], axis=1).astype(o_ref.dtype)


def _perm_in_call(up, pmat, tr):
    rows = up.shape[0]
    return pl.pallas_call(
        _perm_in_kernel,
        out_shape=jax.ShapeDtypeStruct((S5_GROUPS, rows, S5_CW), BF16),
        grid=(N_BLOCKS, rows // tr),
        in_specs=[pl.BlockSpec((tr, BLOCK_W), lambda s, i: (i, s)),
                  _const_spec(pmat.shape, 2)],
        out_specs=pl.BlockSpec((GROUPS_PER_BLOCK, tr, S5_CW), lambda s, i: (s, i, 0)),
        compiler_params=pltpu.CompilerParams(dimension_semantics=("arbitrary", "arbitrary"),
                                             vmem_limit_bytes=VMEM_LIMIT),
        name="perm_in",
    )(up, pmat)


def _perm_out_kernel(y_ref, q_ref, o_ref):
    for hf in range(BLOCK_W // HALF_W):
        ycat = jnp.concatenate([y_ref[j, :, hf * LANES:(hf + 1) * LANES] for j in range(GROUPS_PER_BLOCK)], axis=1)
        o_ref[:, hf * HALF_W:(hf + 1) * HALF_W] = jnp.dot(ycat, q_ref[...],
                                                         preferred_element_type=F32).astype(o_ref.dtype)


def _perm_out_call(yg, qmat, tr):
    rows = yg.shape[1]
    return pl.pallas_call(
        _perm_out_kernel,
        out_shape=jax.ShapeDtypeStruct((rows, N_BLOCKS * BLOCK_W), BF16),
        grid=(N_BLOCKS, rows // tr),
        in_specs=[pl.BlockSpec((GROUPS_PER_BLOCK, tr, S5_CW), lambda s, i: (s, i, 0)),
                  _const_spec(qmat.shape, 2)],
        out_specs=pl.BlockSpec((tr, BLOCK_W), lambda s, i: (i, s)),
        compiler_params=pltpu.CompilerParams(dimension_semantics=("arbitrary", "arbitrary"),
                                             vmem_limit_bytes=VMEM_LIMIT),
        name="perm_out",
    )(yg, qmat)


def _s5_kernel(u_ref, m_ref, sin_ref, g_ref, a_ref, y_ref, inc_ref, hin_ref, *, n_ctx_chunks, n_chunks):
    half = S5_STATE
    gb = u_ref.shape[0]
    for j in range(gb):
        inc_ref[j] = jnp.dot(u_ref[j], sin_ref[j], preferred_element_type=F32)
    a_re = [jnp.broadcast_to(a_ref[j, 0:1, :], (SUBLANES, LANES)) for j in range(gb)]
    a_im = [jnp.broadcast_to(a_ref[j, 1:2, :], (SUBLANES, LANES)) for j in range(gb)]
    is_fwd = lax.broadcasted_iota(jnp.int32, (SUBLANES, LANES), 1) < half

    def step(k, carry):
        pos_b = jnp.where(k < n_ctx_chunks, n_ctx_chunks - 1 - k, n_chunks + n_ctx_chunks - 1 - k)
        rf = pl.multiple_of(k * SUBLANES, SUBLANES)
        rb = pl.multiple_of(pos_b * SUBLANES, SUBLANES)
        out = []
        for j in range(gb):
            h_re, h_im = carry[2 * j], carry[2 * j + 1]
            xf = inc_ref[j, pl.ds(rf, SUBLANES), :]
            xb = inc_ref[j, pl.ds(rb, SUBLANES), :]
            hin_ref[j, pl.ds(rf, SUBLANES), 0:half] = h_re[:, 0:half]
            hin_ref[j, pl.ds(rf, SUBLANES), 2 * half:3 * half] = h_im[:, 0:half]
            hin_ref[j, pl.ds(rb, SUBLANES), half:2 * half] = h_re[:, half:]
            hin_ref[j, pl.ds(rb, SUBLANES), 3 * half:] = h_im[:, half:]
            x_re = jnp.where(is_fwd, xf[:, 0:LANES], xb[:, 0:LANES])
            x_im = jnp.where(is_fwd, xf[:, LANES:], xb[:, LANES:])
            out.append(a_re[j] * h_re - a_im[j] * h_im + x_re)
            out.append(a_re[j] * h_im + a_im[j] * h_re + x_im)
        return tuple(out)

    zero = jnp.zeros((SUBLANES, LANES), F32)
    lax.fori_loop(0, n_chunks, step, (zero,) * (2 * gb))

    r0 = n_ctx_chunks * SUBLANES
    for j in range(gb):
        y = jnp.dot(u_ref[j, r0:, :], m_ref[j], preferred_element_type=F32)
        y = y + jnp.dot(hin_ref[j, r0:, :].astype(BF16), g_ref[j], preferred_element_type=F32)
        y_ref[j] = y.astype(y_ref.dtype)


def _s5_call(ug, m, sin, gmat, a16, n_ctx_chunks, gb):
    groups, rows, _ = ug.shape
    n_chunks = rows // SUBLANES
    out_rows = rows - n_ctx_chunks * SUBLANES
    mat = lambda: pl.BlockSpec((gb, S5_CW, S5_CW), lambda g: (g, 0, 0))
    return pl.pallas_call(
        functools.partial(_s5_kernel, n_ctx_chunks=n_ctx_chunks, n_chunks=n_chunks),
        out_shape=jax.ShapeDtypeStruct((groups, out_rows, S5_CW), BF16),
        grid=(groups // gb,),
        in_specs=[pl.BlockSpec((gb, rows, S5_CW), lambda g: (g, 0, 0)),
                  mat(), mat(), mat(),
                  pl.BlockSpec((gb, 2, LANES), lambda g: (g, 0, 0))],
        out_specs=pl.BlockSpec((gb, out_rows, S5_CW), lambda g: (g, 0, 0)),
        scratch_shapes=[pltpu.VMEM((gb, rows, S5_CW), F32), pltpu.VMEM((gb, rows, S5_CW), F32)],
        compiler_params=pltpu.CompilerParams(dimension_semantics=("arbitrary",),
                                             vmem_limit_bytes=VMEM_LIMIT),
        name="s5",
    )(ug, m, sin, gmat, a16)


def _s5_ops_kernel(*refs):
    for j in range(refs[0].shape[0]):
        _s5_ops_group(*(r.at[j] for r in refs))


def _s5_ops_group(lam_ref, ldt_ref, bt_ref, ct_ref, d_ref, m_ref, sin_ref, g_ref, a_ref):
    T, H = S5_CHUNK, S5_GROUP
    lr = jnp.minimum(lam_ref[0:1, :], -1e-4)
    li = lam_ref[1:2, :]
    dt = jnp.exp(ldt_ref[...])
    taus = lax.broadcasted_iota(jnp.int32, (3 * SUBLANES, LANES), 0).astype(F32)
    mag = jnp.exp(lr * dt * taus)
    pw_r = mag * jnp.cos(li * dt * taus)
    pw_i = mag * jnp.sin(li * dt * taus)
    nr, ni = pw_r[1:2] - 1.0, pw_i[1:2]
    den = lr * lr + li * li
    cf_r = (nr * lr + ni * li) / den
    cf_i = (ni * lr - nr * li) / den
    bb_r = cf_r * bt_ref[0] - cf_i * bt_ref[1]
    bb_i = cf_r * bt_ref[1] + cf_i * bt_ref[0]
    is_fwd = lax.broadcasted_iota(jnp.int32, (H, LANES), 1) < S5_STATE

    def powers(tau_f, tau_b):
        pick = lambda pw, s: jnp.where(is_fwd, jnp.broadcast_to(pw[tau_f(s):tau_f(s) + 1], (H, LANES)),
                                       jnp.broadcast_to(pw[tau_b(s):tau_b(s) + 1], (H, LANES)))
        return (jnp.concatenate([pick(pw_r, s) for s in range(T)], axis=0),
                jnp.concatenate([pick(pw_i, s) for s in range(T)], axis=0))

    tile = lambda a: jnp.concatenate([a] * T, axis=0)
    bbr, bbi, cr, ci = tile(bb_r), tile(bb_i), tile(ct_ref[0]), tile(ct_ref[1])

    er, ei = powers(lambda s: T - 1 - s, lambda s: s)
    sin_ref[:, 0:LANES] = (er * bbr - ei * bbi).astype(sin_ref.dtype)
    sin_ref[:, LANES:] = (er * bbi + ei * bbr).astype(sin_ref.dtype)

    er, ei = powers(lambda t: t + 1, lambda t: T - t)
    gt = jnp.concatenate([er * cr - ei * ci, -(er * ci + ei * cr)], axis=1)
    g_ref[...] = gt.T.astype(g_ref.dtype)

    er, ei = powers(lambda a: a, lambda a: T - 1 - a)
    cp = jnp.concatenate([er * cr - ei * ci, er * ci + ei * cr], axis=1)
    zero = jnp.zeros_like(bb_r)
    lhs = jnp.concatenate([jnp.concatenate([jnp.where(is_fwd, bb_r, zero), jnp.where(is_fwd, -bb_i, zero)], axis=1),
                           jnp.concatenate([jnp.where(is_fwd, zero, bb_r), jnp.where(is_fwd, zero, -bb_i)], axis=1)],
                          axis=0)
    kr = lax.dot_general(lhs, cp, (((1,), (1,)), ((), ())), preferred_element_type=F32,
                         precision=lax.Precision.HIGHEST)
    pad = jnp.zeros((H, S5_CW), F32)
    wide_f = jnp.concatenate([pad, kr[0:H]], axis=1)
    wide_b = jnp.concatenate([kr[H:], pad], axis=1)
    lane = lax.broadcasted_iota(jnp.int32, (H, S5_CW), 1)
    row = lax.broadcasted_iota(jnp.int32, (H, S5_CW), 0)
    skip = jnp.broadcast_to(d_ref[...], (H, S5_CW))
    for s in range(T):
        blk_f = pltpu.roll(wide_f, H * s, 1)[:, S5_CW:] if s else wide_f[:, S5_CW:]
        shift_b = (2 * S5_CW - H * (T - 1 - s)) % (2 * S5_CW)
        blk_b = (pltpu.roll(wide_b, shift_b, 1) if shift_b else wide_b)[:, :S5_CW]
        diag = jnp.where(lane == H * s + row, skip, 0.0)
        m_ref[H * s:H * (s + 1), :] = (blk_f + blk_b + diag).astype(m_ref.dtype)
    a_ref[0:1, :] = pw_r[T:T + 1]
    a_ref[1:2, :] = pw_i[T:T + 1]


def _s5_ops_call(lam, ldt, bt, ct, dt_tiled):
    groups = lam.shape[0]
    gb = 4
    mat = lambda: pl.BlockSpec((gb, S5_CW, S5_CW), lambda g: (g, 0, 0))
    vec = lambda a: pl.BlockSpec((gb,) + a.shape[1:], lambda g: (g,) + (0,) * (a.ndim - 1))
    mshape = jax.ShapeDtypeStruct((groups, S5_CW, S5_CW), BF16)
    return pl.pallas_call(
        _s5_ops_kernel,
        out_shape=(mshape, mshape, mshape, jax.ShapeDtypeStruct((groups, 2, LANES), F32)),
        grid=(groups // gb,),
        in_specs=[vec(lam), vec(ldt), vec(bt), vec(ct), vec(dt_tiled)],
        out_specs=(mat(), mat(), mat(), pl.BlockSpec((gb, 2, LANES), lambda g: (g, 0, 0))),
        compiler_params=pltpu.CompilerParams(dimension_semantics=("arbitrary",)),
        name="s5_ops",
    )(lam, ldt, bt, ct, dt_tiled)


def _s5_operators(lam_re, lam_im, log_dt, b_re, b_im, c_re, c_im, d_skip):
    G, P, H = S5_GROUPS, S5_STATE, S5_GROUP
    lam = jnp.stack([lam_re, lam_im]).astype(F32).transpose(2, 0, 1, 3).reshape(G, 2, 2 * P)
    ldt = jnp.repeat(log_dt.astype(F32).T, P, axis=1).reshape(G, 1, 2 * P)
    bt = jnp.stack([b_re, b_im]).astype(F32).transpose(2, 0, 4, 1, 3).reshape(G, 2, H, 2 * P)
    ct = jnp.stack([c_re, c_im]).astype(F32).transpose(2, 0, 3, 1, 4).reshape(G, 2, H, 2 * P)
    dt_tiled = jnp.tile(d_skip.astype(F32).reshape(G, 1, H), (1, 1, S5_CHUNK))
    return _s5_ops_call(lam, ldt, bt, ct, dt_tiled)


def _final_kernel(x_ref, mod_ref, ng_ref, wg_ref, ya_ref, yp_ref, permt_ref, wglu_ref, bglu_ref, wa_ref, wb_ref,
                  wo_ref, fg_ref, o_ref):
    bsz, tt, _ = x_ref.shape
    rows = bsz * tt
    x3 = x_ref[...]
    xn = _modulated_norm(x3, mod_ref, ng_ref).reshape(rows, D_MODEL).astype(BF16)
    gates = jnp.dot(xn, wg_ref[...], preferred_element_type=F32)
    o1, o2, o3 = ATTN_WIDTH, ATTN_WIDTH + S5_WIDTH, ATTN_WIDTH + S5_WIDTH + D_MODEL
    prow = PERM_TOK // S5_CHUNK * bsz
    parts = []
    for part in range(tt // PERM_TOK):
        slabs = [jnp.concatenate([yp_ref[part * prow:(part + 1) * prow,
                                         blk * BLOCK_W + t * LANES: blk * BLOCK_W + (t + 1) * LANES]
                                  for blk in range(N_BLOCKS)], axis=1) for t in range(S5_CHUNK)]
        yb = jnp.dot(permt_ref[...], jnp.concatenate(slabs, axis=0), preferred_element_type=F32)
        parts.append(yb.reshape(bsz, PERM_TOK, S5_WIDTH))
    y = (parts[0] if len(parts) == 1 else jnp.concatenate(parts, axis=1)).reshape(rows, S5_WIDTH)
    z = y * (0.5 * (1.0 + jnp.tanh(0.7978845608028654 * (y + 0.044715 * (y * y * y)))))
    zz = z * jax.nn.sigmoid(jnp.dot(z.astype(BF16), wglu_ref[...], preferred_element_type=F32) + bglu_ref[...])
    ya = ya_ref[...].reshape(rows, ATTN_WIDTH).astype(F32)
    ta = (ya * _silu(gates[:, 0:o1])).astype(BF16)
    tb = (zz * _silu(gates[:, o1:o2])).astype(BF16)
    pa = jnp.dot(ta, wa_ref[...], preferred_element_type=F32)
    pb = jnp.dot(tb, wb_ref[...], preferred_element_type=F32)
    mix = jax.nn.sigmoid(gates[:, o2:o3]) * pa + jax.nn.sigmoid(gates[:, o3:]) * pb
    o = jnp.dot(mix.astype(BF16), wo_ref[...], preferred_element_type=F32).reshape(bsz, tt, D_MODEL)
    h = x3 + mod_ref[:, :, 2 * D_MODEL:] * o
    ms = jnp.mean(h * h, axis=-1, keepdims=True)
    o_ref[...] = h * lax.rsqrt(ms + EPS) * fg_ref[...]


def _final_call(x, mod3, ng, wg, ya, yp, permt, wglu, bglu, wa, wb, wo, fg):
    bsz, n, _ = x.shape
    tt = TOK_TILE
    up_rows = tt // S5_CHUNK * bsz
    tok = lambda width: pl.BlockSpec((bsz, tt, width), lambda i: (0, i, 0))
    c = lambda shape: _const_spec(shape, 1)
    return pl.pallas_call(
        _final_kernel,
        out_shape=jax.ShapeDtypeStruct((bsz, n, D_MODEL), F32),
        grid=(n // tt,),
        in_specs=[tok(D_MODEL),
                  pl.BlockSpec((bsz, 1, 3 * D_MODEL), lambda i: (0, 0, 0)),
                  c((1, D_MODEL)), c(wg.shape),
                  tok(ATTN_WIDTH),
                  pl.BlockSpec((up_rows, N_BLOCKS * BLOCK_W), lambda i: (i, 0)),
                  c(permt.shape), c(wglu.shape), c((1, S5_WIDTH)), c(wa.shape), c(wb.shape), c(wo.shape),
                  c((1, D_MODEL))],
        out_specs=tok(D_MODEL),
        compiler_params=pltpu.CompilerParams(dimension_semantics=("arbitrary",), vmem_limit_bytes=VMEM_LIMIT),
        name="final",
    )(x, mod3, ng, wg, ya, yp, permt, wglu, bglu, wa, wb, wo, fg)


def _rope_tables(n):
    rows = n // GRID_W
    row_ids = np.repeat(np.arange(rows, dtype=np.float64), GRID_W)
    col_ids = np.tile(np.arange(GRID_W, dtype=np.float64), rows)
    freqs = ROPE_THETA ** (-np.arange(ROPE_FREQS, dtype=np.float64) / ROPE_FREQS)
    ang_r, ang_c = row_ids[:, None] * freqs, col_ids[:, None] * freqs
    cos = np.concatenate([np.cos(ang_r)] * 2 + [np.cos(ang_c)] * 2, axis=1)
    sin = np.concatenate([-np.sin(ang_r), np.sin(ang_r), -np.sin(ang_c), np.sin(ang_c)], axis=1)
    reps = LANES // HEAD_DIM
    return (jnp.asarray(np.tile(cos, (1, reps)).astype(np.float32)),
            jnp.asarray(np.tile(sin, (1, reps)).astype(np.float32)))


def _one_hot(match):
    return jnp.asarray(np.ascontiguousarray(match).astype(BF16))


def _block_ones(width):
    idx = np.arange(width) // HEAD_DIM
    return _one_hot(idx[:, None] == idx[None, :])


def _row_perm(bsz, tt, transpose=False):
    chunks = tt // S5_CHUNK
    r = np.arange(bsz * tt)
    t, pc, b = r // (chunks * bsz), (r // bsz) % chunks, r % bsz
    src = b * tt + pc * S5_CHUNK + t
    match = src[:, None] == np.arange(bsz * tt)[None, :]
    return _one_hot(match.T if transpose else match)


def _lane_perm(transpose=False):
    r = np.arange(HALF_W)
    t, j, h = r // LANES, (r % LANES) // S5_GROUP, r % S5_GROUP
    dst = j * LANES + t * S5_GROUP + h
    match = dst[:, None] == np.arange(HALF_W)[None, :]
    return _one_hot(match.T if transpose else match)


def _kv_expand():
    col = np.arange(N_KV_HEADS * REP_W)
    src = (col // REP_W) * HEAD_DIM + col % HEAD_DIM
    return _one_hot(np.arange(KV_WIDTH)[:, None] == src[None, :])


def kernel(x, c, ctx, c_ctx, norm_g, w_ada, b_ada, w_in, q_norm_g, k_norm_g, s5_lam_re, s5_lam_im, s5_log_dt,
           s5_b_re, s5_b_im, s5_c_re, s5_c_im, s5_d, w_glu, b_glu, w_branch_attn, w_branch_s5, w_out,
           final_norm_g):
    assert w_in.shape[0] == 1, "single-layer block"
    bsz, n, _ = x.shape
    n_ctx = ctx.shape[1]
    assert n % TOK_TILE == 0 and n_ctx % TOK_TILE == 0 and bsz == SUBLANES

    ada_rows = 2 * SUBLANES
    cc = jnp.concatenate([c, c_ctx[None], jnp.zeros((ada_rows - bsz - 1, D_MODEL), F32)], axis=0)
    mod = _ada_call(cc, w_ada[0], b_ada[0][None])
    mod3 = mod[:bsz].reshape(bsz, 1, 3 * D_MODEL)
    mod_ctx3 = mod[bsz:bsz + 1].reshape(1, 1, 3 * D_MODEL)

    offs = [0]
    for s in IN_SIZES:
        offs.append(offs[-1] + s)
    w_bf = w_in[0].astype(BF16)
    w_pre = jnp.concatenate([w_bf[:, offs[0]:offs[3]], w_bf[:, offs[4]:offs[5]]], axis=1)
    w_gates = jnp.concatenate([w_bf[:, offs[3]:offs[4]], w_bf[:, offs[5]:]], axis=1)

    ng = norm_g[0][None]
    qg = jnp.tile(q_norm_g[0], N_HEADS)[None]
    kg = jnp.tile(k_norm_g[0], N_KV_HEADS)[None]
    onesq, onesk = _block_ones(ATTN_WIDTH), _block_ones(KV_WIDTH)
    cos, sin = _rope_tables(n)
    row_perm = _row_perm(bsz, PERM_TOK)
    lane_perm = _lane_perm()

    q, k_all, v_all, up = _pre_call(x, ctx, mod3, mod_ctx3, ng, w_pre, onesq, onesk, qg, kg, cos, sin, row_perm)

    y_attn = _attn_call(q, k_all, v_all, _kv_expand(), tq=1024)

    ug = _perm_in_call(up, lane_perm, tr=384)
    m, s_in, gmat, a16 = _s5_operators(s5_lam_re[0], s5_lam_im[0], s5_log_dt[0], s5_b_re[0], s5_b_im[0],
                                       s5_c_re[0], s5_c_im[0], s5_d[0])
    yg = _s5_call(ug, m, s_in, gmat, a16, n_ctx // S5_CHUNK, gb=GROUPS_PER_BLOCK)
    yp = _perm_out_call(yg, _lane_perm(transpose=True), tr=256)

    return _final_call(x, mod3, ng, w_gates, y_attn, yp, _row_perm(bsz, PERM_TOK, transpose=True),
                       w_glu[0].astype(BF16), b_glu[0][None],
                       w_branch_attn[0].astype(BF16), w_branch_s5[0].astype(BF16), w_out[0].astype(BF16),
                       final_norm_g[None])
```

```python
import functools

import numpy as np
import jax
import jax.numpy as jnp
from jax import lax
from jax.experimental import pallas as pl
from jax.experimental.pallas import tpu as pltpu

D_MODEL = 1024
GRID_W = 64
N_HEADS = 8
N_KV_HEADS = 2
HEAD_DIM = 64
GQA_REP = N_HEADS // N_KV_HEADS
ATTN_WIDTH = N_HEADS * HEAD_DIM
KV_WIDTH = N_KV_HEADS * HEAD_DIM
ATTN_SCALE = HEAD_DIM ** -0.5
ROPE_THETA = 10000.0
ROPE_FREQS = HEAD_DIM // 4
S5_WIDTH = 512
S5_GROUP = 16
S5_GROUPS = S5_WIDTH // S5_GROUP
S5_STATE = 64
EPS = 1e-6
IN_SIZES = (ATTN_WIDTH, KV_WIDTH, KV_WIDTH, ATTN_WIDTH, S5_WIDTH, S5_WIDTH, D_MODEL, D_MODEL)

LANES = 128
SUBLANES = 8
S5_CHUNK = 16
S5_CW = S5_CHUNK * S5_GROUP
GROUPS_PER_BLOCK = LANES // S5_GROUP
N_BLOCKS = S5_WIDTH // LANES
BLOCK_W = S5_CHUNK * LANES
HALF_W = BLOCK_W // 2
TOK_TILE = 128
PERM_TOK = 64
REP_W = GQA_REP * HEAD_DIM
VMEM_LIMIT = 56 * 1024 * 1024

F32 = jnp.float32
BF16 = jnp.bfloat16


def _silu(t):
    return t * jax.nn.sigmoid(t)


def _modulated_norm(x3, mod_ref, ng_ref):
    ms = jnp.mean(x3 * x3, axis=-1, keepdims=True)
    y = x3 * lax.rsqrt(ms + EPS) * ng_ref[...]
    return y * (1.0 + mod_ref[:, :, D_MODEL:2 * D_MODEL]) + mod_ref[:, :, 0:D_MODEL]


def _head_rmsnorm(t, ones_ref, g_ref):
    ss = jnp.dot((t * t).astype(BF16), ones_ref[...], preferred_element_type=F32)
    return t * lax.rsqrt(ss * (1.0 / HEAD_DIM) + EPS) * g_ref[...]


def _rope(t, cos, sin_signed):
    rows = t.shape[0]
    lane = lax.broadcasted_iota(jnp.int32, (rows, LANES), 1)
    first = (lane & ROPE_FREQS) == 0
    outs = []
    for j in range(t.shape[1] // LANES):
        blk = t[:, j * LANES:(j + 1) * LANES]
        partner = jnp.where(first, pltpu.roll(blk, LANES - ROPE_FREQS, 1), pltpu.roll(blk, ROPE_FREQS, 1))
        outs.append(blk * cos + partner * sin_signed)
    return outs[0] if len(outs) == 1 else jnp.concatenate(outs, axis=1)


def _const_spec(shape, grid_rank):
    zeros = (0,) * len(shape)
    return pl.BlockSpec(shape, lambda *_: zeros, pipeline_mode=pl.Buffered(1))


def _ada_kernel(c_ref, w_ref, b_ref, o_ref):
    s = _silu(c_ref[...])
    w = w_ref[...]
    s_hi, w_hi = s.astype(BF16), w.astype(BF16)
    s_lo = (s - s_hi.astype(F32)).astype(BF16)
    w_lo = (w - w_hi.astype(F32)).astype(BF16)
    dot = functools.partial(jnp.dot, preferred_element_type=F32)
    o_ref[...] = dot(s_hi, w_hi) + dot(s_lo, w_hi) + dot(s_hi, w_lo) + b_ref[...]


def _ada_call(cc, w, b):
    rows, n = cc.shape[0], w.shape[1]
    tn = 1024
    return pl.pallas_call(
        _ada_kernel,
        out_shape=jax.ShapeDtypeStruct((rows, n), F32),
        grid=(n // tn,),
        in_specs=[pl.BlockSpec((rows, D_MODEL), lambda j: (0, 0)),
                  pl.BlockSpec((D_MODEL, tn), lambda j: (0, j)),
                  pl.BlockSpec((1, tn), lambda j: (0, j))],
        out_specs=pl.BlockSpec((rows, tn), lambda j: (0, j)),
        compiler_params=pltpu.CompilerParams(dimension_semantics=("arbitrary",)),
        name="ada",
    )(cc, w, b)


def _store_chunk_major(u, perm_ref, up_out, bsz):
    tt = u.shape[0] // bsz
    u3 = u.astype(BF16).reshape(bsz, tt, S5_WIDTH)
    rows = PERM_TOK // S5_CHUNK * bsz
    for part in range(tt // PERM_TOK):
        up = u3[:, part * PERM_TOK:(part + 1) * PERM_TOK, :].reshape(bsz * PERM_TOK, S5_WIDTH)
        r = jnp.dot(perm_ref[...], up, preferred_element_type=F32).astype(BF16)
        for t in range(S5_CHUNK):
            for blk in range(N_BLOCKS):
                up_out[part * rows:(part + 1) * rows, blk * BLOCK_W + t * LANES: blk * BLOCK_W + (t + 1) * LANES] = (
                    r[t * rows:(t + 1) * rows, blk * LANES:(blk + 1) * LANES])


def _pre_kernel(x_ref, c_ref, mod_ref, modc_ref, ng_ref, w_ref, onesq_ref, onesk_ref, qg_ref, kg_ref, cos_ref, sin_ref,
                perm_ref, q_out, k_out, v_out, up_out, *, ctx_steps):
    bsz, tt, _ = x_ref.shape
    o1, o2, o3 = ATTN_WIDTH, ATTN_WIDTH + KV_WIDTH, ATTN_WIDTH + 2 * KV_WIDTH
    step = pl.program_id(0)

    @pl.when(step < ctx_steps)
    def _():
        xn = _modulated_norm(c_ref[...], modc_ref, ng_ref).reshape(bsz * tt, D_MODEL).astype(BF16)
        p = jnp.dot(xn, w_ref[:, o1:], preferred_element_type=F32)
        k = _head_rmsnorm(p[:, 0:KV_WIDTH], onesk_ref, kg_ref)
        k_out[...] = k.astype(BF16).reshape(bsz, tt, KV_WIDTH)
        v_out[...] = p[:, KV_WIDTH:2 * KV_WIDTH].astype(BF16).reshape(bsz, tt, KV_WIDTH)
        _store_chunk_major(p[:, 2 * KV_WIDTH:], perm_ref, up_out, bsz)

    @pl.when(step >= ctx_steps)
    def _():
        xn = _modulated_norm(x_ref[...], mod_ref, ng_ref).reshape(bsz * tt, D_MODEL).astype(BF16)
        p = jnp.dot(xn, w_ref[...], preferred_element_type=F32)
        cos = jnp.concatenate([cos_ref[...]] * bsz, axis=0)
        sin = jnp.concatenate([sin_ref[...]] * bsz, axis=0)
        q = _rope(_head_rmsnorm(p[:, 0:o1], onesq_ref, qg_ref), cos, sin)
        k = _rope(_head_rmsnorm(p[:, o1:o2], onesk_ref, kg_ref), cos, sin)
        q_out[...] = (q * ATTN_SCALE).astype(BF16).reshape(bsz, tt, ATTN_WIDTH)
        k_out[...] = k.astype(BF16).reshape(bsz, tt, KV_WIDTH)
        v_out[...] = p[:, o2:o3].astype(BF16).reshape(bsz, tt, KV_WIDTH)
        _store_chunk_major(p[:, o3:], perm_ref, up_out, bsz)


def _pre_call(x, ctx, mod3, mod_ctx3, ng, w, onesq, onesk, qg, kg, cos, sin, perm):
    bsz, n, _ = x.shape
    n_ctx = ctx.shape[1]
    tt = TOK_TILE
    up_rows = tt // S5_CHUNK * bsz
    ctx_steps = n_ctx // tt
    lat = lambda i: jnp.maximum(i - ctx_steps, 0)
    c = lambda shape: _const_spec(shape, 1)
    return pl.pallas_call(
        functools.partial(_pre_kernel, ctx_steps=ctx_steps),
        out_shape=(jax.ShapeDtypeStruct((bsz, n, ATTN_WIDTH), BF16),
                   jax.ShapeDtypeStruct((bsz, n_ctx + n, KV_WIDTH), BF16),
                   jax.ShapeDtypeStruct((bsz, n_ctx + n, KV_WIDTH), BF16),
                   jax.ShapeDtypeStruct(((n_ctx + n) // S5_CHUNK * bsz, N_BLOCKS * BLOCK_W), BF16)),
        grid=(ctx_steps + n // tt,),
        in_specs=[pl.BlockSpec((bsz, tt, D_MODEL), lambda i: (0, lat(i), 0)),
                  pl.BlockSpec((bsz, tt, D_MODEL), lambda i: (0, jnp.minimum(i, ctx_steps - 1), 0)),
                  c((bsz, 1, 3 * D_MODEL)), c((1, 1, 3 * D_MODEL)),
                  c((1, D_MODEL)), c(w.shape), c(onesq.shape), c(onesk.shape),
                  c((1, ATTN_WIDTH)), c((1, KV_WIDTH)),
                  pl.BlockSpec((tt, LANES), lambda i: (lat(i), 0)),
                  pl.BlockSpec((tt, LANES), lambda i: (lat(i), 0)),
                  c(perm.shape)],
        out_specs=(pl.BlockSpec((bsz, tt, ATTN_WIDTH), lambda i: (0, lat(i), 0)),
                   pl.BlockSpec((bsz, tt, KV_WIDTH), lambda i: (0, i, 0)),
                   pl.BlockSpec((bsz, tt, KV_WIDTH), lambda i: (0, i, 0)),
                   pl.BlockSpec((up_rows, N_BLOCKS * BLOCK_W), lambda i: (i, 0))),
        compiler_params=pltpu.CompilerParams(dimension_semantics=("arbitrary",), vmem_limit_bytes=VMEM_LIMIT),
        name="pre",
    )(x, ctx, mod3, mod_ctx3, ng, w, onesq, onesk, qg, kg, cos, sin, perm)


ATTN_ROWS = 256


def _attn_kernel(q_ref, k_ref, v_ref, e_ref, o_ref, k4t_ref, v4_ref):
    @pl.when(pl.program_id(1) == 0)
    def _():
        k4 = jnp.dot(k_ref[...], e_ref[...], preferred_element_type=F32)
        k4t = k4.T
        v4 = jnp.dot(v_ref[...], e_ref[...], preferred_element_type=F32)
        for g in range(N_KV_HEADS):
            k4t_ref[g] = k4t[g * REP_W:(g + 1) * REP_W].astype(BF16)
            v4_ref[g] = v4[:, g * REP_W:(g + 1) * REP_W].astype(BF16)

    lane = lax.broadcasted_iota(jnp.int32, (ATTN_ROWS, REP_W), 1)
    for part in range(q_ref.shape[0] // ATTN_ROWS):
        rows = slice(part * ATTN_ROWS, (part + 1) * ATTN_ROWS)
        for g in range(N_KV_HEADS):
            qg = q_ref[rows, g * REP_W:(g + 1) * REP_W]
            acc = jnp.zeros((ATTN_ROWS, REP_W), F32)
            for r in range(GQA_REP):
                in_head = (lane >= r * HEAD_DIM) & (lane < (r + 1) * HEAD_DIM)
                qr = jnp.where(in_head, qg, jnp.zeros_like(qg))
                s = jnp.dot(qr, k4t_ref[g], preferred_element_type=F32)
                m = jnp.max(s, axis=1, keepdims=True)
                p = jnp.exp(s - m)
                l = jnp.sum(p, axis=1, keepdims=True)
                o = jnp.dot(p.astype(BF16), v4_ref[g], preferred_element_type=F32)
                acc = jnp.where(in_head, o / l, acc)
            o_ref[rows, g * REP_W:(g + 1) * REP_W] = acc.astype(o_ref.dtype)


def _attn_call(q, k_all, v_all, expand, tq):
    bsz, n, _ = q.shape
    nk = k_all.shape[1]
    return pl.pallas_call(
        _attn_kernel,
        out_shape=jax.ShapeDtypeStruct((bsz, n, ATTN_WIDTH), BF16),
        grid=(bsz, n // tq),
        in_specs=[pl.BlockSpec((None, tq, ATTN_WIDTH), lambda b, i: (b, i, 0)),
                  pl.BlockSpec((None, nk, KV_WIDTH), lambda b, i: (b, 0, 0)),
                  pl.BlockSpec((None, nk, KV_WIDTH), lambda b, i: (b, 0, 0)),
                  _const_spec(expand.shape, 2)],
        out_specs=pl.BlockSpec((None, tq, ATTN_WIDTH), lambda b, i: (b, i, 0)),
        scratch_shapes=[pltpu.VMEM((N_KV_HEADS, REP_W, nk), BF16),
                        pltpu.VMEM((N_KV_HEADS, nk, REP_W), BF16)],
        compiler_params=pltpu.CompilerParams(dimension_semantics=("arbitrary", "arbitrary"),
                                             vmem_limit_bytes=VMEM_LIMIT),
        name="attn",
    )(q, k_all, v_all, expand)


def _perm_in_kernel(s_ref, p_ref, o_ref):
    halves = [jnp.dot(s_ref[:, hf * HALF_W:(hf + 1) * HALF_W], p_ref[...], preferred_element_type=F32)
              for hf in range(BLOCK_W // HALF_W)]
    for j in range(GROUPS_PER_BLOCK):
        o_ref[j] = jnp.concatenate([r[:, j * LANES:(j + 1) * LANES] for r in halves], axis=1).astype(o_ref.dtype)


def _perm_in_call(up, pmat, tr):
    rows = up.shape[0]
    return pl.pallas_call(
        _perm_in_kernel,
        out_shape=jax.ShapeDtypeStruct((S5_GROUPS, rows, S5_CW), BF16),
        grid=(N_BLOCKS, rows // tr),
        in_specs=[pl.BlockSpec((tr, BLOCK_W), lambda s, i: (i, s)),
                  _const_spec(pmat.shape, 2)],
        out_specs=pl.BlockSpec((GROUPS_PER_BLOCK, tr, S5_CW), lambda s, i: (s, i, 0)),
        compiler_params=pltpu.CompilerParams(dimension_semantics=("arbitrary", "arbitrary"),
                                             vmem_limit_bytes=VMEM_LIMIT),
        name="perm_in",
    )(up, pmat)


def _perm_out_kernel(y_ref, q_ref, o_ref):
    for hf in range(BLOCK_W // HALF_W):
        ycat = jnp.concatenate([y_ref[j, :, hf * LANES:(hf + 1) * LANES] for j in range(GROUPS_PER_BLOCK)], axis=1)
        o_ref[:, hf * HALF_W:(hf + 1) * HALF_W] = jnp.dot(ycat, q_ref[...],
                                                         preferred_element_type=F32).astype(o_ref.dtype)


def _perm_out_call(yg, qmat, tr):
    rows = yg.shape[1]
    return pl.pallas_call(
        _perm_out_kernel,
        out_shape=jax.ShapeDtypeStruct((rows, N_BLOCKS * BLOCK_W), BF16),
        grid=(N_BLOCKS, rows // tr),
        in_specs=[pl.BlockSpec((GROUPS_PER_BLOCK, tr, S5_CW), lambda s, i: (s, i, 0)),
                  _const_spec(qmat.shape, 2)],
        out_specs=pl.BlockSpec((tr, BLOCK_W), lambda s, i: (i, s)),
        compiler_params=pltpu.CompilerParams(dimension_semantics=("arbitrary", "arbitrary"),
                                             vmem_limit_bytes=VMEM_LIMIT),
        name="perm_out",
    )(yg, qmat)


def _s5_kernel(u_ref, m_ref, sin_ref, g_ref, a_ref, y_ref, inc_ref, hin_ref, *, n_ctx_chunks, n_chunks):
    half = S5_STATE
    gb = u_ref.shape[0]
    for j in range(gb):
        inc_ref[j] = jnp.dot(u_ref[j], sin_ref[j], preferred_element_type=F32)
    a_re = [jnp.broadcast_to(a_ref[j, 0:1, :], (SUBLANES, LANES)) for j in range(gb)]
    a_im = [jnp.broadcast_to(a_ref[j, 1:2, :], (SUBLANES, LANES)) for j in range(gb)]
    is_fwd = lax.broadcasted_iota(jnp.int32, (SUBLANES, LANES), 1) < half

    def step(k, carry):
        pos_b = jnp.where(k < n_ctx_chunks, n_ctx_chunks - 1 - k, n_chunks + n_ctx_chunks - 1 - k)
        rf = pl.multiple_of(k * SUBLANES, SUBLANES)
        rb = pl.multiple_of(pos_b * SUBLANES, SUBLANES)
        out = []
        for j in range(gb):
            h_re, h_im = carry[2 * j], carry[2 * j + 1]
            xf = inc_ref[j, pl.ds(rf, SUBLANES), :]
            xb = inc_ref[j, pl.ds(rb, SUBLANES), :]
            hin_ref[j, pl.ds(rf, SUBLANES), 0:half] = h_re[:, 0:half]
            hin_ref[j, pl.ds(rf, SUBLANES), 2 * half:3 * half] = h_im[:, 0:half]
            hin_ref[j, pl.ds(rb, SUBLANES), half:2 * half] = h_re[:, half:]
            hin_ref[j, pl.ds(rb, SUBLANES), 3 * half:] = h_im[:, half:]
            x_re = jnp.where(is_fwd, xf[:, 0:LANES], xb[:, 0:LANES])
            x_im = jnp.where(is_fwd, xf[:, LANES:], xb[:, LANES:])
            out.append(a_re[j] * h_re - a_im[j] * h_im + x_re)
            out.append(a_re[j] * h_im + a_im[j] * h_re + x_im)
        return tuple(out)

    zero = jnp.zeros((SUBLANES, LANES), F32)
    lax.fori_loop(0, n_chunks, step, (zero,) * (2 * gb))

    r0 = n_ctx_chunks * SUBLANES
    for j in range(gb):
        y = jnp.dot(u_ref[j, r0:, :], m_ref[j], preferred_element_type=F32)
        y = y + jnp.dot(hin_ref[j, r0:, :].astype(BF16), g_ref[j], preferred_element_type=F32)
        y_ref[j] = y.astype(y_ref.dtype)


def _s5_call(ug, m, sin, gmat, a16, n_ctx_chunks, gb):
    groups, rows, _ = ug.shape
    n_chunks = rows // SUBLANES
    out_rows = rows - n_ctx_chunks * SUBLANES
    mat = lambda: pl.BlockSpec((gb, S5_CW, S5_CW), lambda g: (g, 0, 0))
    return pl.pallas_call(
        functools.partial(_s5_kernel, n_ctx_chunks=n_ctx_chunks, n_chunks=n_chunks),
        out_shape=jax.ShapeDtypeStruct((groups, out_rows, S5_CW), BF16),
        grid=(groups // gb,),
        in_specs=[pl.BlockSpec((gb, rows, S5_CW), lambda g: (g, 0, 0)),
                  mat(), mat(), mat(),
                  pl.BlockSpec((gb, 2, LANES), lambda g: (g, 0, 0))],
        out_specs=pl.BlockSpec((gb, out_rows, S5_CW), lambda g: (g, 0, 0)),
        scratch_shapes=[pltpu.VMEM((gb, rows, S5_CW), F32), pltpu.VMEM((gb, rows, S5_CW), F32)],
        compiler_params=pltpu.CompilerParams(dimension_semantics=("arbitrary",),
                                             vmem_limit_bytes=VMEM_LIMIT),
        name="s5",
    )(ug, m, sin, gmat, a16)


def _s5_ops_kernel(*refs):
    for j in range(refs[0].shape[0]):
        _s5_ops_group(*(r.at[j] for r in refs))


def _s5_ops_group(lam_ref, ldt_ref, bt_ref, ct_ref, d_ref, m_ref, sin_ref, g_ref, a_ref):
    T, H = S5_CHUNK, S5_GROUP
    lr = jnp.minimum(lam_ref[0:1, :], -1e-4)
    li = lam_ref[1:2, :]
    dt = jnp.exp(ldt_ref[...])
    taus = lax.broadcasted_iota(jnp.int32, (3 * SUBLANES, LANES), 0).astype(F32)
    mag = jnp.exp(lr * dt * taus)
    pw_r = mag * jnp.cos(li * dt * taus)
    pw_i = mag * jnp.sin(li * dt * taus)
    nr, ni = pw_r[1:2] - 1.0, pw_i[1:2]
    den = lr * lr + li * li
    cf_r = (nr * lr + ni * li) / den
    cf_i = (ni * lr - nr * li) / den
    bb_r = cf_r * bt_ref[0] - cf_i * bt_ref[1]
    bb_i = cf_r * bt_ref[1] + cf_i * bt_ref[0]
    is_fwd = lax.broadcasted_iota(jnp.int32, (H, LANES), 1) < S5_STATE

    def powers(tau_f, tau_b):
        pick = lambda pw, s: jnp.where(is_fwd, jnp.broadcast_to(pw[tau_f(s):tau_f(s) + 1], (H, LANES)),
                                       jnp.broadcast_to(pw[tau_b(s):tau_b(s) + 1], (H, LANES)))
        return (jnp.concatenate([pick(pw_r, s) for s in range(T)], axis=0),
                jnp.concatenate([pick(pw_i, s) for s in range(T)], axis=0))

    tile = lambda a: jnp.concatenate([a] * T, axis=0)
    bbr, bbi, cr, ci = tile(bb_r), tile(bb_i), tile(ct_ref[0]), tile(ct_ref[1])

    er, ei = powers(lambda s: T - 1 - s, lambda s: s)
    sin_ref[:, 0:LANES] = (er * bbr - ei * bbi).astype(sin_ref.dtype)
    sin_ref[:, LANES:] = (er * bbi + ei * bbr).astype(sin_ref.dtype)

    er, ei = powers(lambda t: t + 1, lambda t: T - t)
    gt = jnp.concatenate([er * cr - ei * ci, -(er * ci + ei * cr)], axis=1)
    g_ref[...] = gt.T.astype(g_ref.dtype)

    er, ei = powers(lambda a: a, lambda a: T - 1 - a)
    cp = jnp.concatenate([er * cr - ei * ci, er * ci + ei * cr], axis=1)
    zero = jnp.zeros_like(bb_r)
    lhs = jnp.concatenate([jnp.concatenate([jnp.where(is_fwd, bb_r, zero), jnp.where(is_fwd, -bb_i, zero)], axis=1),
                           jnp.concatenate([jnp.where(is_fwd, zero, bb_r), jnp.where(is_fwd, zero, -bb_i)], axis=1)],
                          axis=0)
    kr = lax.dot_general(lhs, cp, (((1,), (1,)), ((), ())), preferred_element_type=F32,
                         precision=lax.Precision.HIGHEST)
    pad = jnp.zeros((H, S5_CW), F32)
    wide_f = jnp.concatenate([pad, kr[0:H]], axis=1)
    wide_b = jnp.concatenate([kr[H:], pad], axis=1)
    lane = lax.broadcasted_iota(jnp.int32, (H, S5_CW), 1)
    row = lax.broadcasted_iota(jnp.int32, (H, S5_CW), 0)
    skip = jnp.broadcast_to(d_ref[...], (H, S5_CW))
    for s in range(T):
        blk_f = pltpu.roll(wide_f, H * s, 1)[:, S5_CW:] if s else wide_f[:, S5_CW:]
        shift_b = (2 * S5_CW - H * (T - 1 - s)) % (2 * S5_CW)
        blk_b = (pltpu.roll(wide_b, shift_b, 1) if shift_b else wide_b)[:, :S5_CW]
        diag = jnp.where(lane == H * s + row, skip, 0.0)
        m_ref[H * s:H * (s + 1), :] = (blk_f + blk_b + diag).astype(m_ref.dtype)
    a_ref[0:1, :] = pw_r[T:T + 1]
    a_ref[1:2, :] = pw_i[T:T + 1]


def _s5_ops_call(lam, ldt, bt, ct, dt_tiled):
    groups = lam.shape[0]
    gb = 8
    mat = lambda: pl.BlockSpec((gb, S5_CW, S5_CW), lambda g: (g, 0, 0))
    vec = lambda a: pl.BlockSpec((gb,) + a.shape[1:], lambda g: (g,) + (0,) * (a.ndim - 1))
    mshape = jax.ShapeDtypeStruct((groups, S5_CW, S5_CW), BF16)
    return pl.pallas_call(
        _s5_ops_kernel,
        out_shape=(mshape, mshape, mshape, jax.ShapeDtypeStruct((groups, 2, LANES), F32)),
        grid=(groups // gb,),
        in_specs=[vec(lam), vec(ldt), vec(bt), vec(ct), vec(dt_tiled)],
        out_specs=(mat(), mat(), mat(), pl.BlockSpec((gb, 2, LANES), lambda g: (g, 0, 0))),
        compiler_params=pltpu.CompilerParams(dimension_semantics=("arbitrary",)),
        name="s5_ops",
    )(lam, ldt, bt, ct, dt_tiled)


def _s5_operators(lam_re, lam_im, log_dt, b_re, b_im, c_re, c_im, d_skip):
    G, P, H = S5_GROUPS, S5_STATE, S5_GROUP
    lam = jnp.stack([lam_re, lam_im]).astype(F32).transpose(2, 0, 1, 3).reshape(G, 2, 2 * P)
    ldt = jnp.repeat(log_dt.astype(F32).T, P, axis=1).reshape(G, 1, 2 * P)
    bt = jnp.stack([b_re, b_im]).astype(F32).transpose(2, 0, 4, 1, 3).reshape(G, 2, H, 2 * P)
    ct = jnp.stack([c_re, c_im]).astype(F32).transpose(2, 0, 3, 1, 4).reshape(G, 2, H, 2 * P)
    dt_tiled = jnp.tile(d_skip.astype(F32).reshape(G, 1, H), (1, 1, S5_CHUNK))
    return _s5_ops_call(lam, ldt, bt, ct, dt_tiled)


def _final_kernel(x_ref, mod_ref, ng_ref, wg_ref, ya_ref, yp_ref, permt_ref, wglu_ref, bglu_ref, wa_ref, wb_ref,
                  wo_ref, fg_ref, o_ref):
    bsz, tt, _ = x_ref.shape
    rows = bsz * tt
    x3 = x_ref[...]
    xn = _modulated_norm(x3, mod_ref, ng_ref).reshape(rows, D_MODEL).astype(BF16)
    gates = jnp.dot(xn, wg_ref[...], preferred_element_type=F32)
    o1, o2, o3 = ATTN_WIDTH, ATTN_WIDTH + S5_WIDTH, ATTN_WIDTH + S5_WIDTH + D_MODEL
    prow = PERM_TOK // S5_CHUNK * bsz
    parts = []
    for part in range(tt // PERM_TOK):
        slabs = [jnp.concatenate([yp_ref[part * prow:(part + 1) * prow,
                                         blk * BLOCK_W + t * LANES: blk * BLOCK_W + (t + 1) * LANES]
                                  for blk in range(N_BLOCKS)], axis=1) for t in range(S5_CHUNK)]
        yb = jnp.dot(permt_ref[...], jnp.concatenate(slabs, axis=0), preferred_element_type=F32)
        parts.append(yb.reshape(bsz, PERM_TOK, S5_WIDTH))
    y = (parts[0] if len(parts) == 1 else jnp.concatenate(parts, axis=1)).reshape(rows, S5_WIDTH)
    z = y * (0.5 * (1.0 + jnp.tanh(0.7978845608028654 * (y + 0.044715 * (y * y * y)))))
    zz = z * jax.nn.sigmoid(jnp.dot(z.astype(BF16), wglu_ref[...], preferred_element_type=F32) + bglu_ref[...])
    ya = ya_ref[...].reshape(rows, ATTN_WIDTH).astype(F32)
    ta = (ya * _silu(gates[:, 0:o1])).astype(BF16)
    tb = (zz * _silu(gates[:, o1:o2])).astype(BF16)
    pa = jnp.dot(ta, wa_ref[...], preferred_element_type=F32)
    pb = jnp.dot(tb, wb_ref[...], preferred_element_type=F32)
    mix = jax.nn.sigmoid(gates[:, o2:o3]) * pa + jax.nn.sigmoid(gates[:, o3:]) * pb
    o = jnp.dot(mix.astype(BF16), wo_ref[...], preferred_element_type=F32).reshape(bsz, tt, D_MODEL)
    h = x3 + mod_ref[:, :, 2 * D_MODEL:] * o
    ms = jnp.mean(h * h, axis=-1, keepdims=True)
    o_ref[...] = h * lax.rsqrt(ms + EPS) * fg_ref[...]


def _final_call(x, mod3, ng, wg, ya, yp, permt, wglu, bglu, wa, wb, wo, fg):
    bsz, n, _ = x.shape
    tt = TOK_TILE
    up_rows = tt // S5_CHUNK * bsz
    tok = lambda width: pl.BlockSpec((bsz, tt, width), lambda i: (0, i, 0))
    c = lambda shape: _const_spec(shape, 1)
    return pl.pallas_call(
        _final_kernel,
        out_shape=jax.ShapeDtypeStruct((bsz, n, D_MODEL), F32),
        grid=(n // tt,),
        in_specs=[tok(D_MODEL),
                  pl.BlockSpec((bsz, 1, 3 * D_MODEL), lambda i: (0, 0, 0)),
                  c((1, D_MODEL)), c(wg.shape),
                  tok(ATTN_WIDTH),
                  pl.BlockSpec((up_rows, N_BLOCKS * BLOCK_W), lambda i: (i, 0)),
                  c(permt.shape), c(wglu.shape), c((1, S5_WIDTH)), c(wa.shape), c(wb.shape), c(wo.shape),
                  c((1, D_MODEL))],
        out_specs=tok(D_MODEL),
        compiler_params=pltpu.CompilerParams(dimension_semantics=("arbitrary",), vmem_limit_bytes=VMEM_LIMIT),
        name="final",
    )(x, mod3, ng, wg, ya, yp, permt, wglu, bglu, wa, wb, wo, fg)


def _rope_tables(n):
    rows = n // GRID_W
    row_ids = np.repeat(np.arange(rows, dtype=np.float64), GRID_W)
    col_ids = np.tile(np.arange(GRID_W, dtype=np.float64), rows)
    freqs = ROPE_THETA ** (-np.arange(ROPE_FREQS, dtype=np.float64) / ROPE_FREQS)
    ang_r, ang_c = row_ids[:, None] * freqs, col_ids[:, None] * freqs
    cos = np.concatenate([np.cos(ang_r)] * 2 + [np.cos(ang_c)] * 2, axis=1)
    sin = np.concatenate([-np.sin(ang_r), np.sin(ang_r), -np.sin(ang_c), np.sin(ang_c)], axis=1)
    reps = LANES // HEAD_DIM
    return (jnp.asarray(np.tile(cos, (1, reps)).astype(np.float32)),
            jnp.asarray(np.tile(sin, (1, reps)).astype(np.float32)))


def _one_hot(match):
    return jnp.asarray(np.ascontiguousarray(match).astype(BF16))


def _block_ones(width):
    idx = np.arange(width) // HEAD_DIM
    return _one_hot(idx[:, None] == idx[None, :])


def _row_perm(bsz, tt, transpose=False):
    chunks = tt // S5_CHUNK
    r = np.arange(bsz * tt)
    t, pc, b = r // (chunks * bsz), (r // bsz) % chunks, r % bsz
    src = b * tt + pc * S5_CHUNK + t
    match = src[:, None] == np.arange(bsz * tt)[None, :]
    return _one_hot(match.T if transpose else match)


def _lane_perm(transpose=False):
    r = np.arange(HALF_W)
    t, j, h = r // LANES, (r % LANES) // S5_GROUP, r % S5_GROUP
    dst = j * LANES + t * S5_GROUP + h
    match = dst[:, None] == np.arange(HALF_W)[None, :]
    return _one_hot(match.T if transpose else match)


def _kv_expand():
    col = np.arange(N_KV_HEADS * REP_W)
    src = (col // REP_W) * HEAD_DIM + col % HEAD_DIM
    return _one_hot(np.arange(KV_WIDTH)[:, None] == src[None, :])


def kernel(x, c, ctx, c_ctx, norm_g, w_ada, b_ada, w_in, q_norm_g, k_norm_g, s5_lam_re, s5_lam_im, s5_log_dt,
           s5_b_re, s5_b_im, s5_c_re, s5_c_im, s5_d, w_glu, b_glu, w_branch_attn, w_branch_s5, w_out,
           final_norm_g):
    assert w_in.shape[0] == 1, "single-layer block"
    bsz, n, _ = x.shape
    n_ctx = ctx.shape[1]
    assert n % TOK_TILE == 0 and n_ctx % TOK_TILE == 0 and bsz == SUBLANES

    ada_rows = 2 * SUBLANES
    cc = jnp.concatenate([c, c_ctx[None], jnp.zeros((ada_rows - bsz - 1, D_MODEL), F32)], axis=0)
    mod = _ada_call(cc, w_ada[0], b_ada[0][None])
    mod3 = mod[:bsz].reshape(bsz, 1, 3 * D_MODEL)
    mod_ctx3 = mod[bsz:bsz + 1].reshape(1, 1, 3 * D_MODEL)

    offs = [0]
    for s in IN_SIZES:
        offs.append(offs[-1] + s)
    w_bf = w_in[0].astype(BF16)
    w_pre = jnp.concatenate([w_bf[:, offs[0]:offs[3]], w_bf[:, offs[4]:offs[5]]], axis=1)
    w_gates = jnp.concatenate([w_bf[:, offs[3]:offs[4]], w_bf[:, offs[5]:]], axis=1)

    ng = norm_g[0][None]
    qg = jnp.tile(q_norm_g[0], N_HEADS)[None]
    kg = jnp.tile(k_norm_g[0], N_KV_HEADS)[None]
    onesq, onesk = _block_ones(ATTN_WIDTH), _block_ones(KV_WIDTH)
    cos, sin = _rope_tables(n)
    row_perm = _row_perm(bsz, PERM_TOK)
    lane_perm = _lane_perm()

    q, k_all, v_all, up = _pre_call(x, ctx, mod3, mod_ctx3, ng, w_pre, onesq, onesk, qg, kg, cos, sin, row_perm)

    y_attn = _attn_call(q, k_all, v_all, _kv_expand(), tq=1024)

    ug = _perm_in_call(up, lane_perm, tr=576)
    m, s_in, gmat, a16 = _s5_operators(s5_lam_re[0], s5_lam_im[0], s5_log_dt[0], s5_b_re[0], s5_b_im[0],
                                       s5_c_re[0], s5_c_im[0], s5_d[0])
    yg = _s5_call(ug, m, s_in, gmat, a16, n_ctx // S5_CHUNK, gb=GROUPS_PER_BLOCK)
    yp = _perm_out_call(yg, _lane_perm(transpose=True), tr=512)

    return _final_call(x, mod3, ng, w_gates, y_attn, yp, _row_perm(bsz, PERM_TOK, transpose=True),
                       w_glu[0].astype(BF16), b_glu[0][None],
                       w_branch_attn[0].astype(BF16), w_branch_s5[0].astype(BF16), w_out[0].astype(BF16),
                       final_norm_g[None])
```

```python
import functools

import numpy as np
import jax
import jax.numpy as jnp
from jax import lax
from jax.experimental import pallas as pl
from jax.experimental.pallas import tpu as pltpu

D_MODEL = 1024
GRID_W = 64
N_HEADS = 8
N_KV_HEADS = 2
HEAD_DIM = 64
GQA_REP = N_HEADS // N_KV_HEADS
ATTN_WIDTH = N_HEADS * HEAD_DIM
KV_WIDTH = N_KV_HEADS * HEAD_DIM
ATTN_SCALE = HEAD_DIM ** -0.5
ROPE_THETA = 10000.0
ROPE_FREQS = HEAD_DIM // 4
S5_WIDTH = 512
S5_GROUP = 16
S5_GROUPS = S5_WIDTH // S5_GROUP
S5_STATE = 64
EPS = 1e-6
IN_SIZES = (ATTN_WIDTH, KV_WIDTH, KV_WIDTH, ATTN_WIDTH, S5_WIDTH, S5_WIDTH, D_MODEL, D_MODEL)

LANES = 128
SUBLANES = 8
S5_CHUNK = 16
S5_CW = S5_CHUNK * S5_GROUP
GROUPS_PER_BLOCK = LANES // S5_GROUP
N_BLOCKS = S5_WIDTH // LANES
BLOCK_W = S5_CHUNK * LANES
HALF_W = BLOCK_W // 2
TOK_TILE = 128
PERM_TOK = 64
REP_W = GQA_REP * HEAD_DIM
VMEM_LIMIT = 56 * 1024 * 1024

F32 = jnp.float32
BF16 = jnp.bfloat16


def _silu(t):
    return t * jax.nn.sigmoid(t)


def _modulated_norm(x3, mod_ref, ng_ref):
    ms = jnp.mean(x3 * x3, axis=-1, keepdims=True)
    y = x3 * lax.rsqrt(ms + EPS) * ng_ref[...]
    return y * (1.0 + mod_ref[:, :, D_MODEL:2 * D_MODEL]) + mod_ref[:, :, 0:D_MODEL]


def _head_rmsnorm(t, ones_ref, g_ref):
    ss = jnp.dot((t * t).astype(BF16), ones_ref[...], preferred_element_type=F32)
    return t * lax.rsqrt(ss * (1.0 / HEAD_DIM) + EPS) * g_ref[...]


def _rope(t, cos, sin_signed):
    rows = t.shape[0]
    lane = lax.broadcasted_iota(jnp.int32, (rows, LANES), 1)
    first = (lane & ROPE_FREQS) == 0
    outs = []
    for j in range(t.shape[1] // LANES):
        blk = t[:, j * LANES:(j + 1) * LANES]
        partner = jnp.where(first, pltpu.roll(blk, LANES - ROPE_FREQS, 1), pltpu.roll(blk, ROPE_FREQS, 1))
        outs.append(blk * cos + partner * sin_signed)
    return outs[0] if len(outs) == 1 else jnp.concatenate(outs, axis=1)


def _const_spec(shape, grid_rank):
    zeros = (0,) * len(shape)
    return pl.BlockSpec(shape, lambda *_: zeros, pipeline_mode=pl.Buffered(1))


def _ada_kernel(c_ref, w_ref, b_ref, o_ref):
    s = _silu(c_ref[...])
    w = w_ref[...]
    s_hi, w_hi = s.astype(BF16), w.astype(BF16)
    s_lo = (s - s_hi.astype(F32)).astype(BF16)
    w_lo = (w - w_hi.astype(F32)).astype(BF16)
    dot = functools.partial(jnp.dot, preferred_element_type=F32)
    o_ref[...] = dot(s_hi, w_hi) + dot(s_lo, w_hi) + dot(s_hi, w_lo) + b_ref[...]


def _ada_call(cc, w, b):
    rows, n = cc.shape[0], w.shape[1]
    tn = 1024
    return pl.pallas_call(
        _ada_kernel,
        out_shape=jax.ShapeDtypeStruct((rows, n), F32),
        grid=(n // tn,),
        in_specs=[pl.BlockSpec((rows, D_MODEL), lambda j: (0, 0)),
                  pl.BlockSpec((D_MODEL, tn), lambda j: (0, j)),
                  pl.BlockSpec((1, tn), lambda j: (0, j))],
        out_specs=pl.BlockSpec((rows, tn), lambda j: (0, j)),
        compiler_params=pltpu.CompilerParams(dimension_semantics=("arbitrary",)),
        name="ada",
    )(cc, w, b)


def _store_chunk_major(u, perm_ref, up_out, bsz):
    tt = u.shape[0] // bsz
    u3 = u.astype(BF16).reshape(bsz, tt, S5_WIDTH)
    rows = PERM_TOK // S5_CHUNK * bsz
    for part in range(tt // PERM_TOK):
        up = u3[:, part * PERM_TOK:(part + 1) * PERM_TOK, :].reshape(bsz * PERM_TOK, S5_WIDTH)
        r = jnp.dot(perm_ref[...], up, preferred_element_type=F32).astype(BF16)
        for t in range(S5_CHUNK):
            for blk in range(N_BLOCKS):
                up_out[part * rows:(part + 1) * rows, blk * BLOCK_W + t * LANES: blk * BLOCK_W + (t + 1) * LANES] = (
                    r[t * rows:(t + 1) * rows, blk * LANES:(blk + 1) * LANES])


def _pre_kernel(x_ref, c_ref, mod_ref, modc_ref, ng_ref, w_ref, onesq_ref, onesk_ref, qg_ref, kg_ref, cos_ref, sin_ref,
                perm_ref, q_out, k_out, v_out, up_out, *, ctx_steps):
    bsz, tt, _ = x_ref.shape
    o1, o2, o3 = ATTN_WIDTH, ATTN_WIDTH + KV_WIDTH, ATTN_WIDTH + 2 * KV_WIDTH
    step = pl.program_id(0)

    @pl.when(step < ctx_steps)
    def _():
        xn = _modulated_norm(c_ref[...], modc_ref, ng_ref).reshape(bsz * tt, D_MODEL).astype(BF16)
        p = jnp.dot(xn, w_ref[:, o1:], preferred_element_type=F32)
        k = _head_rmsnorm(p[:, 0:KV_WIDTH], onesk_ref, kg_ref)
        k_out[...] = k.astype(BF16).reshape(bsz, tt, KV_WIDTH)
        v_out[...] = p[:, KV_WIDTH:2 * KV_WIDTH].astype(BF16).reshape(bsz, tt, KV_WIDTH)
        _store_chunk_major(p[:, 2 * KV_WIDTH:], perm_ref, up_out, bsz)

    @pl.when(step >= ctx_steps)
    def _():
        xn = _modulated_norm(x_ref[...], mod_ref, ng_ref).reshape(bsz * tt, D_MODEL).astype(BF16)
        p = jnp.dot(xn, w_ref[...], preferred_element_type=F32)
        cos = jnp.concatenate([cos_ref[...]] * bsz, axis=0)
        sin = jnp.concatenate([sin_ref[...]] * bsz, axis=0)
        q = _rope(_head_rmsnorm(p[:, 0:o1], onesq_ref, qg_ref), cos, sin)
        k = _rope(_head_rmsnorm(p[:, o1:o2], onesk_ref, kg_ref), cos, sin)
        q_out[...] = (q * ATTN_SCALE).astype(BF16).reshape(bsz, tt, ATTN_WIDTH)
        k_out[...] = k.astype(BF16).reshape(bsz, tt, KV_WIDTH)
        v_out[...] = p[:, o2:o3].astype(BF16).reshape(bsz, tt, KV_WIDTH)
        _store_chunk_major(p[:, o3:], perm_ref, up_out, bsz)


def _pre_call(x, ctx, mod3, mod_ctx3, ng, w, onesq, onesk, qg, kg, cos, sin, perm):
    bsz, n, _ = x.shape
    n_ctx = ctx.shape[1]
    tt = TOK_TILE
    up_rows = tt // S5_CHUNK * bsz
    ctx_steps = n_ctx // tt
    lat = lambda i: jnp.maximum(i - ctx_steps, 0)
    c = lambda shape: _const_spec(shape, 1)
    return pl.pallas_call(
        functools.partial(_pre_kernel, ctx_steps=ctx_steps),
        out_shape=(jax.ShapeDtypeStruct((bsz, n, ATTN_WIDTH), BF16),
                   jax.ShapeDtypeStruct((bsz, n_ctx + n, KV_WIDTH), BF16),
                   jax.ShapeDtypeStruct((bsz, n_ctx + n, KV_WIDTH), BF16),
                   jax.ShapeDtypeStruct(((n_ctx + n) // S5_CHUNK * bsz, N_BLOCKS * BLOCK_W), BF16)),
        grid=(ctx_steps + n // tt,),
        in_specs=[pl.BlockSpec((bsz, tt, D_MODEL), lambda i: (0, lat(i), 0)),
                  pl.BlockSpec((bsz, tt, D_MODEL), lambda i: (0, jnp.minimum(i, ctx_steps - 1), 0)),
                  c((bsz, 1, 3 * D_MODEL)), c((1, 1, 3 * D_MODEL)),
                  c((1, D_MODEL)), c(w.shape), c(onesq.shape), c(onesk.shape),
                  c((1, ATTN_WIDTH)), c((1, KV_WIDTH)),
                  pl.BlockSpec((tt, LANES), lambda i: (lat(i), 0)),
                  pl.BlockSpec((tt, LANES), lambda i: (lat(i), 0)),
                  c(perm.shape)],
        out_specs=(pl.BlockSpec((bsz, tt, ATTN_WIDTH), lambda i: (0, lat(i), 0)),
                   pl.BlockSpec((bsz, tt, KV_WIDTH), lambda i: (0, i, 0)),
                   pl.BlockSpec((bsz, tt, KV_WIDTH), lambda i: (0, i, 0)),
                   pl.BlockSpec((up_rows, N_BLOCKS * BLOCK_W), lambda i: (i, 0))),
        compiler_params=pltpu.CompilerParams(dimension_semantics=("arbitrary",), vmem_limit_bytes=VMEM_LIMIT),
        name="pre",
    )(x, ctx, mod3, mod_ctx3, ng, w, onesq, onesk, qg, kg, cos, sin, perm)


ATTN_ROWS = 256


def _attn_kernel(q_ref, k_ref, v_ref, e_ref, o_ref, k4t_ref, v4_ref):
    @pl.when(pl.program_id(1) == 0)
    def _():
        kt = k_ref[...].astype(F32).T.astype(BF16)
        v4 = jnp.dot(v_ref[...], e_ref[...], preferred_element_type=F32)
        for g in range(N_KV_HEADS):
            k4t_ref[g] = jnp.concatenate([kt[g * HEAD_DIM:(g + 1) * HEAD_DIM]] * GQA_REP, axis=0)
            v4_ref[g] = v4[:, g * REP_W:(g + 1) * REP_W].astype(BF16)

    lane = lax.broadcasted_iota(jnp.int32, (ATTN_ROWS, REP_W), 1)
    for part in range(q_ref.shape[0] // ATTN_ROWS):
        rows = slice(part * ATTN_ROWS, (part + 1) * ATTN_ROWS)
        for g in range(N_KV_HEADS):
            qg = q_ref[rows, g * REP_W:(g + 1) * REP_W]
            acc = jnp.zeros((ATTN_ROWS, REP_W), F32)
            for r in range(GQA_REP):
                in_head = (lane >= r * HEAD_DIM) & (lane < (r + 1) * HEAD_DIM)
                qr = jnp.where(in_head, qg, jnp.zeros_like(qg))
                s = jnp.dot(qr, k4t_ref[g], preferred_element_type=F32)
                m = jnp.max(s, axis=1, keepdims=True)
                p = jnp.exp(s - m)
                l = jnp.sum(p, axis=1, keepdims=True)
                o = jnp.dot(p.astype(BF16), v4_ref[g], preferred_element_type=F32)
                acc = jnp.where(in_head, o / l, acc)
            o_ref[rows, g * REP_W:(g + 1) * REP_W] = acc.astype(o_ref.dtype)


def _attn_call(q, k_all, v_all, expand, tq):
    bsz, n, _ = q.shape
    nk = k_all.shape[1]
    return pl.pallas_call(
        _attn_kernel,
        out_shape=jax.ShapeDtypeStruct((bsz, n, ATTN_WIDTH), BF16),
        grid=(bsz, n // tq),
        in_specs=[pl.BlockSpec((None, tq, ATTN_WIDTH), lambda b, i: (b, i, 0)),
                  pl.BlockSpec((None, nk, KV_WIDTH), lambda b, i: (b, 0, 0)),
                  pl.BlockSpec((None, nk, KV_WIDTH), lambda b, i: (b, 0, 0)),
                  _const_spec(expand.shape, 2)],
        out_specs=pl.BlockSpec((None, tq, ATTN_WIDTH), lambda b, i: (b, i, 0)),
        scratch_shapes=[pltpu.VMEM((N_KV_HEADS, REP_W, nk), BF16),
                        pltpu.VMEM((N_KV_HEADS, nk, REP_W), BF16)],
        compiler_params=pltpu.CompilerParams(dimension_semantics=("arbitrary", "arbitrary"),
                                             vmem_limit_bytes=VMEM_LIMIT),
        name="attn",
    )(q, k_all, v_all, expand)


def _perm_in_kernel(s_ref, p_ref, o_ref):
    halves = [jnp.dot(s_ref[:, hf * HALF_W:(hf + 1) * HALF_W], p_ref[...], preferred_element_type=F32)
              for hf in range(BLOCK_W // HALF_W)]
    for j in range(GROUPS_PER_BLOCK):
        o_ref[j] = jnp.concatenate([r[:, j * LANES:(j + 1) * LANES] for r in halves], axis=1).astype(o_ref.dtype)


def _perm_in_call(up, pmat, tr):
    rows = up.shape[0]
    return pl.pallas_call(
        _perm_in_kernel,
        out_shape=jax.ShapeDtypeStruct((S5_GROUPS, rows, S5_CW), BF16),
        grid=(N_BLOCKS, rows // tr),
        in_specs=[pl.BlockSpec((tr, BLOCK_W), lambda s, i: (i, s)),
                  _const_spec(pmat.shape, 2)],
        out_specs=pl.BlockSpec((GROUPS_PER_BLOCK, tr, S5_CW), lambda s, i: (s, i, 0)),
        compiler_params=pltpu.CompilerParams(dimension_semantics=("arbitrary", "arbitrary"),
                                             vmem_limit_bytes=VMEM_LIMIT),
        name="perm_in",
    )(up, pmat)


def _perm_out_kernel(y_ref, q_ref, o_ref):
    for hf in range(BLOCK_W // HALF_W):
        ycat = jnp.concatenate([y_ref[j, :, hf * LANES:(hf + 1) * LANES] for j in range(GROUPS_PER_BLOCK)], axis=1)
        o_ref[:, hf * HALF_W:(hf + 1) * HALF_W] = jnp.dot(ycat, q_ref[...],
                                                         preferred_element_type=F32).astype(o_ref.dtype)


def _perm_out_call(yg, qmat, tr):
    rows = yg.shape[1]
    return pl.pallas_call(
        _perm_out_kernel,
        out_shape=jax.ShapeDtypeStruct((rows, N_BLOCKS * BLOCK_W), BF16),
        grid=(N_BLOCKS, rows // tr),
        in_specs=[pl.BlockSpec((GROUPS_PER_BLOCK, tr, S5_CW), lambda s, i: (s, i, 0)),
                  _const_spec(qmat.shape, 2)],
        out_specs=pl.BlockSpec((tr, BLOCK_W), lambda s, i: (i, s)),
        compiler_params=pltpu.CompilerParams(dimension_semantics=("arbitrary", "arbitrary"),
                                             vmem_limit_bytes=VMEM_LIMIT),
        name="perm_out",
    )(yg, qmat)


def _s5_kernel(u_ref, m_ref, sin_ref, g_ref, a_ref, y_ref, inc_ref, hin_ref, *, n_ctx_chunks, n_chunks):
    half = S5_STATE
    gb = u_ref.shape[0]
    for j in range(gb):
        inc_ref[j] = jnp.dot(u_ref[j], sin_ref[j], preferred_element_type=F32)
    a_re = [jnp.broadcast_to(a_ref[j, 0:1, :], (SUBLANES, LANES)) for j in range(gb)]
    a_im = [jnp.broadcast_to(a_ref[j, 1:2, :], (SUBLANES, LANES)) for j in range(gb)]
    is_fwd = lax.broadcasted_iota(jnp.int32, (SUBLANES, LANES), 1) < half

    def step(k, carry):
        pos_b = jnp.where(k < n_ctx_chunks, n_ctx_chunks - 1 - k, n_chunks + n_ctx_chunks - 1 - k)
        rf = pl.multiple_of(k * SUBLANES, SUBLANES)
        rb = pl.multiple_of(pos_b * SUBLANES, SUBLANES)
        out = []
        for j in range(gb):
            h_re, h_im = carry[2 * j], carry[2 * j + 1]
            xf = inc_ref[j, pl.ds(rf, SUBLANES), :]
            xb = inc_ref[j, pl.ds(rb, SUBLANES), :]
            hin_ref[j, pl.ds(rf, SUBLANES), 0:half] = h_re[:, 0:half]
            hin_ref[j, pl.ds(rf, SUBLANES), 2 * half:3 * half] = h_im[:, 0:half]
            hin_ref[j, pl.ds(rb, SUBLANES), half:2 * half] = h_re[:, half:]
            hin_ref[j, pl.ds(rb, SUBLANES), 3 * half:] = h_im[:, half:]
            x_re = jnp.where(is_fwd, xf[:, 0:LANES], xb[:, 0:LANES])
            x_im = jnp.where(is_fwd, xf[:, LANES:], xb[:, LANES:])
            out.append(a_re[j] * h_re - a_im[j] * h_im + x_re)
            out.append(a_re[j] * h_im + a_im[j] * h_re + x_im)
        return tuple(out)

    zero = jnp.zeros((SUBLANES, LANES), F32)
    lax.fori_loop(0, n_chunks, step, (zero,) * (2 * gb))

    r0 = n_ctx_chunks * SUBLANES
    for j in range(gb):
        y = jnp.dot(u_ref[j, r0:, :], m_ref[j], preferred_element_type=F32)
        y = y + jnp.dot(hin_ref[j, r0:, :].astype(BF16), g_ref[j], preferred_element_type=F32)
        y_ref[j] = y.astype(y_ref.dtype)


def _s5_call(ug, m, sin, gmat, a16, n_ctx_chunks, gb):
    groups, rows, _ = ug.shape
    n_chunks = rows // SUBLANES
    out_rows = rows - n_ctx_chunks * SUBLANES
    mat = lambda: pl.BlockSpec((gb, S5_CW, S5_CW), lambda g: (g, 0, 0))
    return pl.pallas_call(
        functools.partial(_s5_kernel, n_ctx_chunks=n_ctx_chunks, n_chunks=n_chunks),
        out_shape=jax.ShapeDtypeStruct((groups, out_rows, S5_CW), BF16),
        grid=(groups // gb,),
        in_specs=[pl.BlockSpec((gb, rows, S5_CW), lambda g: (g, 0, 0)),
                  mat(), mat(), mat(),
                  pl.BlockSpec((gb, 2, LANES), lambda g: (g, 0, 0))],
        out_specs=pl.BlockSpec((gb, out_rows, S5_CW), lambda g: (g, 0, 0)),
        scratch_shapes=[pltpu.VMEM((gb, rows, S5_CW), F32), pltpu.VMEM((gb, rows, S5_CW), F32)],
        compiler_params=pltpu.CompilerParams(dimension_semantics=("arbitrary",),
                                             vmem_limit_bytes=VMEM_LIMIT),
        name="s5",
    )(ug, m, sin, gmat, a16)


def _s5_ops_kernel(*refs):
    for j in range(refs[0].shape[0]):
        _s5_ops_group(*(r.at[j] for r in refs))


def _s5_ops_group(lam_ref, ldt_ref, bt_ref, ct_ref, d_ref, m_ref, sin_ref, g_ref, a_ref):
    T, H = S5_CHUNK, S5_GROUP
    lr = jnp.minimum(lam_ref[0:1, :], -1e-4)
    li = lam_ref[1:2, :]
    dt = jnp.exp(ldt_ref[...])
    taus = lax.broadcasted_iota(jnp.int32, (3 * SUBLANES, LANES), 0).astype(F32)
    mag = jnp.exp(lr * dt * taus)
    pw_r = mag * jnp.cos(li * dt * taus)
    pw_i = mag * jnp.sin(li * dt * taus)
    nr, ni = pw_r[1:2] - 1.0, pw_i[1:2]
    den = lr * lr + li * li
    cf_r = (nr * lr + ni * li) / den
    cf_i = (ni * lr - nr * li) / den
    bb_r = cf_r * bt_ref[0] - cf_i * bt_ref[1]
    bb_i = cf_r * bt_ref[1] + cf_i * bt_ref[0]
    is_fwd = lax.broadcasted_iota(jnp.int32, (H, LANES), 1) < S5_STATE

    def powers(tau_f, tau_b):
        pick = lambda pw, s: jnp.where(is_fwd, jnp.broadcast_to(pw[tau_f(s):tau_f(s) + 1], (H, LANES)),
                                       jnp.broadcast_to(pw[tau_b(s):tau_b(s) + 1], (H, LANES)))
        return (jnp.concatenate([pick(pw_r, s) for s in range(T)], axis=0),
                jnp.concatenate([pick(pw_i, s) for s in range(T)], axis=0))

    tile = lambda a: jnp.concatenate([a] * T, axis=0)
    bbr, bbi, cr, ci = tile(bb_r), tile(bb_i), tile(ct_ref[0]), tile(ct_ref[1])

    er, ei = powers(lambda s: T - 1 - s, lambda s: s)
    sin_ref[:, 0:LANES] = (er * bbr - ei * bbi).astype(sin_ref.dtype)
    sin_ref[:, LANES:] = (er * bbi + ei * bbr).astype(sin_ref.dtype)

    er, ei = powers(lambda t: t + 1, lambda t: T - t)
    gt = jnp.concatenate([er * cr - ei * ci, -(er * ci + ei * cr)], axis=1)
    g_ref[...] = gt.T.astype(g_ref.dtype)

    er, ei = powers(lambda a: a, lambda a: T - 1 - a)
    cp = jnp.concatenate([er * cr - ei * ci, er * ci + ei * cr], axis=1)
    zero = jnp.zeros_like(bb_r)
    lhs = jnp.concatenate([jnp.concatenate([jnp.where(is_fwd, bb_r, zero), jnp.where(is_fwd, -bb_i, zero)], axis=1),
                           jnp.concatenate([jnp.where(is_fwd, zero, bb_r), jnp.where(is_fwd, zero, -bb_i)], axis=1)],
                          axis=0)
    kr = lax.dot_general(lhs, cp, (((1,), (1,)), ((), ())), preferred_element_type=F32,
                         precision=lax.Precision.HIGHEST)
    pad = jnp.zeros((H, S5_CW), F32)
    wide_f = jnp.concatenate([pad, kr[0:H]], axis=1)
    wide_b = jnp.concatenate([kr[H:], pad], axis=1)
    lane = lax.broadcasted_iota(jnp.int32, (H, S5_CW), 1)
    row = lax.broadcasted_iota(jnp.int32, (H, S5_CW), 0)
    skip = jnp.broadcast_to(d_ref[...], (H, S5_CW))
    for s in range(T):
        blk_f = pltpu.roll(wide_f, H * s, 1)[:, S5_CW:] if s else wide_f[:, S5_CW:]
        shift_b = (2 * S5_CW - H * (T - 1 - s)) % (2 * S5_CW)
        blk_b = (pltpu.roll(wide_b, shift_b, 1) if shift_b else wide_b)[:, :S5_CW]
        diag = jnp.where(lane == H * s + row, skip, 0.0)
        m_ref[H * s:H * (s + 1), :] = (blk_f + blk_b + diag).astype(m_ref.dtype)
    a_ref[0:1, :] = pw_r[T:T + 1]
    a_ref[1:2, :] = pw_i[T:T + 1]


def _s5_ops_call(lam, ldt, bt, ct, dt_tiled):
    groups = lam.shape[0]
    gb = 8
    mat = lambda: pl.BlockSpec((gb, S5_CW, S5_CW), lambda g: (g, 0, 0))
    vec = lambda a: pl.BlockSpec((gb,) + a.shape[1:], lambda g: (g,) + (0,) * (a.ndim - 1))
    mshape = jax.ShapeDtypeStruct((groups, S5_CW, S5_CW), BF16)
    return pl.pallas_call(
        _s5_ops_kernel,
        out_shape=(mshape, mshape, mshape, jax.ShapeDtypeStruct((groups, 2, LANES), F32)),
        grid=(groups // gb,),
        in_specs=[vec(lam), vec(ldt), vec(bt), vec(ct), vec(dt_tiled)],
        out_specs=(mat(), mat(), mat(), pl.BlockSpec((gb, 2, LANES), lambda g: (g, 0, 0))),
        compiler_params=pltpu.CompilerParams(dimension_semantics=("arbitrary",)),
        name="s5_ops",
    )(lam, ldt, bt, ct, dt_tiled)


def _s5_operators(lam_re, lam_im, log_dt, b_re, b_im, c_re, c_im, d_skip):
    G, P, H = S5_GROUPS, S5_STATE, S5_GROUP
    lam = jnp.stack([lam_re, lam_im]).astype(F32).transpose(2, 0, 1, 3).reshape(G, 2, 2 * P)
    ldt = jnp.repeat(log_dt.astype(F32).T, P, axis=1).reshape(G, 1, 2 * P)
    bt = jnp.stack([b_re, b_im]).astype(F32).transpose(2, 0, 4, 1, 3).reshape(G, 2, H, 2 * P)
    ct = jnp.stack([c_re, c_im]).astype(F32).transpose(2, 0, 3, 1, 4).reshape(G, 2, H, 2 * P)
    dt_tiled = jnp.tile(d_skip.astype(F32).reshape(G, 1, H), (1, 1, S5_CHUNK))
    return _s5_ops_call(lam, ldt, bt, ct, dt_tiled)


def _final_kernel(x_ref, mod_ref, ng_ref, wg_ref, ya_ref, yp_ref, permt_ref, wglu_ref, bglu_ref, wa_ref, wb_ref,
                  wo_ref, fg_ref, o_ref):
    bsz, tt, _ = x_ref.shape
    rows = bsz * tt
    x3 = x_ref[...]
    xn = _modulated_norm(x3, mod_ref, ng_ref).reshape(rows, D_MODEL).astype(BF16)
    gates = jnp.dot(xn, wg_ref[...], preferred_element_type=F32)
    o1, o2, o3 = ATTN_WIDTH, ATTN_WIDTH + S5_WIDTH, ATTN_WIDTH + S5_WIDTH + D_MODEL
    prow = PERM_TOK // S5_CHUNK * bsz
    parts = []
    for part in range(tt // PERM_TOK):
        slabs = [jnp.concatenate([yp_ref[part * prow:(part + 1) * prow,
                                         blk * BLOCK_W + t * LANES: blk * BLOCK_W + (t + 1) * LANES]
                                  for blk in range(N_BLOCKS)], axis=1) for t in range(S5_CHUNK)]
        yb = jnp.dot(permt_ref[...], jnp.concatenate(slabs, axis=0), preferred_element_type=F32)
        parts.append(yb.reshape(bsz, PERM_TOK, S5_WIDTH))
    y = (parts[0] if len(parts) == 1 else jnp.concatenate(parts, axis=1)).reshape(rows, S5_WIDTH)
    z = y * (0.5 * (1.0 + jnp.tanh(0.7978845608028654 * (y + 0.044715 * (y * y * y)))))
    zz = z * jax.nn.sigmoid(jnp.dot(z.astype(BF16), wglu_ref[...], preferred_element_type=F32) + bglu_ref[...])
    ya = ya_ref[...].reshape(rows, ATTN_WIDTH).astype(F32)
    ta = (ya * _silu(gates[:, 0:o1])).astype(BF16)
    tb = (zz * _silu(gates[:, o1:o2])).astype(BF16)
    pa = jnp.dot(ta, wa_ref[...], preferred_element_type=F32)
    pb = jnp.dot(tb, wb_ref[...], preferred_element_type=F32)
    mix = jax.nn.sigmoid(gates[:, o2:o3]) * pa + jax.nn.sigmoid(gates[:, o3:]) * pb
    o = jnp.dot(mix.astype(BF16), wo_ref[...], preferred_element_type=F32).reshape(bsz, tt, D_MODEL)
    h = x3 + mod_ref[:, :, 2 * D_MODEL:] * o
    ms = jnp.mean(h * h, axis=-1, keepdims=True)
    o_ref[...] = h * lax.rsqrt(ms + EPS) * fg_ref[...]


def _final_call(x, mod3, ng, wg, ya, yp, permt, wglu, bglu, wa, wb, wo, fg):
    bsz, n, _ = x.shape
    tt = TOK_TILE
    up_rows = tt // S5_CHUNK * bsz
    tok = lambda width: pl.BlockSpec((bsz, tt, width), lambda i: (0, i, 0))
    c = lambda shape: _const_spec(shape, 1)
    return pl.pallas_call(
        _final_kernel,
        out_shape=jax.ShapeDtypeStruct((bsz, n, D_MODEL), F32),
        grid=(n // tt,),
        in_specs=[tok(D_MODEL),
                  pl.BlockSpec((bsz, 1, 3 * D_MODEL), lambda i: (0, 0, 0)),
                  c((1, D_MODEL)), c(wg.shape),
                  tok(ATTN_WIDTH),
                  pl.BlockSpec((up_rows, N_BLOCKS * BLOCK_W), lambda i: (i, 0)),
                  c(permt.shape), c(wglu.shape), c((1, S5_WIDTH)), c(wa.shape), c(wb.shape), c(wo.shape),
                  c((1, D_MODEL))],
        out_specs=tok(D_MODEL),
        compiler_params=pltpu.CompilerParams(dimension_semantics=("arbitrary",), vmem_limit_bytes=VMEM_LIMIT),
        name="final",
    )(x, mod3, ng, wg, ya, yp, permt, wglu, bglu, wa, wb, wo, fg)


def _rope_tables(n):
    rows = n // GRID_W
    row_ids = np.repeat(np.arange(rows, dtype=np.float64), GRID_W)
    col_ids = np.tile(np.arange(GRID_W, dtype=np.float64), rows)
    freqs = ROPE_THETA ** (-np.arange(ROPE_FREQS, dtype=np.float64) / ROPE_FREQS)
    ang_r, ang_c = row_ids[:, None] * freqs, col_ids[:, None] * freqs
    cos = np.concatenate([np.cos(ang_r)] * 2 + [np.cos(ang_c)] * 2, axis=1)
    sin = np.concatenate([-np.sin(ang_r), np.sin(ang_r), -np.sin(ang_c), np.sin(ang_c)], axis=1)
    reps = LANES // HEAD_DIM
    return (jnp.asarray(np.tile(cos, (1, reps)).astype(np.float32)),
            jnp.asarray(np.tile(sin, (1, reps)).astype(np.float32)))


def _one_hot(match):
    return jnp.asarray(np.ascontiguousarray(match).astype(BF16))


def _block_ones(width):
    idx = np.arange(width) // HEAD_DIM
    return _one_hot(idx[:, None] == idx[None, :])


def _row_perm(bsz, tt, transpose=False):
    chunks = tt // S5_CHUNK
    r = np.arange(bsz * tt)
    t, pc, b = r // (chunks * bsz), (r // bsz) % chunks, r % bsz
    src = b * tt + pc * S5_CHUNK + t
    match = src[:, None] == np.arange(bsz * tt)[None, :]
    return _one_hot(match.T if transpose else match)


def _lane_perm(transpose=False):
    r = np.arange(HALF_W)
    t, j, h = r // LANES, (r % LANES) // S5_GROUP, r % S5_GROUP
    dst = j * LANES + t * S5_GROUP + h
    match = dst[:, None] == np.arange(HALF_W)[None, :]
    return _one_hot(match.T if transpose else match)


def _kv_expand():
    col = np.arange(N_KV_HEADS * REP_W)
    src = (col // REP_W) * HEAD_DIM + col % HEAD_DIM
    return _one_hot(np.arange(KV_WIDTH)[:, None] == src[None, :])


def kernel(x, c, ctx, c_ctx, norm_g, w_ada, b_ada, w_in, q_norm_g, k_norm_g, s5_lam_re, s5_lam_im, s5_log_dt,
           s5_b_re, s5_b_im, s5_c_re, s5_c_im, s5_d, w_glu, b_glu, w_branch_attn, w_branch_s5, w_out,
           final_norm_g):
    assert w_in.shape[0] == 1, "single-layer block"
    bsz, n, _ = x.shape
    n_ctx = ctx.shape[1]
    assert n % TOK_TILE == 0 and n_ctx % TOK_TILE == 0 and bsz == SUBLANES

    ada_rows = 2 * SUBLANES
    cc = jnp.concatenate([c, c_ctx[None], jnp.zeros((ada_rows - bsz - 1, D_MODEL), F32)], axis=0)
    mod = _ada_call(cc, w_ada[0], b_ada[0][None])
    mod3 = mod[:bsz].reshape(bsz, 1, 3 * D_MODEL)
    mod_ctx3 = mod[bsz:bsz + 1].reshape(1, 1, 3 * D_MODEL)

    offs = [0]
    for s in IN_SIZES:
        offs.append(offs[-1] + s)
    w_bf = w_in[0].astype(BF16)
    w_pre = jnp.concatenate([w_bf[:, offs[0]:offs[3]], w_bf[:, offs[4]:offs[5]]], axis=1)
    w_gates = jnp.concatenate([w_bf[:, offs[3]:offs[4]], w_bf[:, offs[5]:]], axis=1)

    ng = norm_g[0][None]
    qg = jnp.tile(q_norm_g[0], N_HEADS)[None]
    kg = jnp.tile(k_norm_g[0], N_KV_HEADS)[None]
    onesq, onesk = _block_ones(ATTN_WIDTH), _block_ones(KV_WIDTH)
    cos, sin = _rope_tables(n)
    row_perm = _row_perm(bsz, PERM_TOK)
    lane_perm = _lane_perm()

    q, k_all, v_all, up = _pre_call(x, ctx, mod3, mod_ctx3, ng, w_pre, onesq, onesk, qg, kg, cos, sin, row_perm)

    y_attn = _attn_call(q, k_all, v_all, _kv_expand(), tq=1024)

    ug = _perm_in_call(up, lane_perm, tr=576)
    m, s_in, gmat, a16 = _s5_operators(s5_lam_re[0], s5_lam_im[0], s5_log_dt[0], s5_b_re[0], s5_b_im[0],
                                       s5_c_re[0], s5_c_im[0], s5_d[0])
    yg = _s5_call(ug, m, s_in, gmat, a16, n_ctx // S5_CHUNK, gb=GROUPS_PER_BLOCK)
    yp = _perm_out_call(yg, _lane_perm(transpose=True), tr=512)

    return _final_call(x, mod3, ng, w_gates, y_attn, yp, _row_perm(bsz, PERM_TOK, transpose=True),
                       w_glu[0].astype(BF16), b_glu[0][None],
                       w_branch_attn[0].astype(BF16), w_branch_s5[0].astype(BF16), w_out[0].astype(BF16),
                       final_norm_g[None])
```

```python
import functools

import numpy as np
import jax
import jax.numpy as jnp
from jax import lax
from jax.experimental import pallas as pl
from jax.experimental.pallas import tpu as pltpu

D_MODEL = 1024
GRID_W = 64
N_HEADS = 8
N_KV_HEADS = 2
HEAD_DIM = 64
GQA_REP = N_HEADS // N_KV_HEADS
ATTN_WIDTH = N_HEADS * HEAD_DIM
KV_WIDTH = N_KV_HEADS * HEAD_DIM
ATTN_SCALE = HEAD_DIM ** -0.5
ROPE_THETA = 10000.0
ROPE_FREQS = HEAD_DIM // 4
S5_WIDTH = 512
S5_GROUP = 16
S5_GROUPS = S5_WIDTH // S5_GROUP
S5_STATE = 64
EPS = 1e-6
IN_SIZES = (ATTN_WIDTH, KV_WIDTH, KV_WIDTH, ATTN_WIDTH, S5_WIDTH, S5_WIDTH, D_MODEL, D_MODEL)

LANES = 128
SUBLANES = 8
S5_CHUNK = 16
S5_CW = S5_CHUNK * S5_GROUP
GROUPS_PER_BLOCK = LANES // S5_GROUP
N_BLOCKS = S5_WIDTH // LANES
BLOCK_W = S5_CHUNK * LANES
HALF_W = BLOCK_W // 2
TOK_TILE = 128
PERM_TOK = 64
REP_W = GQA_REP * HEAD_DIM
VMEM_LIMIT = 56 * 1024 * 1024

F32 = jnp.float32
BF16 = jnp.bfloat16


def _silu(t):
    return t * jax.nn.sigmoid(t)


def _modulated_norm(x3, mod_ref, ng_ref):
    ms = jnp.mean(x3 * x3, axis=-1, keepdims=True)
    y = x3 * lax.rsqrt(ms + EPS) * ng_ref[...]
    return y * (1.0 + mod_ref[:, :, D_MODEL:2 * D_MODEL]) + mod_ref[:, :, 0:D_MODEL]


def _head_rmsnorm(t, ones_ref, g_ref):
    ss = jnp.dot((t * t).astype(BF16), ones_ref[...], preferred_element_type=F32)
    return t * lax.rsqrt(ss * (1.0 / HEAD_DIM) + EPS) * g_ref[...]


def _rope(t, cos, sin_signed):
    rows = t.shape[0]
    lane = lax.broadcasted_iota(jnp.int32, (rows, LANES), 1)
    first = (lane & ROPE_FREQS) == 0
    outs = []
    for j in range(t.shape[1] // LANES):
        blk = t[:, j * LANES:(j + 1) * LANES]
        partner = jnp.where(first, pltpu.roll(blk, LANES - ROPE_FREQS, 1), pltpu.roll(blk, ROPE_FREQS, 1))
        outs.append(blk * cos + partner * sin_signed)
    return outs[0] if len(outs) == 1 else jnp.concatenate(outs, axis=1)


def _const_spec(shape, grid_rank):
    zeros = (0,) * len(shape)
    return pl.BlockSpec(shape, lambda *_: zeros, pipeline_mode=pl.Buffered(1))


def _ada_kernel(c_ref, w_ref, b_ref, o_ref):
    s = _silu(c_ref[...])
    w = w_ref[...]
    s_hi, w_hi = s.astype(BF16), w.astype(BF16)
    s_lo = (s - s_hi.astype(F32)).astype(BF16)
    w_lo = (w - w_hi.astype(F32)).astype(BF16)
    dot = functools.partial(jnp.dot, preferred_element_type=F32)
    o_ref[...] = dot(s_hi, w_hi) + dot(s_lo, w_hi) + dot(s_hi, w_lo) + b_ref[...]


def _ada_call(cc, w, b):
    rows, n = cc.shape[0], w.shape[1]
    tn = 1024
    return pl.pallas_call(
        _ada_kernel,
        out_shape=jax.ShapeDtypeStruct((rows, n), F32),
        grid=(n // tn,),
        in_specs=[pl.BlockSpec((rows, D_MODEL), lambda j: (0, 0)),
                  pl.BlockSpec((D_MODEL, tn), lambda j: (0, j)),
                  pl.BlockSpec((1, tn), lambda j: (0, j))],
        out_specs=pl.BlockSpec((rows, tn), lambda j: (0, j)),
        compiler_params=pltpu.CompilerParams(dimension_semantics=("arbitrary",)),
        name="ada",
    )(cc, w, b)


def _store_chunk_major(u, perm_ref, up_out, bsz):
    tt = u.shape[0] // bsz
    u3 = u.astype(BF16).reshape(bsz, tt, S5_WIDTH)
    rows = PERM_TOK // S5_CHUNK * bsz
    for part in range(tt // PERM_TOK):
        up = u3[:, part * PERM_TOK:(part + 1) * PERM_TOK, :].reshape(bsz * PERM_TOK, S5_WIDTH)
        r = jnp.dot(perm_ref[...], up, preferred_element_type=F32).astype(BF16)
        for t in range(S5_CHUNK):
            for blk in range(N_BLOCKS):
                up_out[part * rows:(part + 1) * rows, blk * BLOCK_W + t * LANES: blk * BLOCK_W + (t + 1) * LANES] = (
                    r[t * rows:(t + 1) * rows, blk * LANES:(blk + 1) * LANES])


def _pre_kernel(x_ref, c_ref, mod_ref, modc_ref, ng_ref, w_ref, onesq_ref, onesk_ref, qg_ref, kg_ref, cos_ref, sin_ref,
                perm_ref, q_out, k_out, v_out, up_out, *, ctx_steps):
    bsz, tt, _ = x_ref.shape
    o1, o2, o3 = ATTN_WIDTH, ATTN_WIDTH + KV_WIDTH, ATTN_WIDTH + 2 * KV_WIDTH
    step = pl.program_id(0)

    @pl.when(step < ctx_steps)
    def _():
        xn = _modulated_norm(c_ref[...], modc_ref, ng_ref).reshape(bsz * tt, D_MODEL).astype(BF16)
        p = jnp.dot(xn, w_ref[:, o1:], preferred_element_type=F32)
        k = _head_rmsnorm(p[:, 0:KV_WIDTH], onesk_ref, kg_ref)
        k_out[...] = k.astype(BF16).reshape(bsz, tt, KV_WIDTH)
        v_out[...] = p[:, KV_WIDTH:2 * KV_WIDTH].astype(BF16).reshape(bsz, tt, KV_WIDTH)
        _store_chunk_major(p[:, 2 * KV_WIDTH:], perm_ref, up_out, bsz)

    @pl.when(step >= ctx_steps)
    def _():
        xn = _modulated_norm(x_ref[...], mod_ref, ng_ref).reshape(bsz * tt, D_MODEL).astype(BF16)
        p = jnp.dot(xn, w_ref[...], preferred_element_type=F32)
        cos = jnp.concatenate([cos_ref[...]] * bsz, axis=0)
        sin = jnp.concatenate([sin_ref[...]] * bsz, axis=0)
        q = _rope(_head_rmsnorm(p[:, 0:o1], onesq_ref, qg_ref), cos, sin)
        k = _rope(_head_rmsnorm(p[:, o1:o2], onesk_ref, kg_ref), cos, sin)
        q_out[...] = (q * ATTN_SCALE).astype(BF16).reshape(bsz, tt, ATTN_WIDTH)
        k_out[...] = k.astype(BF16).reshape(bsz, tt, KV_WIDTH)
        v_out[...] = p[:, o2:o3].astype(BF16).reshape(bsz, tt, KV_WIDTH)
        _store_chunk_major(p[:, o3:], perm_ref, up_out, bsz)


def _pre_call(x, ctx, mod3, mod_ctx3, ng, w, onesq, onesk, qg, kg, cos, sin, perm):
    bsz, n, _ = x.shape
    n_ctx = ctx.shape[1]
    tt = TOK_TILE
    up_rows = tt // S5_CHUNK * bsz
    ctx_steps = n_ctx // tt
    lat = lambda i: jnp.maximum(i - ctx_steps, 0)
    c = lambda shape: _const_spec(shape, 1)
    return pl.pallas_call(
        functools.partial(_pre_kernel, ctx_steps=ctx_steps),
        out_shape=(jax.ShapeDtypeStruct((bsz, n, ATTN_WIDTH), BF16),
                   jax.ShapeDtypeStruct((bsz, n_ctx + n, KV_WIDTH), BF16),
                   jax.ShapeDtypeStruct((bsz, n_ctx + n, KV_WIDTH), BF16),
                   jax.ShapeDtypeStruct(((n_ctx + n) // S5_CHUNK * bsz, N_BLOCKS * BLOCK_W), BF16)),
        grid=(ctx_steps + n // tt,),
        in_specs=[pl.BlockSpec((bsz, tt, D_MODEL), lambda i: (0, lat(i), 0)),
                  pl.BlockSpec((bsz, tt, D_MODEL), lambda i: (0, jnp.minimum(i, ctx_steps - 1), 0)),
                  c((bsz, 1, 3 * D_MODEL)), c((1, 1, 3 * D_MODEL)),
                  c((1, D_MODEL)), c(w.shape), c(onesq.shape), c(onesk.shape),
                  c((1, ATTN_WIDTH)), c((1, KV_WIDTH)),
                  pl.BlockSpec((tt, LANES), lambda i: (lat(i), 0)),
                  pl.BlockSpec((tt, LANES), lambda i: (lat(i), 0)),
                  c(perm.shape)],
        out_specs=(pl.BlockSpec((bsz, tt, ATTN_WIDTH), lambda i: (0, lat(i), 0)),
                   pl.BlockSpec((bsz, tt, KV_WIDTH), lambda i: (0, i, 0)),
                   pl.BlockSpec((bsz, tt, KV_WIDTH), lambda i: (0, i, 0)),
                   pl.BlockSpec((up_rows, N_BLOCKS * BLOCK_W), lambda i: (i, 0))),
        compiler_params=pltpu.CompilerParams(dimension_semantics=("arbitrary",), vmem_limit_bytes=VMEM_LIMIT),
        name="pre",
    )(x, ctx, mod3, mod_ctx3, ng, w, onesq, onesk, qg, kg, cos, sin, perm)


FINAL_BATCHES = 2
ATTN_ROWS = 256


def _attn_kernel(q_ref, k_ref, v_ref, e_ref, o_ref, k4t_ref, v4_ref):
    @pl.when(pl.program_id(1) == 0)
    def _():
        kt = k_ref[...].astype(F32).T.astype(BF16)
        v4 = jnp.dot(v_ref[...], e_ref[...], preferred_element_type=F32)
        for g in range(N_KV_HEADS):
            k4t_ref[g] = jnp.concatenate([kt[g * HEAD_DIM:(g + 1) * HEAD_DIM]] * GQA_REP, axis=0)
            v4_ref[g] = v4[:, g * REP_W:(g + 1) * REP_W].astype(BF16)

    lane = lax.broadcasted_iota(jnp.int32, (ATTN_ROWS, REP_W), 1)
    for part in range(q_ref.shape[0] // ATTN_ROWS):
        rows = slice(part * ATTN_ROWS, (part + 1) * ATTN_ROWS)
        for g in range(N_KV_HEADS):
            qg = q_ref[rows, g * REP_W:(g + 1) * REP_W]
            acc = jnp.zeros((ATTN_ROWS, REP_W), F32)
            for r in range(GQA_REP):
                in_head = (lane >= r * HEAD_DIM) & (lane < (r + 1) * HEAD_DIM)
                qr = jnp.where(in_head, qg, jnp.zeros_like(qg))
                s = jnp.dot(qr, k4t_ref[g], preferred_element_type=F32)
                m = jnp.max(s, axis=1, keepdims=True)
                p = jnp.exp(s - m)
                l = jnp.sum(p, axis=1, keepdims=True)
                o = jnp.dot(p.astype(BF16), v4_ref[g], preferred_element_type=F32)
                acc = jnp.where(in_head, o / l, acc)
            o_ref[rows, g * REP_W:(g + 1) * REP_W] = acc.astype(o_ref.dtype)


def _attn_call(q, k_all, v_all, expand, tq):
    bsz, n, _ = q.shape
    nk = k_all.shape[1]
    return pl.pallas_call(
        _attn_kernel,
        out_shape=jax.ShapeDtypeStruct((bsz, n, ATTN_WIDTH), BF16),
        grid=(bsz, n // tq),
        in_specs=[pl.BlockSpec((None, tq, ATTN_WIDTH), lambda b, i: (b, i, 0)),
                  pl.BlockSpec((None, nk, KV_WIDTH), lambda b, i: (b, 0, 0)),
                  pl.BlockSpec((None, nk, KV_WIDTH), lambda b, i: (b, 0, 0)),
                  _const_spec(expand.shape, 2)],
        out_specs=pl.BlockSpec((None, tq, ATTN_WIDTH), lambda b, i: (b, i, 0)),
        scratch_shapes=[pltpu.VMEM((N_KV_HEADS, REP_W, nk), BF16),
                        pltpu.VMEM((N_KV_HEADS, nk, REP_W), BF16)],
        compiler_params=pltpu.CompilerParams(dimension_semantics=("arbitrary", "arbitrary"),
                                             vmem_limit_bytes=VMEM_LIMIT),
        name="attn",
    )(q, k_all, v_all, expand)


def _perm_in_kernel(s_ref, p_ref, o_ref):
    halves = [jnp.dot(s_ref[:, hf * HALF_W:(hf + 1) * HALF_W], p_ref[...], preferred_element_type=F32)
              for hf in range(BLOCK_W // HALF_W)]
    for j in range(GROUPS_PER_BLOCK):
        o_ref[j] = jnp.concatenate([r[:, j * LANES:(j + 1) * LANES] for r in halves], axis=1).astype(o_ref.dtype)


def _perm_in_call(up, pmat, tr):
    rows = up.shape[0]
    return pl.pallas_call(
        _perm_in_kernel,
        out_shape=jax.ShapeDtypeStruct((S5_GROUPS, rows, S5_CW), BF16),
        grid=(N_BLOCKS, rows // tr),
        in_specs=[pl.BlockSpec((tr, BLOCK_W), lambda s, i: (i, s)),
                  _const_spec(pmat.shape, 2)],
        out_specs=pl.BlockSpec((GROUPS_PER_BLOCK, tr, S5_CW), lambda s, i: (s, i, 0)),
        compiler_params=pltpu.CompilerParams(dimension_semantics=("arbitrary", "arbitrary"),
                                             vmem_limit_bytes=VMEM_LIMIT),
        name="perm_in",
    )(up, pmat)


def _perm_out_kernel(y_ref, q_ref, o_ref):
    for hf in range(BLOCK_W // HALF_W):
        ycat = jnp.concatenate([y_ref[j, :, hf * LANES:(hf + 1) * LANES] for j in range(GROUPS_PER_BLOCK)], axis=1)
        o_ref[:, hf * HALF_W:(hf + 1) * HALF_W] = jnp.dot(ycat, q_ref[...],
                                                         preferred_element_type=F32).astype(o_ref.dtype)


def _perm_out_call(yg, qmat, tr):
    rows = yg.shape[1]
    return pl.pallas_call(
        _perm_out_kernel,
        out_shape=jax.ShapeDtypeStruct((rows, N_BLOCKS * BLOCK_W), BF16),
        grid=(N_BLOCKS, rows // tr),
        in_specs=[pl.BlockSpec((GROUPS_PER_BLOCK, tr, S5_CW), lambda s, i: (s, i, 0)),
                  _const_spec(qmat.shape, 2)],
        out_specs=pl.BlockSpec((tr, BLOCK_W), lambda s, i: (i, s)),
        compiler_params=pltpu.CompilerParams(dimension_semantics=("arbitrary", "arbitrary"),
                                             vmem_limit_bytes=VMEM_LIMIT),
        name="perm_out",
    )(yg, qmat)


def _s5_kernel(u_ref, m_ref, sin_ref, g_ref, a_ref, y_ref, inc_ref, hin_ref, *, n_ctx_chunks, n_chunks):
    half = S5_STATE
    gb = u_ref.shape[0]
    for j in range(gb):
        inc_ref[j] = jnp.dot(u_ref[j], sin_ref[j], preferred_element_type=F32)
    a_re = [jnp.broadcast_to(a_ref[j, 0:1, :], (SUBLANES, LANES)) for j in range(gb)]
    a_im = [jnp.broadcast_to(a_ref[j, 1:2, :], (SUBLANES, LANES)) for j in range(gb)]
    is_fwd = lax.broadcasted_iota(jnp.int32, (SUBLANES, LANES), 1) < half

    def step(k, carry):
        pos_b = jnp.where(k < n_ctx_chunks, n_ctx_chunks - 1 - k, n_chunks + n_ctx_chunks - 1 - k)
        rf = pl.multiple_of(k * SUBLANES, SUBLANES)
        rb = pl.multiple_of(pos_b * SUBLANES, SUBLANES)
        out = []
        for j in range(gb):
            h_re, h_im = carry[2 * j], carry[2 * j + 1]
            xf = inc_ref[j, pl.ds(rf, SUBLANES), :]
            xb = inc_ref[j, pl.ds(rb, SUBLANES), :]
            hin_ref[j, pl.ds(rf, SUBLANES), 0:half] = h_re[:, 0:half]
            hin_ref[j, pl.ds(rf, SUBLANES), 2 * half:3 * half] = h_im[:, 0:half]
            hin_ref[j, pl.ds(rb, SUBLANES), half:2 * half] = h_re[:, half:]
            hin_ref[j, pl.ds(rb, SUBLANES), 3 * half:] = h_im[:, half:]
            x_re = jnp.where(is_fwd, xf[:, 0:LANES], xb[:, 0:LANES])
            x_im = jnp.where(is_fwd, xf[:, LANES:], xb[:, LANES:])
            out.append(a_re[j] * h_re - a_im[j] * h_im + x_re)
            out.append(a_re[j] * h_im + a_im[j] * h_re + x_im)
        return tuple(out)

    zero = jnp.zeros((SUBLANES, LANES), F32)
    lax.fori_loop(0, n_chunks, step, (zero,) * (2 * gb))

    r0 = n_ctx_chunks * SUBLANES
    for j in range(gb):
        y = jnp.dot(u_ref[j, r0:, :], m_ref[j], preferred_element_type=F32)
        y = y + jnp.dot(hin_ref[j, r0:, :].astype(BF16), g_ref[j], preferred_element_type=F32)
        y_ref[j] = y.astype(y_ref.dtype)


def _s5_call(ug, m, sin, gmat, a16, n_ctx_chunks, gb):
    groups, rows, _ = ug.shape
    n_chunks = rows // SUBLANES
    out_rows = rows - n_ctx_chunks * SUBLANES
    mat = lambda: pl.BlockSpec((gb, S5_CW, S5_CW), lambda g: (g, 0, 0))
    return pl.pallas_call(
        functools.partial(_s5_kernel, n_ctx_chunks=n_ctx_chunks, n_chunks=n_chunks),
        out_shape=jax.ShapeDtypeStruct((groups, out_rows, S5_CW), BF16),
        grid=(groups // gb,),
        in_specs=[pl.BlockSpec((gb, rows, S5_CW), lambda g: (g, 0, 0)),
                  mat(), mat(), mat(),
                  pl.BlockSpec((gb, 2, LANES), lambda g: (g, 0, 0))],
        out_specs=pl.BlockSpec((gb, out_rows, S5_CW), lambda g: (g, 0, 0)),
        scratch_shapes=[pltpu.VMEM((gb, rows, S5_CW), F32), pltpu.VMEM((gb, rows, S5_CW), F32)],
        compiler_params=pltpu.CompilerParams(dimension_semantics=("arbitrary",),
                                             vmem_limit_bytes=VMEM_LIMIT),
        name="s5",
    )(ug, m, sin, gmat, a16)


def _s5_ops_kernel(*refs):
    for j in range(refs[0].shape[0]):
        _s5_ops_group(*(r.at[j] for r in refs))


def _s5_ops_group(lam_ref, ldt_ref, bt_ref, ct_ref, d_ref, m_ref, sin_ref, g_ref, a_ref):
    T, H = S5_CHUNK, S5_GROUP
    lr = jnp.minimum(lam_ref[0:1, :], -1e-4)
    li = lam_ref[1:2, :]
    dt = jnp.exp(ldt_ref[...])
    taus = lax.broadcasted_iota(jnp.int32, (3 * SUBLANES, LANES), 0).astype(F32)
    mag = jnp.exp(lr * dt * taus)
    pw_r = mag * jnp.cos(li * dt * taus)
    pw_i = mag * jnp.sin(li * dt * taus)
    nr, ni = pw_r[1:2] - 1.0, pw_i[1:2]
    den = lr * lr + li * li
    cf_r = (nr * lr + ni * li) / den
    cf_i = (ni * lr - nr * li) / den
    bb_r = cf_r * bt_ref[0] - cf_i * bt_ref[1]
    bb_i = cf_r * bt_ref[1] + cf_i * bt_ref[0]
    is_fwd = lax.broadcasted_iota(jnp.int32, (H, LANES), 1) < S5_STATE

    def powers(tau_f, tau_b):
        pick = lambda pw, s: jnp.where(is_fwd, jnp.broadcast_to(pw[tau_f(s):tau_f(s) + 1], (H, LANES)),
                                       jnp.broadcast_to(pw[tau_b(s):tau_b(s) + 1], (H, LANES)))
        return (jnp.concatenate([pick(pw_r, s) for s in range(T)], axis=0),
                jnp.concatenate([pick(pw_i, s) for s in range(T)], axis=0))

    tile = lambda a: jnp.concatenate([a] * T, axis=0)
    bbr, bbi, cr, ci = tile(bb_r), tile(bb_i), tile(ct_ref[0]), tile(ct_ref[1])

    er, ei = powers(lambda s: T - 1 - s, lambda s: s)
    sin_ref[:, 0:LANES] = (er * bbr - ei * bbi).astype(sin_ref.dtype)
    sin_ref[:, LANES:] = (er * bbi + ei * bbr).astype(sin_ref.dtype)

    er, ei = powers(lambda t: t + 1, lambda t: T - t)
    gt = jnp.concatenate([er * cr - ei * ci, -(er * ci + ei * cr)], axis=1)
    g_ref[...] = gt.T.astype(g_ref.dtype)

    er, ei = powers(lambda a: a, lambda a: T - 1 - a)
    cp = jnp.concatenate([er * cr - ei * ci, er * ci + ei * cr], axis=1)
    zero = jnp.zeros_like(bb_r)
    lhs = jnp.concatenate([jnp.concatenate([jnp.where(is_fwd, bb_r, zero), jnp.where(is_fwd, -bb_i, zero)], axis=1),
                           jnp.concatenate([jnp.where(is_fwd, zero, bb_r), jnp.where(is_fwd, zero, -bb_i)], axis=1)],
                          axis=0)
    kr = lax.dot_general(lhs, cp, (((1,), (1,)), ((), ())), preferred_element_type=F32,
                         precision=lax.Precision.HIGHEST)
    pad = jnp.zeros((H, S5_CW), F32)
    wide_f = jnp.concatenate([pad, kr[0:H]], axis=1)
    wide_b = jnp.concatenate([kr[H:], pad], axis=1)
    lane = lax.broadcasted_iota(jnp.int32, (H, S5_CW), 1)
    row = lax.broadcasted_iota(jnp.int32, (H, S5_CW), 0)
    skip = jnp.broadcast_to(d_ref[...], (H, S5_CW))
    for s in range(T):
        blk_f = pltpu.roll(wide_f, H * s, 1)[:, S5_CW:] if s else wide_f[:, S5_CW:]
        shift_b = (2 * S5_CW - H * (T - 1 - s)) % (2 * S5_CW)
        blk_b = (pltpu.roll(wide_b, shift_b, 1) if shift_b else wide_b)[:, :S5_CW]
        diag = jnp.where(lane == H * s + row, skip, 0.0)
        m_ref[H * s:H * (s + 1), :] = (blk_f + blk_b + diag).astype(m_ref.dtype)
    a_ref[0:1, :] = pw_r[T:T + 1]
    a_ref[1:2, :] = pw_i[T:T + 1]


def _s5_ops_call(lam, ldt, bt, ct, dt_tiled):
    groups = lam.shape[0]
    gb = 8
    mat = lambda: pl.BlockSpec((gb, S5_CW, S5_CW), lambda g: (g, 0, 0))
    vec = lambda a: pl.BlockSpec((gb,) + a.shape[1:], lambda g: (g,) + (0,) * (a.ndim - 1))
    mshape = jax.ShapeDtypeStruct((groups, S5_CW, S5_CW), BF16)
    return pl.pallas_call(
        _s5_ops_kernel,
        out_shape=(mshape, mshape, mshape, jax.ShapeDtypeStruct((groups, 2, LANES), F32)),
        grid=(groups // gb,),
        in_specs=[vec(lam), vec(ldt), vec(bt), vec(ct), vec(dt_tiled)],
        out_specs=(mat(), mat(), mat(), pl.BlockSpec((gb, 2, LANES), lambda g: (g, 0, 0))),
        compiler_params=pltpu.CompilerParams(dimension_semantics=("arbitrary",)),
        name="s5_ops",
    )(lam, ldt, bt, ct, dt_tiled)


def _s5_operators(lam_re, lam_im, log_dt, b_re, b_im, c_re, c_im, d_skip):
    G, P, H = S5_GROUPS, S5_STATE, S5_GROUP
    lam = jnp.stack([lam_re, lam_im]).astype(F32).transpose(2, 0, 1, 3).reshape(G, 2, 2 * P)
    ldt = jnp.repeat(log_dt.astype(F32).T, P, axis=1).reshape(G, 1, 2 * P)
    bt = jnp.stack([b_re, b_im]).astype(F32).transpose(2, 0, 4, 1, 3).reshape(G, 2, H, 2 * P)
    ct = jnp.stack([c_re, c_im]).astype(F32).transpose(2, 0, 3, 1, 4).reshape(G, 2, H, 2 * P)
    dt_tiled = jnp.tile(d_skip.astype(F32).reshape(G, 1, H), (1, 1, S5_CHUNK))
    return _s5_ops_call(lam, ldt, bt, ct, dt_tiled)


def _final_kernel(x_ref, mod_ref, ng_ref, wg_ref, ya_ref, yp_ref, permt_ref, wglu_ref, bglu_ref, wa_ref, wb_ref,
                  wo_ref, fg_ref, o_ref):
    bsz, tt, _ = x_ref.shape
    o1, o2, o3 = ATTN_WIDTH, ATTN_WIDTH + S5_WIDTH, ATTN_WIDTH + S5_WIDTH + D_MODEL
    nb = FINAL_BATCHES
    rows = nb * tt
    groups = [slice(i * nb, (i + 1) * nb) for i in range(bsz // nb)]
    prow = PERM_TOK // S5_CHUNK * bsz
    parts = []
    for part in range(tt // PERM_TOK):
        slabs = [jnp.concatenate([yp_ref[part * prow:(part + 1) * prow,
                                         blk * BLOCK_W + t * LANES: blk * BLOCK_W + (t + 1) * LANES]
                                  for blk in range(N_BLOCKS)], axis=1) for t in range(S5_CHUNK)]
        yb = jnp.dot(permt_ref[...], jnp.concatenate(slabs, axis=0), preferred_element_type=F32)
        parts.append(yb.reshape(bsz, PERM_TOK, S5_WIDTH))
    y3 = parts[0] if len(parts) == 1 else jnp.concatenate(parts, axis=1)
    x3s, gates = [], []
    for hs in groups:
        x3 = x_ref[hs]
        x3s.append(x3)
        xn = _modulated_norm(x3, mod_ref.at[hs], ng_ref).reshape(rows, D_MODEL).astype(BF16)
        gates.append(jnp.dot(xn, wg_ref[...], preferred_element_type=F32))
    mixes = []
    for hs, g in zip(groups, gates):
        y = y3[hs].reshape(rows, S5_WIDTH)
        z = y * (0.5 * (1.0 + jnp.tanh(0.7978845608028654 * (y + 0.044715 * (y * y * y)))))
        zz = z * jax.nn.sigmoid(jnp.dot(z.astype(BF16), wglu_ref[...], preferred_element_type=F32) + bglu_ref[...])
        ya = ya_ref[hs].reshape(rows, ATTN_WIDTH).astype(F32)
        ta = (ya * _silu(g[:, 0:o1])).astype(BF16)
        tb = (zz * _silu(g[:, o1:o2])).astype(BF16)
        pa = jnp.dot(ta, wa_ref[...], preferred_element_type=F32)
        pb = jnp.dot(tb, wb_ref[...], preferred_element_type=F32)
        mixes.append((jax.nn.sigmoid(g[:, o2:o3]) * pa + jax.nn.sigmoid(g[:, o3:]) * pb).astype(BF16))
    outs = [jnp.dot(m, wo_ref[...], preferred_element_type=F32).reshape(nb, tt, D_MODEL) for m in mixes]
    for hs, x3, o in zip(groups, x3s, outs):
        h = x3 + mod_ref[hs, :, 2 * D_MODEL:] * o
        ms = jnp.mean(h * h, axis=-1, keepdims=True)
        o_ref[hs] = h * lax.rsqrt(ms + EPS) * fg_ref[...]


def _final_call(x, mod3, ng, wg, ya, yp, permt, wglu, bglu, wa, wb, wo, fg):
    bsz, n, _ = x.shape
    tt = TOK_TILE
    up_rows = tt // S5_CHUNK * bsz
    tok = lambda width: pl.BlockSpec((bsz, tt, width), lambda i: (0, i, 0))
    c = lambda shape: _const_spec(shape, 1)
    return pl.pallas_call(
        _final_kernel,
        out_shape=jax.ShapeDtypeStruct((bsz, n, D_MODEL), F32),
        grid=(n // tt,),
        in_specs=[tok(D_MODEL),
                  pl.BlockSpec((bsz, 1, 3 * D_MODEL), lambda i: (0, 0, 0)),
                  c((1, D_MODEL)), c(wg.shape),
                  tok(ATTN_WIDTH),
                  pl.BlockSpec((up_rows, N_BLOCKS * BLOCK_W), lambda i: (i, 0)),
                  c(permt.shape), c(wglu.shape), c((1, S5_WIDTH)), c(wa.shape), c(wb.shape), c(wo.shape),
                  c((1, D_MODEL))],
        out_specs=tok(D_MODEL),
        compiler_params=pltpu.CompilerParams(dimension_semantics=("arbitrary",), vmem_limit_bytes=VMEM_LIMIT),
        name="final",
    )(x, mod3, ng, wg, ya, yp, permt, wglu, bglu, wa, wb, wo, fg)


def _rope_tables(n):
    rows = n // GRID_W
    row_ids = np.repeat(np.arange(rows, dtype=np.float64), GRID_W)
    col_ids = np.tile(np.arange(GRID_W, dtype=np.float64), rows)
    freqs = ROPE_THETA ** (-np.arange(ROPE_FREQS, dtype=np.float64) / ROPE_FREQS)
    ang_r, ang_c = row_ids[:, None] * freqs, col_ids[:, None] * freqs
    cos = np.concatenate([np.cos(ang_r)] * 2 + [np.cos(ang_c)] * 2, axis=1)
    sin = np.concatenate([-np.sin(ang_r), np.sin(ang_r), -np.sin(ang_c), np.sin(ang_c)], axis=1)
    reps = LANES // HEAD_DIM
    return (jnp.asarray(np.tile(cos, (1, reps)).astype(np.float32)),
            jnp.asarray(np.tile(sin, (1, reps)).astype(np.float32)))


def _one_hot(match):
    return jnp.asarray(np.ascontiguousarray(match).astype(BF16))


def _block_ones(width):
    idx = np.arange(width) // HEAD_DIM
    return _one_hot(idx[:, None] == idx[None, :])


def _row_perm(bsz, tt, transpose=False):
    chunks = tt // S5_CHUNK
    r = np.arange(bsz * tt)
    t, pc, b = r // (chunks * bsz), (r // bsz) % chunks, r % bsz
    src = b * tt + pc * S5_CHUNK + t
    match = src[:, None] == np.arange(bsz * tt)[None, :]
    return _one_hot(match.T if transpose else match)


def _lane_perm(transpose=False):
    r = np.arange(HALF_W)
    t, j, h = r // LANES, (r % LANES) // S5_GROUP, r % S5_GROUP
    dst = j * LANES + t * S5_GROUP + h
    match = dst[:, None] == np.arange(HALF_W)[None, :]
    return _one_hot(match.T if transpose else match)


def _kv_expand():
    col = np.arange(N_KV_HEADS * REP_W)
    src = (col // REP_W) * HEAD_DIM + col % HEAD_DIM
    return _one_hot(np.arange(KV_WIDTH)[:, None] == src[None, :])


def kernel(x, c, ctx, c_ctx, norm_g, w_ada, b_ada, w_in, q_norm_g, k_norm_g, s5_lam_re, s5_lam_im, s5_log_dt,
           s5_b_re, s5_b_im, s5_c_re, s5_c_im, s5_d, w_glu, b_glu, w_branch_attn, w_branch_s5, w_out,
           final_norm_g):
    assert w_in.shape[0] == 1, "single-layer block"
    bsz, n, _ = x.shape
    n_ctx = ctx.shape[1]
    assert n % TOK_TILE == 0 and n_ctx % TOK_TILE == 0 and bsz == SUBLANES

    ada_rows = 2 * SUBLANES
    cc = jnp.concatenate([c, c_ctx[None], jnp.zeros((ada_rows - bsz - 1, D_MODEL), F32)], axis=0)
    mod = _ada_call(cc, w_ada[0], b_ada[0][None])
    mod3 = mod[:bsz].reshape(bsz, 1, 3 * D_MODEL)
    mod_ctx3 = mod[bsz:bsz + 1].reshape(1, 1, 3 * D_MODEL)

    offs = [0]
    for s in IN_SIZES:
        offs.append(offs[-1] + s)
    w_bf = w_in[0].astype(BF16)
    w_pre = jnp.concatenate([w_bf[:, offs[0]:offs[3]], w_bf[:, offs[4]:offs[5]]], axis=1)
    w_gates = jnp.concatenate([w_bf[:, offs[3]:offs[4]], w_bf[:, offs[5]:]], axis=1)

    ng = norm_g[0][None]
    qg = jnp.tile(q_norm_g[0], N_HEADS)[None]
    kg = jnp.tile(k_norm_g[0], N_KV_HEADS)[None]
    onesq, onesk = _block_ones(ATTN_WIDTH), _block_ones(KV_WIDTH)
    cos, sin = _rope_tables(n)
    row_perm = _row_perm(bsz, PERM_TOK)
    lane_perm = _lane_perm()

    q, k_all, v_all, up = _pre_call(x, ctx, mod3, mod_ctx3, ng, w_pre, onesq, onesk, qg, kg, cos, sin, row_perm)

    y_attn = _attn_call(q, k_all, v_all, _kv_expand(), tq=1024)

    ug = _perm_in_call(up, lane_perm, tr=576)
    m, s_in, gmat, a16 = _s5_operators(s5_lam_re[0], s5_lam_im[0], s5_log_dt[0], s5_b_re[0], s5_b_im[0],
                                       s5_c_re[0], s5_c_im[0], s5_d[0])
    yg = _s5_call(ug, m, s_in, gmat, a16, n_ctx // S5_CHUNK, gb=GROUPS_PER_BLOCK)
    yp = _perm_out_call(yg, _lane_perm(transpose=True), tr=512)

    return _final_call(x, mod3, ng, w_gates, y_attn, yp, _row_perm(bsz, PERM_TOK, transpose=True),
                       w_glu[0].astype(BF16), b_glu[0][None],
                       w_branch_attn[0].astype(BF16), w_branch_s5[0].astype(BF16), w_out[0].astype(BF16),
                       final_norm_g[None])
```

```python
import functools

import numpy as np
import jax
import jax.numpy as jnp
from jax import lax
from jax.experimental import pallas as pl
from jax.experimental.pallas import tpu as pltpu

D_MODEL = 1024
GRID_W = 64
N_HEADS = 8
N_KV_HEADS = 2
HEAD_DIM = 64
GQA_REP = N_HEADS // N_KV_HEADS
ATTN_WIDTH = N_HEADS * HEAD_DIM
KV_WIDTH = N_KV_HEADS * HEAD_DIM
ATTN_SCALE = HEAD_DIM ** -0.5
ROPE_THETA = 10000.0
ROPE_FREQS = HEAD_DIM // 4
S5_WIDTH = 512
S5_GROUP = 16
S5_GROUPS = S5_WIDTH // S5_GROUP
S5_STATE = 64
EPS = 1e-6
IN_SIZES = (ATTN_WIDTH, KV_WIDTH, KV_WIDTH, ATTN_WIDTH, S5_WIDTH, S5_WIDTH, D_MODEL, D_MODEL)

LANES = 128
SUBLANES = 8
S5_CHUNK = 16
S5_CW = S5_CHUNK * S5_GROUP
GROUPS_PER_BLOCK = LANES // S5_GROUP
N_BLOCKS = S5_WIDTH // LANES
BLOCK_W = S5_CHUNK * LANES
HALF_W = BLOCK_W // 2
TOK_TILE = 128
PERM_TOK = 64
REP_W = GQA_REP * HEAD_DIM
VMEM_LIMIT = 56 * 1024 * 1024

F32 = jnp.float32
BF16 = jnp.bfloat16


def _silu(t):
    return t * jax.nn.sigmoid(t)


def _modulated_norm(x3, mod_ref, ng_ref):
    ms = jnp.mean(x3 * x3, axis=-1, keepdims=True)
    y = x3 * lax.rsqrt(ms + EPS) * ng_ref[...]
    return y * (1.0 + mod_ref[:, :, D_MODEL:2 * D_MODEL]) + mod_ref[:, :, 0:D_MODEL]


def _head_rmsnorm(t, ones_ref, g_ref):
    ss = jnp.dot((t * t).astype(BF16), ones_ref[...], preferred_element_type=F32)
    return t * lax.rsqrt(ss * (1.0 / HEAD_DIM) + EPS) * g_ref[...]


def _rope(t, cos, sin_signed):
    rows = t.shape[0]
    lane = lax.broadcasted_iota(jnp.int32, (rows, LANES), 1)
    first = (lane & ROPE_FREQS) == 0
    outs = []
    for j in range(t.shape[1] // LANES):
        blk = t[:, j * LANES:(j + 1) * LANES]
        partner = jnp.where(first, pltpu.roll(blk, LANES - ROPE_FREQS, 1), pltpu.roll(blk, ROPE_FREQS, 1))
        outs.append(blk * cos + partner * sin_signed)
    return outs[0] if len(outs) == 1 else jnp.concatenate(outs, axis=1)


def _const_spec(shape, grid_rank):
    zeros = (0,) * len(shape)
    return pl.BlockSpec(shape, lambda *_: zeros, pipeline_mode=pl.Buffered(1))


def _ada_kernel(c_ref, w_ref, b_ref, o_ref):
    s = _silu(c_ref[...])
    w = w_ref[...]
    s_hi, w_hi = s.astype(BF16), w.astype(BF16)
    s_lo = (s - s_hi.astype(F32)).astype(BF16)
    w_lo = (w - w_hi.astype(F32)).astype(BF16)
    dot = functools.partial(jnp.dot, preferred_element_type=F32)
    o_ref[...] = dot(s_hi, w_hi) + dot(s_lo, w_hi) + dot(s_hi, w_lo) + b_ref[...]


def _ada_call(cc, w, b):
    rows, n = cc.shape[0], w.shape[1]
    tn = 1024
    return pl.pallas_call(
        _ada_kernel,
        out_shape=jax.ShapeDtypeStruct((rows, n), F32),
        grid=(n // tn,),
        in_specs=[pl.BlockSpec((rows, D_MODEL), lambda j: (0, 0)),
                  pl.BlockSpec((D_MODEL, tn), lambda j: (0, j)),
                  pl.BlockSpec((1, tn), lambda j: (0, j))],
        out_specs=pl.BlockSpec((rows, tn), lambda j: (0, j)),
        compiler_params=pltpu.CompilerParams(dimension_semantics=("arbitrary",)),
        name="ada",
    )(cc, w, b)


def _store_chunk_major(u, perm_ref, up_out, bsz):
    tt = u.shape[0] // bsz
    u3 = u.astype(BF16).reshape(bsz, tt, S5_WIDTH)
    rows = PERM_TOK // S5_CHUNK * bsz
    for part in range(tt // PERM_TOK):
        up = u3[:, part * PERM_TOK:(part + 1) * PERM_TOK, :].reshape(bsz * PERM_TOK, S5_WIDTH)
        r = jnp.dot(perm_ref[...], up, preferred_element_type=F32).astype(BF16)
        for t in range(S5_CHUNK):
            for blk in range(N_BLOCKS):
                up_out[part * rows:(part + 1) * rows, blk * BLOCK_W + t * LANES: blk * BLOCK_W + (t + 1) * LANES] = (
                    r[t * rows:(t + 1) * rows, blk * LANES:(blk + 1) * LANES])


def _pre_kernel(x_ref, c_ref, mod_ref, modc_ref, ng_ref, w_ref, onesq_ref, onesk_ref, qg_ref, kg_ref, cos_ref, sin_ref,
                perm_ref, q_out, k_out, v_out, up_out, *, ctx_steps):
    bsz, tt, _ = x_ref.shape
    o1, o2, o3 = ATTN_WIDTH, ATTN_WIDTH + KV_WIDTH, ATTN_WIDTH + 2 * KV_WIDTH
    step = pl.program_id(0)

    @pl.when(step < ctx_steps)
    def _():
        xn = _modulated_norm(c_ref[...], modc_ref, ng_ref).reshape(bsz * tt, D_MODEL).astype(BF16)
        p = jnp.dot(xn, w_ref[:, o1:], preferred_element_type=F32)
        k = _head_rmsnorm(p[:, 0:KV_WIDTH], onesk_ref, kg_ref)
        k_out[...] = k.astype(BF16).reshape(bsz, tt, KV_WIDTH)
        v_out[...] = p[:, KV_WIDTH:2 * KV_WIDTH].astype(BF16).reshape(bsz, tt, KV_WIDTH)
        _store_chunk_major(p[:, 2 * KV_WIDTH:], perm_ref, up_out, bsz)

    @pl.when(step >= ctx_steps)
    def _():
        xn = _modulated_norm(x_ref[...], mod_ref, ng_ref).reshape(bsz * tt, D_MODEL).astype(BF16)
        p = jnp.dot(xn, w_ref[...], preferred_element_type=F32)
        cos = jnp.concatenate([cos_ref[...]] * bsz, axis=0)
        sin = jnp.concatenate([sin_ref[...]] * bsz, axis=0)
        q = _rope(_head_rmsnorm(p[:, 0:o1], onesq_ref, qg_ref), cos, sin)
        k = _rope(_head_rmsnorm(p[:, o1:o2], onesk_ref, kg_ref), cos, sin)
        q_out[...] = (q * ATTN_SCALE).astype(BF16).reshape(bsz, tt, ATTN_WIDTH)
        k_out[...] = k.astype(BF16).reshape(bsz, tt, KV_WIDTH)
        v_out[...] = p[:, o2:o3].astype(BF16).reshape(bsz, tt, KV_WIDTH)
        _store_chunk_major(p[:, o3:], perm_ref, up_out, bsz)


def _pre_call(x, ctx, mod3, mod_ctx3, ng, w, onesq, onesk, qg, kg, cos, sin, perm):
    bsz, n, _ = x.shape
    n_ctx = ctx.shape[1]
    tt = TOK_TILE
    up_rows = tt // S5_CHUNK * bsz
    ctx_steps = n_ctx // tt
    lat = lambda i: jnp.maximum(i - ctx_steps, 0)
    c = lambda shape: _const_spec(shape, 1)
    return pl.pallas_call(
        functools.partial(_pre_kernel, ctx_steps=ctx_steps),
        out_shape=(jax.ShapeDtypeStruct((bsz, n, ATTN_WIDTH), BF16),
                   jax.ShapeDtypeStruct((bsz, n_ctx + n, KV_WIDTH), BF16),
                   jax.ShapeDtypeStruct((bsz, n_ctx + n, KV_WIDTH), BF16),
                   jax.ShapeDtypeStruct(((n_ctx + n) // S5_CHUNK * bsz, N_BLOCKS * BLOCK_W), BF16)),
        grid=(ctx_steps + n // tt,),
        in_specs=[pl.BlockSpec((bsz, tt, D_MODEL), lambda i: (0, lat(i), 0)),
                  pl.BlockSpec((bsz, tt, D_MODEL), lambda i: (0, jnp.minimum(i, ctx_steps - 1), 0)),
                  c((bsz, 1, 3 * D_MODEL)), c((1, 1, 3 * D_MODEL)),
                  c((1, D_MODEL)), c(w.shape), c(onesq.shape), c(onesk.shape),
                  c((1, ATTN_WIDTH)), c((1, KV_WIDTH)),
                  pl.BlockSpec((tt, LANES), lambda i: (lat(i), 0)),
                  pl.BlockSpec((tt, LANES), lambda i: (lat(i), 0)),
                  c(perm.shape)],
        out_specs=(pl.BlockSpec((bsz, tt, ATTN_WIDTH), lambda i: (0, lat(i), 0)),
                   pl.BlockSpec((bsz, tt, KV_WIDTH), lambda i: (0, i, 0)),
                   pl.BlockSpec((bsz, tt, KV_WIDTH), lambda i: (0, i, 0)),
                   pl.BlockSpec((up_rows, N_BLOCKS * BLOCK_W), lambda i: (i, 0))),
        compiler_params=pltpu.CompilerParams(dimension_semantics=("arbitrary",), vmem_limit_bytes=VMEM_LIMIT),
        name="pre",
    )(x, ctx, mod3, mod_ctx3, ng, w, onesq, onesk, qg, kg, cos, sin, perm)


FINAL_BATCHES = 2
ATTN_ROWS = 256


def _attn_kernel(q_ref, k_ref, v_ref, e_ref, o_ref, k4t_ref, v4_ref):
    @pl.when(pl.program_id(1) == 0)
    def _():
        kt = k_ref[...].astype(F32).T.astype(BF16)
        v4 = jnp.dot(v_ref[...], e_ref[...], preferred_element_type=F32)
        for g in range(N_KV_HEADS):
            k4t_ref[g] = jnp.concatenate([kt[g * HEAD_DIM:(g + 1) * HEAD_DIM]] * GQA_REP, axis=0)
            v4_ref[g] = v4[:, g * REP_W:(g + 1) * REP_W].astype(BF16)

    lane = lax.broadcasted_iota(jnp.int32, (ATTN_ROWS, REP_W), 1)
    for part in range(q_ref.shape[0] // ATTN_ROWS):
        rows = slice(part * ATTN_ROWS, (part + 1) * ATTN_ROWS)
        for g in range(N_KV_HEADS):
            qg = q_ref[rows, g * REP_W:(g + 1) * REP_W]
            acc = jnp.zeros((ATTN_ROWS, REP_W), F32)
            for r in range(GQA_REP):
                in_head = (lane >= r * HEAD_DIM) & (lane < (r + 1) * HEAD_DIM)
                qr = jnp.where(in_head, qg, jnp.zeros_like(qg))
                s = jnp.dot(qr, k4t_ref[g], preferred_element_type=F32)
                m = jnp.max(s, axis=1, keepdims=True)
                p = jnp.exp(s - m)
                l = jnp.sum(p, axis=1, keepdims=True)
                o = jnp.dot(p.astype(BF16), v4_ref[g], preferred_element_type=F32)
                acc = jnp.where(in_head, o / l, acc)
            o_ref[rows, g * REP_W:(g + 1) * REP_W] = acc.astype(o_ref.dtype)


def _attn_call(q, k_all, v_all, expand, tq):
    bsz, n, _ = q.shape
    nk = k_all.shape[1]
    return pl.pallas_call(
        _attn_kernel,
        out_shape=jax.ShapeDtypeStruct((bsz, n, ATTN_WIDTH), BF16),
        grid=(bsz, n // tq),
        in_specs=[pl.BlockSpec((None, tq, ATTN_WIDTH), lambda b, i: (b, i, 0)),
                  pl.BlockSpec((None, nk, KV_WIDTH), lambda b, i: (b, 0, 0)),
                  pl.BlockSpec((None, nk, KV_WIDTH), lambda b, i: (b, 0, 0)),
                  _const_spec(expand.shape, 2)],
        out_specs=pl.BlockSpec((None, tq, ATTN_WIDTH), lambda b, i: (b, i, 0)),
        scratch_shapes=[pltpu.VMEM((N_KV_HEADS, REP_W, nk), BF16),
                        pltpu.VMEM((N_KV_HEADS, nk, REP_W), BF16)],
        compiler_params=pltpu.CompilerParams(dimension_semantics=("arbitrary", "arbitrary"),
                                             vmem_limit_bytes=VMEM_LIMIT),
        name="attn",
    )(q, k_all, v_all, expand)


def _perm_in_kernel(s_ref, p_ref, o_ref):
    halves = [jnp.dot(s_ref[:, hf * HALF_W:(hf + 1) * HALF_W], p_ref[...], preferred_element_type=F32)
              for hf in range(BLOCK_W // HALF_W)]
    for j in range(GROUPS_PER_BLOCK):
        o_ref[j] = jnp.concatenate([r[:, j * LANES:(j + 1) * LANES] for r in halves], axis=1).astype(o_ref.dtype)


def _perm_in_call(up, pmat, tr):
    rows = up.shape[0]
    return pl.pallas_call(
        _perm_in_kernel,
        out_shape=jax.ShapeDtypeStruct((S5_GROUPS, rows, S5_CW), BF16),
        grid=(N_BLOCKS, rows // tr),
        in_specs=[pl.BlockSpec((tr, BLOCK_W), lambda s, i: (i, s)),
                  _const_spec(pmat.shape, 2)],
        out_specs=pl.BlockSpec((GROUPS_PER_BLOCK, tr, S5_CW), lambda s, i: (s, i, 0)),
        compiler_params=pltpu.CompilerParams(dimension_semantics=("arbitrary", "arbitrary"),
                                             vmem_limit_bytes=VMEM_LIMIT),
        name="perm_in",
    )(up, pmat)


def _s5_kernel(u_ref, m_ref, sin_ref, g_ref, a_ref, q_ref, y_ref, inc_ref, hin_ref, *, n_ctx_chunks, n_chunks):
    half = S5_STATE
    gb = u_ref.shape[0]
    for j in range(gb):
        inc_ref[j] = jnp.dot(u_ref[j], sin_ref[j], preferred_element_type=F32)
    a_re = [jnp.broadcast_to(a_ref[j, 0:1, :], (SUBLANES, LANES)) for j in range(gb)]
    a_im = [jnp.broadcast_to(a_ref[j, 1:2, :], (SUBLANES, LANES)) for j in range(gb)]
    is_fwd = lax.broadcasted_iota(jnp.int32, (SUBLANES, LANES), 1) < half

    def step(k, carry):
        pos_b = jnp.where(k < n_ctx_chunks, n_ctx_chunks - 1 - k, n_chunks + n_ctx_chunks - 1 - k)
        rf = pl.multiple_of(k * SUBLANES, SUBLANES)
        rb = pl.multiple_of(pos_b * SUBLANES, SUBLANES)
        out = []
        for j in range(gb):
            h_re, h_im = carry[2 * j], carry[2 * j + 1]
            xf = inc_ref[j, pl.ds(rf, SUBLANES), :]
            xb = inc_ref[j, pl.ds(rb, SUBLANES), :]
            hin_ref[j, pl.ds(rf, SUBLANES), 0:half] = h_re[:, 0:half]
            hin_ref[j, pl.ds(rf, SUBLANES), 2 * half:3 * half] = h_im[:, 0:half]
            hin_ref[j, pl.ds(rb, SUBLANES), half:2 * half] = h_re[:, half:]
            hin_ref[j, pl.ds(rb, SUBLANES), 3 * half:] = h_im[:, half:]
            x_re = jnp.where(is_fwd, xf[:, 0:LANES], xb[:, 0:LANES])
            x_im = jnp.where(is_fwd, xf[:, LANES:], xb[:, LANES:])
            out.append(a_re[j] * h_re - a_im[j] * h_im + x_re)
            out.append(a_re[j] * h_im + a_im[j] * h_re + x_im)
        return tuple(out)

    zero = jnp.zeros((SUBLANES, LANES), F32)
    lax.fori_loop(0, n_chunks, step, (zero,) * (2 * gb))

    r0 = n_ctx_chunks * SUBLANES
    ys = []
    for j in range(gb):
        y = jnp.dot(u_ref[j, r0:, :], m_ref[j], preferred_element_type=F32)
        y = y + jnp.dot(hin_ref[j, r0:, :].astype(BF16), g_ref[j], preferred_element_type=F32)
        ys.append(y.astype(BF16))
    for hf in range(BLOCK_W // HALF_W):
        ycat = jnp.concatenate([y[:, hf * LANES:(hf + 1) * LANES] for y in ys], axis=1)
        y_ref[:, hf * HALF_W:(hf + 1) * HALF_W] = jnp.dot(ycat, q_ref[...],
                                                         preferred_element_type=F32).astype(y_ref.dtype)


def _s5_call(ug, m, sin, gmat, a16, qmat, n_ctx_chunks):
    groups, rows, _ = ug.shape
    gb = GROUPS_PER_BLOCK
    n_chunks = rows // SUBLANES
    out_rows = rows - n_ctx_chunks * SUBLANES
    mat = lambda: pl.BlockSpec((gb, S5_CW, S5_CW), lambda g: (g, 0, 0))
    return pl.pallas_call(
        functools.partial(_s5_kernel, n_ctx_chunks=n_ctx_chunks, n_chunks=n_chunks),
        out_shape=jax.ShapeDtypeStruct((out_rows, groups // gb * BLOCK_W), BF16),
        grid=(groups // gb,),
        in_specs=[pl.BlockSpec((gb, rows, S5_CW), lambda g: (g, 0, 0)),
                  mat(), mat(), mat(),
                  pl.BlockSpec((gb, 2, LANES), lambda g: (g, 0, 0)),
                  _const_spec(qmat.shape, 1)],
        out_specs=pl.BlockSpec((out_rows, BLOCK_W), lambda g: (0, g)),
        scratch_shapes=[pltpu.VMEM((gb, rows, S5_CW), F32), pltpu.VMEM((gb, rows, S5_CW), F32)],
        compiler_params=pltpu.CompilerParams(dimension_semantics=("arbitrary",),
                                             vmem_limit_bytes=VMEM_LIMIT),
        name="s5",
    )(ug, m, sin, gmat, a16, qmat)


def _s5_ops_kernel(*refs):
    for j in range(refs[0].shape[0]):
        _s5_ops_group(*(r.at[j] for r in refs))


def _s5_ops_group(lam_ref, ldt_ref, bt_ref, ct_ref, d_ref, m_ref, sin_ref, g_ref, a_ref):
    T, H = S5_CHUNK, S5_GROUP
    lr = jnp.minimum(lam_ref[0:1, :], -1e-4)
    li = lam_ref[1:2, :]
    dt = jnp.exp(ldt_ref[...])
    taus = lax.broadcasted_iota(jnp.int32, (3 * SUBLANES, LANES), 0).astype(F32)
    mag = jnp.exp(lr * dt * taus)
    pw_r = mag * jnp.cos(li * dt * taus)
    pw_i = mag * jnp.sin(li * dt * taus)
    nr, ni = pw_r[1:2] - 1.0, pw_i[1:2]
    den = lr * lr + li * li
    cf_r = (nr * lr + ni * li) / den
    cf_i = (ni * lr - nr * li) / den
    bb_r = cf_r * bt_ref[0] - cf_i * bt_ref[1]
    bb_i = cf_r * bt_ref[1] + cf_i * bt_ref[0]
    is_fwd = lax.broadcasted_iota(jnp.int32, (H, LANES), 1) < S5_STATE

    def powers(tau_f, tau_b):
        pick = lambda pw, s: jnp.where(is_fwd, jnp.broadcast_to(pw[tau_f(s):tau_f(s) + 1], (H, LANES)),
                                       jnp.broadcast_to(pw[tau_b(s):tau_b(s) + 1], (H, LANES)))
        return (jnp.concatenate([pick(pw_r, s) for s in range(T)], axis=0),
                jnp.concatenate([pick(pw_i, s) for s in range(T)], axis=0))

    tile = lambda a: jnp.concatenate([a] * T, axis=0)
    bbr, bbi, cr, ci = tile(bb_r), tile(bb_i), tile(ct_ref[0]), tile(ct_ref[1])

    er, ei = powers(lambda s: T - 1 - s, lambda s: s)
    sin_ref[:, 0:LANES] = (er * bbr - ei * bbi).astype(sin_ref.dtype)
    sin_ref[:, LANES:] = (er * bbi + ei * bbr).astype(sin_ref.dtype)

    er, ei = powers(lambda t: t + 1, lambda t: T - t)
    gt = jnp.concatenate([er * cr - ei * ci, -(er * ci + ei * cr)], axis=1)
    g_ref[...] = gt.T.astype(g_ref.dtype)

    er, ei = powers(lambda a: a, lambda a: T - 1 - a)
    cp = jnp.concatenate([er * cr - ei * ci, er * ci + ei * cr], axis=1)
    zero = jnp.zeros_like(bb_r)
    lhs = jnp.concatenate([jnp.concatenate([jnp.where(is_fwd, bb_r, zero), jnp.where(is_fwd, -bb_i, zero)], axis=1),
                           jnp.concatenate([jnp.where(is_fwd, zero, bb_r), jnp.where(is_fwd, zero, -bb_i)], axis=1)],
                          axis=0)
    kr = lax.dot_general(lhs, cp, (((1,), (1,)), ((), ())), preferred_element_type=F32,
                         precision=lax.Precision.HIGHEST)
    pad = jnp.zeros((H, S5_CW), F32)
    wide_f = jnp.concatenate([pad, kr[0:H]], axis=1)
    wide_b = jnp.concatenate([kr[H:], pad], axis=1)
    lane = lax.broadcasted_iota(jnp.int32, (H, S5_CW), 1)
    row = lax.broadcasted_iota(jnp.int32, (H, S5_CW), 0)
    skip = jnp.broadcast_to(d_ref[...], (H, S5_CW))
    for s in range(T):
        blk_f = pltpu.roll(wide_f, H * s, 1)[:, S5_CW:] if s else wide_f[:, S5_CW:]
        shift_b = (2 * S5_CW - H * (T - 1 - s)) % (2 * S5_CW)
        blk_b = (pltpu.roll(wide_b, shift_b, 1) if shift_b else wide_b)[:, :S5_CW]
        diag = jnp.where(lane == H * s + row, skip, 0.0)
        m_ref[H * s:H * (s + 1), :] = (blk_f + blk_b + diag).astype(m_ref.dtype)
    a_ref[0:1, :] = pw_r[T:T + 1]
    a_ref[1:2, :] = pw_i[T:T + 1]


def _s5_ops_call(lam, ldt, bt, ct, dt_tiled):
    groups = lam.shape[0]
    gb = 8
    mat = lambda: pl.BlockSpec((gb, S5_CW, S5_CW), lambda g: (g, 0, 0))
    vec = lambda a: pl.BlockSpec((gb,) + a.shape[1:], lambda g: (g,) + (0,) * (a.ndim - 1))
    mshape = jax.ShapeDtypeStruct((groups, S5_CW, S5_CW), BF16)
    return pl.pallas_call(
        _s5_ops_kernel,
        out_shape=(mshape, mshape, mshape, jax.ShapeDtypeStruct((groups, 2, LANES), F32)),
        grid=(groups // gb,),
        in_specs=[vec(lam), vec(ldt), vec(bt), vec(ct), vec(dt_tiled)],
        out_specs=(mat(), mat(), mat(), pl.BlockSpec((gb, 2, LANES), lambda g: (g, 0, 0))),
        compiler_params=pltpu.CompilerParams(dimension_semantics=("arbitrary",)),
        name="s5_ops",
    )(lam, ldt, bt, ct, dt_tiled)


def _s5_operators(lam_re, lam_im, log_dt, b_re, b_im, c_re, c_im, d_skip):
    G, P, H = S5_GROUPS, S5_STATE, S5_GROUP
    lam = jnp.stack([lam_re, lam_im]).astype(F32).transpose(2, 0, 1, 3).reshape(G, 2, 2 * P)
    ldt = jnp.repeat(log_dt.astype(F32).T, P, axis=1).reshape(G, 1, 2 * P)
    bt = jnp.stack([b_re, b_im]).astype(F32).transpose(2, 0, 4, 1, 3).reshape(G, 2, H, 2 * P)
    ct = jnp.stack([c_re, c_im]).astype(F32).transpose(2, 0, 3, 1, 4).reshape(G, 2, H, 2 * P)
    dt_tiled = jnp.tile(d_skip.astype(F32).reshape(G, 1, H), (1, 1, S5_CHUNK))
    return _s5_ops_call(lam, ldt, bt, ct, dt_tiled)


def _final_kernel(x_ref, mod_ref, ng_ref, wg_ref, ya_ref, yp_ref, permt_ref, wglu_ref, bglu_ref, wa_ref, wb_ref,
                  wo_ref, fg_ref, o_ref):
    bsz, tt, _ = x_ref.shape
    o1, o2, o3 = ATTN_WIDTH, ATTN_WIDTH + S5_WIDTH, ATTN_WIDTH + S5_WIDTH + D_MODEL
    nb = FINAL_BATCHES
    rows = nb * tt
    groups = [slice(i * nb, (i + 1) * nb) for i in range(bsz // nb)]
    prow = PERM_TOK // S5_CHUNK * bsz
    parts = []
    for part in range(tt // PERM_TOK):
        slabs = [jnp.concatenate([yp_ref[part * prow:(part + 1) * prow,
                                         blk * BLOCK_W + t * LANES: blk * BLOCK_W + (t + 1) * LANES]
                                  for blk in range(N_BLOCKS)], axis=1) for t in range(S5_CHUNK)]
        yb = jnp.dot(permt_ref[...], jnp.concatenate(slabs, axis=0), preferred_element_type=F32)
        parts.append(yb.reshape(bsz, PERM_TOK, S5_WIDTH))
    y3 = parts[0] if len(parts) == 1 else jnp.concatenate(parts, axis=1)
    x3s, gates = [], []
    for hs in groups:
        x3 = x_ref[hs]
        x3s.append(x3)
        xn = _modulated_norm(x3, mod_ref.at[hs], ng_ref).reshape(rows, D_MODEL).astype(BF16)
        gates.append(jnp.dot(xn, wg_ref[...], preferred_element_type=F32))
    mixes = []
    for hs, g in zip(groups, gates):
        y = y3[hs].reshape(rows, S5_WIDTH)
        z = y * (0.5 * (1.0 + jnp.tanh(0.7978845608028654 * (y + 0.044715 * (y * y * y)))))
        zz = z * jax.nn.sigmoid(jnp.dot(z.astype(BF16), wglu_ref[...], preferred_element_type=F32) + bglu_ref[...])
        ya = ya_ref[hs].reshape(rows, ATTN_WIDTH).astype(F32)
        ta = (ya * _silu(g[:, 0:o1])).astype(BF16)
        tb = (zz * _silu(g[:, o1:o2])).astype(BF16)
        pa = jnp.dot(ta, wa_ref[...], preferred_element_type=F32)
        pb = jnp.dot(tb, wb_ref[...], preferred_element_type=F32)
        mixes.append((jax.nn.sigmoid(g[:, o2:o3]) * pa + jax.nn.sigmoid(g[:, o3:]) * pb).astype(BF16))
    outs = [jnp.dot(m, wo_ref[...], preferred_element_type=F32).reshape(nb, tt, D_MODEL) for m in mixes]
    for hs, x3, o in zip(groups, x3s, outs):
        h = x3 + mod_ref[hs, :, 2 * D_MODEL:] * o
        ms = jnp.mean(h * h, axis=-1, keepdims=True)
        o_ref[hs] = h * lax.rsqrt(ms + EPS) * fg_ref[...]


def _final_call(x, mod3, ng, wg, ya, yp, permt, wglu, bglu, wa, wb, wo, fg):
    bsz, n, _ = x.shape
    tt = TOK_TILE
    up_rows = tt // S5_CHUNK * bsz
    tok = lambda width: pl.BlockSpec((bsz, tt, width), lambda i: (0, i, 0))
    c = lambda shape: _const_spec(shape, 1)
    return pl.pallas_call(
        _final_kernel,
        out_shape=jax.ShapeDtypeStruct((bsz, n, D_MODEL), F32),
        grid=(n // tt,),
        in_specs=[tok(D_MODEL),
                  pl.BlockSpec((bsz, 1, 3 * D_MODEL), lambda i: (0, 0, 0)),
                  c((1, D_MODEL)), c(wg.shape),
                  tok(ATTN_WIDTH),
                  pl.BlockSpec((up_rows, N_BLOCKS * BLOCK_W), lambda i: (i, 0)),
                  c(permt.shape), c(wglu.shape), c((1, S5_WIDTH)), c(wa.shape), c(wb.shape), c(wo.shape),
                  c((1, D_MODEL))],
        out_specs=tok(D_MODEL),
        compiler_params=pltpu.CompilerParams(dimension_semantics=("arbitrary",), vmem_limit_bytes=VMEM_LIMIT),
        name="final",
    )(x, mod3, ng, wg, ya, yp, permt, wglu, bglu, wa, wb, wo, fg)


def _rope_tables(n):
    rows = n // GRID_W
    row_ids = np.repeat(np.arange(rows, dtype=np.float64), GRID_W)
    col_ids = np.tile(np.arange(GRID_W, dtype=np.float64), rows)
    freqs = ROPE_THETA ** (-np.arange(ROPE_FREQS, dtype=np.float64) / ROPE_FREQS)
    ang_r, ang_c = row_ids[:, None] * freqs, col_ids[:, None] * freqs
    cos = np.concatenate([np.cos(ang_r)] * 2 + [np.cos(ang_c)] * 2, axis=1)
    sin = np.concatenate([-np.sin(ang_r), np.sin(ang_r), -np.sin(ang_c), np.sin(ang_c)], axis=1)
    reps = LANES // HEAD_DIM
    return (jnp.asarray(np.tile(cos, (1, reps)).astype(np.float32)),
            jnp.asarray(np.tile(sin, (1, reps)).astype(np.float32)))


def _one_hot(match):
    return jnp.asarray(np.ascontiguousarray(match).astype(BF16))


def _block_ones(width):
    idx = np.arange(width) // HEAD_DIM
    return _one_hot(idx[:, None] == idx[None, :])


def _row_perm(bsz, tt, transpose=False):
    chunks = tt // S5_CHUNK
    r = np.arange(bsz * tt)
    t, pc, b = r // (chunks * bsz), (r // bsz) % chunks, r % bsz
    src = b * tt + pc * S5_CHUNK + t
    match = src[:, None] == np.arange(bsz * tt)[None, :]
    return _one_hot(match.T if transpose else match)


def _lane_perm(transpose=False):
    r = np.arange(HALF_W)
    t, j, h = r // LANES, (r % LANES) // S5_GROUP, r % S5_GROUP
    dst = j * LANES + t * S5_GROUP + h
    match = dst[:, None] == np.arange(HALF_W)[None, :]
    return _one_hot(match.T if transpose else match)


def _kv_expand():
    col = np.arange(N_KV_HEADS * REP_W)
    src = (col // REP_W) * HEAD_DIM + col % HEAD_DIM
    return _one_hot(np.arange(KV_WIDTH)[:, None] == src[None, :])


def kernel(x, c, ctx, c_ctx, norm_g, w_ada, b_ada, w_in, q_norm_g, k_norm_g, s5_lam_re, s5_lam_im, s5_log_dt,
           s5_b_re, s5_b_im, s5_c_re, s5_c_im, s5_d, w_glu, b_glu, w_branch_attn, w_branch_s5, w_out,
           final_norm_g):
    assert w_in.shape[0] == 1, "single-layer block"
    bsz, n, _ = x.shape
    n_ctx = ctx.shape[1]
    assert n % TOK_TILE == 0 and n_ctx % TOK_TILE == 0 and bsz == SUBLANES

    ada_rows = 2 * SUBLANES
    cc = jnp.concatenate([c, c_ctx[None], jnp.zeros((ada_rows - bsz - 1, D_MODEL), F32)], axis=0)
    mod = _ada_call(cc, w_ada[0], b_ada[0][None])
    mod3 = mod[:bsz].reshape(bsz, 1, 3 * D_MODEL)
    mod_ctx3 = mod[bsz:bsz + 1].reshape(1, 1, 3 * D_MODEL)

    offs = [0]
    for s in IN_SIZES:
        offs.append(offs[-1] + s)
    w_bf = w_in[0].astype(BF16)
    w_pre = jnp.concatenate([w_bf[:, offs[0]:offs[3]], w_bf[:, offs[4]:offs[5]]], axis=1)
    w_gates = jnp.concatenate([w_bf[:, offs[3]:offs[4]], w_bf[:, offs[5]:]], axis=1)

    ng = norm_g[0][None]
    qg = jnp.tile(q_norm_g[0], N_HEADS)[None]
    kg = jnp.tile(k_norm_g[0], N_KV_HEADS)[None]
    onesq, onesk = _block_ones(ATTN_WIDTH), _block_ones(KV_WIDTH)
    cos, sin = _rope_tables(n)
    row_perm = _row_perm(bsz, PERM_TOK)
    lane_perm = _lane_perm()

    q, k_all, v_all, up = _pre_call(x, ctx, mod3, mod_ctx3, ng, w_pre, onesq, onesk, qg, kg, cos, sin, row_perm)

    y_attn = _attn_call(q, k_all, v_all, _kv_expand(), tq=1024)

    ug = _perm_in_call(up, lane_perm, tr=576)
    m, s_in, gmat, a16 = _s5_operators(s5_lam_re[0], s5_lam_im[0], s5_log_dt[0], s5_b_re[0], s5_b_im[0],
                                       s5_c_re[0], s5_c_im[0], s5_d[0])
    yp = _s5_call(ug, m, s_in, gmat, a16, _lane_perm(transpose=True), n_ctx // S5_CHUNK)

    return _final_call(x, mod3, ng, w_gates, y_attn, yp, _row_perm(bsz, PERM_TOK, transpose=True),
                       w_glu[0].astype(BF16), b_glu[0][None],
                       w_branch_attn[0].astype(BF16), w_branch_s5[0].astype(BF16), w_out[0].astype(BF16),
                       final_norm_g[None])
```

```python
import functools

import numpy as np
import jax
import jax.numpy as jnp
from jax import lax
from jax.experimental import pallas as pl
from jax.experimental.pallas import tpu as pltpu

D_MODEL = 1024
GRID_W = 64
N_HEADS = 8
N_KV_HEADS = 2
HEAD_DIM = 64
GQA_REP = N_HEADS // N_KV_HEADS
ATTN_WIDTH = N_HEADS * HEAD_DIM
KV_WIDTH = N_KV_HEADS * HEAD_DIM
ATTN_SCALE = HEAD_DIM ** -0.5
ROPE_THETA = 10000.0
ROPE_FREQS = HEAD_DIM // 4
S5_WIDTH = 512
S5_GROUP = 16
S5_GROUPS = S5_WIDTH // S5_GROUP
S5_STATE = 64
EPS = 1e-6
IN_SIZES = (ATTN_WIDTH, KV_WIDTH, KV_WIDTH, ATTN_WIDTH, S5_WIDTH, S5_WIDTH, D_MODEL, D_MODEL)

LANES = 128
SUBLANES = 8
S5_CHUNK = 16
S5_CW = S5_CHUNK * S5_GROUP
GROUPS_PER_BLOCK = LANES // S5_GROUP
N_BLOCKS = S5_WIDTH // LANES
BLOCK_W = S5_CHUNK * LANES
HALF_W = BLOCK_W // 2
TOK_TILE = 128
PERM_TOK = 64
REP_W = GQA_REP * HEAD_DIM
VMEM_LIMIT = 56 * 1024 * 1024

F32 = jnp.float32
BF16 = jnp.bfloat16


def _silu(t):
    return t * jax.nn.sigmoid(t)


def _modulated_norm(x3, mod_ref, ng_ref):
    ms = jnp.mean(x3 * x3, axis=-1, keepdims=True)
    y = x3 * lax.rsqrt(ms + EPS) * ng_ref[...]
    return y * (1.0 + mod_ref[:, :, D_MODEL:2 * D_MODEL]) + mod_ref[:, :, 0:D_MODEL]


def _head_rmsnorm(t, ones_ref, g_ref):
    ss = jnp.dot((t * t).astype(BF16), ones_ref[...], preferred_element_type=F32)
    return t * lax.rsqrt(ss * (1.0 / HEAD_DIM) + EPS) * g_ref[...]


def _rope(t, cos, sin_signed):
    rows = t.shape[0]
    lane = lax.broadcasted_iota(jnp.int32, (rows, LANES), 1)
    first = (lane & ROPE_FREQS) == 0
    outs = []
    for j in range(t.shape[1] // LANES):
        blk = t[:, j * LANES:(j + 1) * LANES]
        partner = jnp.where(first, pltpu.roll(blk, LANES - ROPE_FREQS, 1), pltpu.roll(blk, ROPE_FREQS, 1))
        outs.append(blk * cos + partner * sin_signed)
    return outs[0] if len(outs) == 1 else jnp.concatenate(outs, axis=1)


def _const_spec(shape, grid_rank):
    zeros = (0,) * len(shape)
    return pl.BlockSpec(shape, lambda *_: zeros, pipeline_mode=pl.Buffered(1))


def _ada_kernel(c_ref, w_ref, b_ref, o_ref):
    s = _silu(c_ref[...])
    w = w_ref[...]
    s_hi, w_hi = s.astype(BF16), w.astype(BF16)
    s_lo = (s - s_hi.astype(F32)).astype(BF16)
    w_lo = (w - w_hi.astype(F32)).astype(BF16)
    dot = functools.partial(jnp.dot, preferred_element_type=F32)
    o_ref[...] = dot(s_hi, w_hi) + dot(s_lo, w_hi) + dot(s_hi, w_lo) + b_ref[...]


def _ada_call(cc, w, b):
    rows, n = cc.shape[0], w.shape[1]
    tn = 1024
    return pl.pallas_call(
        _ada_kernel,
        out_shape=jax.ShapeDtypeStruct((rows, n), F32),
        grid=(n // tn,),
        in_specs=[pl.BlockSpec((rows, D_MODEL), lambda j: (0, 0)),
                  pl.BlockSpec((D_MODEL, tn), lambda j: (0, j)),
                  pl.BlockSpec((1, tn), lambda j: (0, j))],
        out_specs=pl.BlockSpec((rows, tn), lambda j: (0, j)),
        compiler_params=pltpu.CompilerParams(dimension_semantics=("arbitrary",)),
        name="ada",
    )(cc, w, b)


def _store_chunk_major(u, perm_ref, up_out, bsz):
    tt = u.shape[0] // bsz
    u3 = u.astype(BF16).reshape(bsz, tt, S5_WIDTH)
    rows = PERM_TOK // S5_CHUNK * bsz
    for part in range(tt // PERM_TOK):
        up = u3[:, part * PERM_TOK:(part + 1) * PERM_TOK, :].reshape(bsz * PERM_TOK, S5_WIDTH)
        r = jnp.dot(perm_ref[...], up, preferred_element_type=F32).astype(BF16)
        for t in range(S5_CHUNK):
            for blk in range(N_BLOCKS):
                up_out[part * rows:(part + 1) * rows, blk * BLOCK_W + t * LANES: blk * BLOCK_W + (t + 1) * LANES] = (
                    r[t * rows:(t + 1) * rows, blk * LANES:(blk + 1) * LANES])


def _pre_kernel(x_ref, c_ref, mod_ref, modc_ref, ng_ref, w_ref, onesq_ref, onesk_ref, qg_ref, kg_ref, cos_ref, sin_ref,
                perm_ref, q_out, k_out, v_out, up_out, *, ctx_steps):
    bsz, tt, _ = x_ref.shape
    o1, o2, o3 = ATTN_WIDTH, ATTN_WIDTH + KV_WIDTH, ATTN_WIDTH + 2 * KV_WIDTH
    step = pl.program_id(0)

    @pl.when(step < ctx_steps)
    def _():
        xn = _modulated_norm(c_ref[...], modc_ref, ng_ref).reshape(bsz * tt, D_MODEL).astype(BF16)
        p = jnp.dot(xn, w_ref[:, o1:], preferred_element_type=F32)
        k = _head_rmsnorm(p[:, 0:KV_WIDTH], onesk_ref, kg_ref)
        k_out[...] = k.astype(BF16).reshape(bsz, tt, KV_WIDTH)
        v_out[...] = p[:, KV_WIDTH:2 * KV_WIDTH].astype(BF16).reshape(bsz, tt, KV_WIDTH)
        _store_chunk_major(p[:, 2 * KV_WIDTH:], perm_ref, up_out, bsz)

    @pl.when(step >= ctx_steps)
    def _():
        xn = _modulated_norm(x_ref[...], mod_ref, ng_ref).reshape(bsz * tt, D_MODEL).astype(BF16)
        p = jnp.dot(xn, w_ref[...], preferred_element_type=F32)
        cos = jnp.concatenate([cos_ref[...]] * bsz, axis=0)
        sin = jnp.concatenate([sin_ref[...]] * bsz, axis=0)
        q = _rope(_head_rmsnorm(p[:, 0:o1], onesq_ref, qg_ref), cos, sin)
        k = _rope(_head_rmsnorm(p[:, o1:o2], onesk_ref, kg_ref), cos, sin)
        q_out[...] = (q * ATTN_SCALE).astype(BF16).reshape(bsz, tt, ATTN_WIDTH)
        k_out[...] = k.astype(BF16).reshape(bsz, tt, KV_WIDTH)
        v_out[...] = p[:, o2:o3].astype(BF16).reshape(bsz, tt, KV_WIDTH)
        _store_chunk_major(p[:, o3:], perm_ref, up_out, bsz)


def _pre_call(x, ctx, mod3, mod_ctx3, ng, w, onesq, onesk, qg, kg, cos, sin, perm):
    bsz, n, _ = x.shape
    n_ctx = ctx.shape[1]
    tt = TOK_TILE
    up_rows = tt // S5_CHUNK * bsz
    ctx_steps = n_ctx // tt
    lat = lambda i: jnp.maximum(i - ctx_steps, 0)
    c = lambda shape: _const_spec(shape, 1)
    return pl.pallas_call(
        functools.partial(_pre_kernel, ctx_steps=ctx_steps),
        out_shape=(jax.ShapeDtypeStruct((bsz, n, ATTN_WIDTH), BF16),
                   jax.ShapeDtypeStruct((bsz, n_ctx + n, KV_WIDTH), BF16),
                   jax.ShapeDtypeStruct((bsz, n_ctx + n, KV_WIDTH), BF16),
                   jax.ShapeDtypeStruct(((n_ctx + n) // S5_CHUNK * bsz, N_BLOCKS * BLOCK_W), BF16)),
        grid=(ctx_steps + n // tt,),
        in_specs=[pl.BlockSpec((bsz, tt, D_MODEL), lambda i: (0, lat(i), 0)),
                  pl.BlockSpec((bsz, tt, D_MODEL), lambda i: (0, jnp.minimum(i, ctx_steps - 1), 0)),
                  c((bsz, 1, 3 * D_MODEL)), c((1, 1, 3 * D_MODEL)),
                  c((1, D_MODEL)), c(w.shape), c(onesq.shape), c(onesk.shape),
                  c((1, ATTN_WIDTH)), c((1, KV_WIDTH)),
                  pl.BlockSpec((tt, LANES), lambda i: (lat(i), 0)),
                  pl.BlockSpec((tt, LANES), lambda i: (lat(i), 0)),
                  c(perm.shape)],
        out_specs=(pl.BlockSpec((bsz, tt, ATTN_WIDTH), lambda i: (0, lat(i), 0)),
                   pl.BlockSpec((bsz, tt, KV_WIDTH), lambda i: (0, i, 0)),
                   pl.BlockSpec((bsz, tt, KV_WIDTH), lambda i: (0, i, 0)),
                   pl.BlockSpec((up_rows, N_BLOCKS * BLOCK_W), lambda i: (i, 0))),
        compiler_params=pltpu.CompilerParams(dimension_semantics=("arbitrary",), vmem_limit_bytes=VMEM_LIMIT),
        name="pre",
    )(x, ctx, mod3, mod_ctx3, ng, w, onesq, onesk, qg, kg, cos, sin, perm)


FINAL_BATCHES = 2
ATTN_ROWS = 256


def _attn_kernel(q_ref, k_ref, v_ref, e_ref, o_ref, k4t_ref, v4_ref):
    @pl.when(pl.program_id(1) == 0)
    def _():
        kt = k_ref[...].astype(F32).T.astype(BF16)
        v4 = jnp.dot(v_ref[...], e_ref[...], preferred_element_type=F32)
        for g in range(N_KV_HEADS):
            k4t_ref[g] = jnp.concatenate([kt[g * HEAD_DIM:(g + 1) * HEAD_DIM]] * GQA_REP, axis=0)
            v4_ref[g] = v4[:, g * REP_W:(g + 1) * REP_W].astype(BF16)

    lane = lax.broadcasted_iota(jnp.int32, (ATTN_ROWS, REP_W), 1)
    for part in range(q_ref.shape[0] // ATTN_ROWS):
        rows = slice(part * ATTN_ROWS, (part + 1) * ATTN_ROWS)
        for g in range(N_KV_HEADS):
            qg = q_ref[rows, g * REP_W:(g + 1) * REP_W]
            acc = jnp.zeros((ATTN_ROWS, REP_W), F32)
            for r in range(GQA_REP):
                in_head = (lane >= r * HEAD_DIM) & (lane < (r + 1) * HEAD_DIM)
                qr = jnp.where(in_head, qg, jnp.zeros_like(qg))
                s = jnp.dot(qr, k4t_ref[g], preferred_element_type=F32)
                m = jnp.max(s, axis=1, keepdims=True)
                p = jnp.exp(s - m)
                l = jnp.sum(p, axis=1, keepdims=True)
                o = jnp.dot(p.astype(BF16), v4_ref[g], preferred_element_type=F32)
                acc = jnp.where(in_head, o / l, acc)
            o_ref[rows, g * REP_W:(g + 1) * REP_W] = acc.astype(o_ref.dtype)


def _attn_call(q, k_all, v_all, expand, tq):
    bsz, n, _ = q.shape
    nk = k_all.shape[1]
    return pl.pallas_call(
        _attn_kernel,
        out_shape=jax.ShapeDtypeStruct((bsz, n, ATTN_WIDTH), BF16),
        grid=(bsz, n // tq),
        in_specs=[pl.BlockSpec((None, tq, ATTN_WIDTH), lambda b, i: (b, i, 0)),
                  pl.BlockSpec((None, nk, KV_WIDTH), lambda b, i: (b, 0, 0)),
                  pl.BlockSpec((None, nk, KV_WIDTH), lambda b, i: (b, 0, 0)),
                  _const_spec(expand.shape, 2)],
        out_specs=pl.BlockSpec((None, tq, ATTN_WIDTH), lambda b, i: (b, i, 0)),
        scratch_shapes=[pltpu.VMEM((N_KV_HEADS, REP_W, nk), BF16),
                        pltpu.VMEM((N_KV_HEADS, nk, REP_W), BF16)],
        compiler_params=pltpu.CompilerParams(dimension_semantics=("arbitrary", "arbitrary"),
                                             vmem_limit_bytes=VMEM_LIMIT),
        name="attn",
    )(q, k_all, v_all, expand)


def _s5_kernel(s_ref, p_ref, m_ref, sin_ref, g_ref, a_ref, q_ref, y_ref, u_ref, inc_ref, hin_ref, *, n_ctx_chunks,
               n_chunks):
    half = S5_STATE
    gb = u_ref.shape[0]
    for hf in range(BLOCK_W // HALF_W):
        r = jnp.dot(s_ref[:, hf * HALF_W:(hf + 1) * HALF_W], p_ref[...], preferred_element_type=F32).astype(BF16)
        for j in range(gb):
            u_ref[j, :, hf * LANES:(hf + 1) * LANES] = r[:, j * LANES:(j + 1) * LANES]
    for j in range(gb):
        inc_ref[j] = jnp.dot(u_ref[j], sin_ref[j], preferred_element_type=F32)
    a_re = [jnp.broadcast_to(a_ref[j, 0:1, :], (SUBLANES, LANES)) for j in range(gb)]
    a_im = [jnp.broadcast_to(a_ref[j, 1:2, :], (SUBLANES, LANES)) for j in range(gb)]
    is_fwd = lax.broadcasted_iota(jnp.int32, (SUBLANES, LANES), 1) < half

    def step(k, carry):
        pos_b = jnp.where(k < n_ctx_chunks, n_ctx_chunks - 1 - k, n_chunks + n_ctx_chunks - 1 - k)
        rf = pl.multiple_of(k * SUBLANES, SUBLANES)
        rb = pl.multiple_of(pos_b * SUBLANES, SUBLANES)
        out = []
        for j in range(gb):
            h_re, h_im = carry[2 * j], carry[2 * j + 1]
            xf = inc_ref[j, pl.ds(rf, SUBLANES), :]
            xb = inc_ref[j, pl.ds(rb, SUBLANES), :]
            hin_ref[j, pl.ds(rf, SUBLANES), 0:half] = h_re[:, 0:half]
            hin_ref[j, pl.ds(rf, SUBLANES), 2 * half:3 * half] = h_im[:, 0:half]
            hin_ref[j, pl.ds(rb, SUBLANES), half:2 * half] = h_re[:, half:]
            hin_ref[j, pl.ds(rb, SUBLANES), 3 * half:] = h_im[:, half:]
            x_re = jnp.where(is_fwd, xf[:, 0:LANES], xb[:, 0:LANES])
            x_im = jnp.where(is_fwd, xf[:, LANES:], xb[:, LANES:])
            out.append(a_re[j] * h_re - a_im[j] * h_im + x_re)
            out.append(a_re[j] * h_im + a_im[j] * h_re + x_im)
        return tuple(out)

    zero = jnp.zeros((SUBLANES, LANES), F32)
    lax.fori_loop(0, n_chunks, step, (zero,) * (2 * gb))

    r0 = n_ctx_chunks * SUBLANES
    ys = []
    for j in range(gb):
        y = jnp.dot(u_ref[j, r0:, :], m_ref[j], preferred_element_type=F32)
        y = y + jnp.dot(hin_ref[j, r0:, :].astype(BF16), g_ref[j], preferred_element_type=F32)
        ys.append(y.astype(BF16))
    for hf in range(BLOCK_W // HALF_W):
        ycat = jnp.concatenate([y[:, hf * LANES:(hf + 1) * LANES] for y in ys], axis=1)
        y_ref[:, hf * HALF_W:(hf + 1) * HALF_W] = jnp.dot(ycat, q_ref[...],
                                                         preferred_element_type=F32).astype(y_ref.dtype)


def _s5_call(up, pmat, m, sin, gmat, a16, qmat, n_ctx_chunks):
    groups, rows = m.shape[0], up.shape[0]
    gb = GROUPS_PER_BLOCK
    n_chunks = rows // SUBLANES
    out_rows = rows - n_ctx_chunks * SUBLANES
    mat = lambda: pl.BlockSpec((gb, S5_CW, S5_CW), lambda g: (g, 0, 0))
    return pl.pallas_call(
        functools.partial(_s5_kernel, n_ctx_chunks=n_ctx_chunks, n_chunks=n_chunks),
        out_shape=jax.ShapeDtypeStruct((out_rows, groups // gb * BLOCK_W), BF16),
        grid=(groups // gb,),
        in_specs=[pl.BlockSpec((rows, BLOCK_W), lambda g: (0, g)),
                  _const_spec(pmat.shape, 1),
                  mat(), mat(), mat(),
                  pl.BlockSpec((gb, 2, LANES), lambda g: (g, 0, 0)),
                  _const_spec(qmat.shape, 1)],
        out_specs=pl.BlockSpec((out_rows, BLOCK_W), lambda g: (0, g)),
        scratch_shapes=[pltpu.VMEM((gb, rows, S5_CW), BF16),
                        pltpu.VMEM((gb, rows, S5_CW), F32), pltpu.VMEM((gb, rows, S5_CW), F32)],
        compiler_params=pltpu.CompilerParams(dimension_semantics=("arbitrary",),
                                             vmem_limit_bytes=VMEM_LIMIT),
        name="s5",
    )(up, pmat, m, sin, gmat, a16, qmat)


def _s5_ops_kernel(*refs):
    for j in range(refs[0].shape[0]):
        _s5_ops_group(*(r.at[j] for r in refs))


def _s5_ops_group(lam_ref, ldt_ref, bt_ref, ct_ref, d_ref, m_ref, sin_ref, g_ref, a_ref):
    T, H = S5_CHUNK, S5_GROUP
    lr = jnp.minimum(lam_ref[0:1, :], -1e-4)
    li = lam_ref[1:2, :]
    dt = jnp.exp(ldt_ref[...])
    taus = lax.broadcasted_iota(jnp.int32, (3 * SUBLANES, LANES), 0).astype(F32)
    mag = jnp.exp(lr * dt * taus)
    pw_r = mag * jnp.cos(li * dt * taus)
    pw_i = mag * jnp.sin(li * dt * taus)
    nr, ni = pw_r[1:2] - 1.0, pw_i[1:2]
    den = lr * lr + li * li
    cf_r = (nr * lr + ni * li) / den
    cf_i = (ni * lr - nr * li) / den
    bb_r = cf_r * bt_ref[0] - cf_i * bt_ref[1]
    bb_i = cf_r * bt_ref[1] + cf_i * bt_ref[0]
    is_fwd = lax.broadcasted_iota(jnp.int32, (H, LANES), 1) < S5_STATE

    def powers(tau_f, tau_b):
        pick = lambda pw, s: jnp.where(is_fwd, jnp.broadcast_to(pw[tau_f(s):tau_f(s) + 1], (H, LANES)),
                                       jnp.broadcast_to(pw[tau_b(s):tau_b(s) + 1], (H, LANES)))
        return (jnp.concatenate([pick(pw_r, s) for s in range(T)], axis=0),
                jnp.concatenate([pick(pw_i, s) for s in range(T)], axis=0))

    tile = lambda a: jnp.concatenate([a] * T, axis=0)
    bbr, bbi, cr, ci = tile(bb_r), tile(bb_i), tile(ct_ref[0]), tile(ct_ref[1])

    er, ei = powers(lambda s: T - 1 - s, lambda s: s)
    sin_ref[:, 0:LANES] = (er * bbr - ei * bbi).astype(sin_ref.dtype)
    sin_ref[:, LANES:] = (er * bbi + ei * bbr).astype(sin_ref.dtype)

    er, ei = powers(lambda t: t + 1, lambda t: T - t)
    gt = jnp.concatenate([er * cr - ei * ci, -(er * ci + ei * cr)], axis=1)
    g_ref[...] = gt.T.astype(g_ref.dtype)

    er, ei = powers(lambda a: a, lambda a: T - 1 - a)
    cp = jnp.concatenate([er * cr - ei * ci, er * ci + ei * cr], axis=1)
    zero = jnp.zeros_like(bb_r)
    lhs = jnp.concatenate([jnp.concatenate([jnp.where(is_fwd, bb_r, zero), jnp.where(is_fwd, -bb_i, zero)], axis=1),
                           jnp.concatenate([jnp.where(is_fwd, zero, bb_r), jnp.where(is_fwd, zero, -bb_i)], axis=1)],
                          axis=0)
    kr = lax.dot_general(lhs, cp, (((1,), (1,)), ((), ())), preferred_element_type=F32,
                         precision=lax.Precision.HIGHEST)
    pad = jnp.zeros((H, S5_CW), F32)
    wide_f = jnp.concatenate([pad, kr[0:H]], axis=1)
    wide_b = jnp.concatenate([kr[H:], pad], axis=1)
    lane = lax.broadcasted_iota(jnp.int32, (H, S5_CW), 1)
    row = lax.broadcasted_iota(jnp.int32, (H, S5_CW), 0)
    skip = jnp.broadcast_to(d_ref[...], (H, S5_CW))
    for s in range(T):
        blk_f = pltpu.roll(wide_f, H * s, 1)[:, S5_CW:] if s else wide_f[:, S5_CW:]
        shift_b = (2 * S5_CW - H * (T - 1 - s)) % (2 * S5_CW)
        blk_b = (pltpu.roll(wide_b, shift_b, 1) if shift_b else wide_b)[:, :S5_CW]
        diag = jnp.where(lane == H * s + row, skip, 0.0)
        m_ref[H * s:H * (s + 1), :] = (blk_f + blk_b + diag).astype(m_ref.dtype)
    a_ref[0:1, :] = pw_r[T:T + 1]
    a_ref[1:2, :] = pw_i[T:T + 1]


def _s5_ops_call(lam, ldt, bt, ct, dt_tiled):
    groups = lam.shape[0]
    gb = 8
    mat = lambda: pl.BlockSpec((gb, S5_CW, S5_CW), lambda g: (g, 0, 0))
    vec = lambda a: pl.BlockSpec((gb,) + a.shape[1:], lambda g: (g,) + (0,) * (a.ndim - 1))
    mshape = jax.ShapeDtypeStruct((groups, S5_CW, S5_CW), BF16)
    return pl.pallas_call(
        _s5_ops_kernel,
        out_shape=(mshape, mshape, mshape, jax.ShapeDtypeStruct((groups, 2, LANES), F32)),
        grid=(groups // gb,),
        in_specs=[vec(lam), vec(ldt), vec(bt), vec(ct), vec(dt_tiled)],
        out_specs=(mat(), mat(), mat(), pl.BlockSpec((gb, 2, LANES), lambda g: (g, 0, 0))),
        compiler_params=pltpu.CompilerParams(dimension_semantics=("arbitrary",)),
        name="s5_ops",
    )(lam, ldt, bt, ct, dt_tiled)


def _s5_operators(lam_re, lam_im, log_dt, b_re, b_im, c_re, c_im, d_skip):
    G, P, H = S5_GROUPS, S5_STATE, S5_GROUP
    lam = jnp.stack([lam_re, lam_im]).astype(F32).transpose(2, 0, 1, 3).reshape(G, 2, 2 * P)
    ldt = jnp.repeat(log_dt.astype(F32).T, P, axis=1).reshape(G, 1, 2 * P)
    bt = jnp.stack([b_re, b_im]).astype(F32).transpose(2, 0, 4, 1, 3).reshape(G, 2, H, 2 * P)
    ct = jnp.stack([c_re, c_im]).astype(F32).transpose(2, 0, 3, 1, 4).reshape(G, 2, H, 2 * P)
    dt_tiled = jnp.tile(d_skip.astype(F32).reshape(G, 1, H), (1, 1, S5_CHUNK))
    return _s5_ops_call(lam, ldt, bt, ct, dt_tiled)


def _final_kernel(x_ref, mod_ref, ng_ref, wg_ref, ya_ref, yp_ref, permt_ref, wglu_ref, bglu_ref, wa_ref, wb_ref,
                  wo_ref, fg_ref, o_ref):
    bsz, tt, _ = x_ref.shape
    o1, o2, o3 = ATTN_WIDTH, ATTN_WIDTH + S5_WIDTH, ATTN_WIDTH + S5_WIDTH + D_MODEL
    nb = FINAL_BATCHES
    rows = nb * tt
    groups = [slice(i * nb, (i + 1) * nb) for i in range(bsz // nb)]
    prow = PERM_TOK // S5_CHUNK * bsz
    parts = []
    for part in range(tt // PERM_TOK):
        slabs = [jnp.concatenate([yp_ref[part * prow:(part + 1) * prow,
                                         blk * BLOCK_W + t * LANES: blk * BLOCK_W + (t + 1) * LANES]
                                  for blk in range(N_BLOCKS)], axis=1) for t in range(S5_CHUNK)]
        yb = jnp.dot(permt_ref[...], jnp.concatenate(slabs, axis=0), preferred_element_type=F32)
        parts.append(yb.reshape(bsz, PERM_TOK, S5_WIDTH))
    y3 = parts[0] if len(parts) == 1 else jnp.concatenate(parts, axis=1)
    x3s, gates = [], []
    for hs in groups:
        x3 = x_ref[hs]
        x3s.append(x3)
        xn = _modulated_norm(x3, mod_ref.at[hs], ng_ref).reshape(rows, D_MODEL).astype(BF16)
        gates.append(jnp.dot(xn, wg_ref[...], preferred_element_type=F32))
    mixes = []
    for hs, g in zip(groups, gates):
        y = y3[hs].reshape(rows, S5_WIDTH)
        z = y * (0.5 * (1.0 + jnp.tanh(0.7978845608028654 * (y + 0.044715 * (y * y * y)))))
        zz = z * jax.nn.sigmoid(jnp.dot(z.astype(BF16), wglu_ref[...], preferred_element_type=F32) + bglu_ref[...])
        ya = ya_ref[hs].reshape(rows, ATTN_WIDTH).astype(F32)
        ta = (ya * _silu(g[:, 0:o1])).astype(BF16)
        tb = (zz * _silu(g[:, o1:o2])).astype(BF16)
        pa = jnp.dot(ta, wa_ref[...], preferred_element_type=F32)
        pb = jnp.dot(tb, wb_ref[...], preferred_element_type=F32)
        mixes.append((jax.nn.sigmoid(g[:, o2:o3]) * pa + jax.nn.sigmoid(g[:, o3:]) * pb).astype(BF16))
    outs = [jnp.dot(m, wo_ref[...], preferred_element_type=F32).reshape(nb, tt, D_MODEL) for m in mixes]
    for hs, x3, o in zip(groups, x3s, outs):
        h = x3 + mod_ref[hs, :, 2 * D_MODEL:] * o
        ms = jnp.mean(h * h, axis=-1, keepdims=True)
        o_ref[hs] = h * lax.rsqrt(ms + EPS) * fg_ref[...]


def _final_call(x, mod3, ng, wg, ya, yp, permt, wglu, bglu, wa, wb, wo, fg):
    bsz, n, _ = x.shape
    tt = TOK_TILE
    up_rows = tt // S5_CHUNK * bsz
    tok = lambda width: pl.BlockSpec((bsz, tt, width), lambda i: (0, i, 0))
    c = lambda shape: _const_spec(shape, 1)
    return pl.pallas_call(
        _final_kernel,
        out_shape=jax.ShapeDtypeStruct((bsz, n, D_MODEL), F32),
        grid=(n // tt,),
        in_specs=[tok(D_MODEL),
                  pl.BlockSpec((bsz, 1, 3 * D_MODEL), lambda i: (0, 0, 0)),
                  c((1, D_MODEL)), c(wg.shape),
                  tok(ATTN_WIDTH),
                  pl.BlockSpec((up_rows, N_BLOCKS * BLOCK_W), lambda i: (i, 0)),
                  c(permt.shape), c(wglu.shape), c((1, S5_WIDTH)), c(wa.shape), c(wb.shape), c(wo.shape),
                  c((1, D_MODEL))],
        out_specs=tok(D_MODEL),
        compiler_params=pltpu.CompilerParams(dimension_semantics=("arbitrary",), vmem_limit_bytes=VMEM_LIMIT),
        name="final",
    )(x, mod3, ng, wg, ya, yp, permt, wglu, bglu, wa, wb, wo, fg)


def _rope_tables(n):
    rows = n // GRID_W
    row_ids = np.repeat(np.arange(rows, dtype=np.float64), GRID_W)
    col_ids = np.tile(np.arange(GRID_W, dtype=np.float64), rows)
    freqs = ROPE_THETA ** (-np.arange(ROPE_FREQS, dtype=np.float64) / ROPE_FREQS)
    ang_r, ang_c = row_ids[:, None] * freqs, col_ids[:, None] * freqs
    cos = np.concatenate([np.cos(ang_r)] * 2 + [np.cos(ang_c)] * 2, axis=1)
    sin = np.concatenate([-np.sin(ang_r), np.sin(ang_r), -np.sin(ang_c), np.sin(ang_c)], axis=1)
    reps = LANES // HEAD_DIM
    return (jnp.asarray(np.tile(cos, (1, reps)).astype(np.float32)),
            jnp.asarray(np.tile(sin, (1, reps)).astype(np.float32)))


def _one_hot(match):
    return jnp.asarray(np.ascontiguousarray(match).astype(BF16))


def _block_ones(width):
    idx = np.arange(width) // HEAD_DIM
    return _one_hot(idx[:, None] == idx[None, :])


def _row_perm(bsz, tt, transpose=False):
    chunks = tt // S5_CHUNK
    r = np.arange(bsz * tt)
    t, pc, b = r // (chunks * bsz), (r // bsz) % chunks, r % bsz
    src = b * tt + pc * S5_CHUNK + t
    match = src[:, None] == np.arange(bsz * tt)[None, :]
    return _one_hot(match.T if transpose else match)


def _lane_perm(transpose=False):
    r = np.arange(HALF_W)
    t, j, h = r // LANES, (r % LANES) // S5_GROUP, r % S5_GROUP
    dst = j * LANES + t * S5_GROUP + h
    match = dst[:, None] == np.arange(HALF_W)[None, :]
    return _one_hot(match.T if transpose else match)


def _kv_expand():
    col = np.arange(N_KV_HEADS * REP_W)
    src = (col // REP_W) * HEAD_DIM + col % HEAD_DIM
    return _one_hot(np.arange(KV_WIDTH)[:, None] == src[None, :])


def kernel(x, c, ctx, c_ctx, norm_g, w_ada, b_ada, w_in, q_norm_g, k_norm_g, s5_lam_re, s5_lam_im, s5_log_dt,
           s5_b_re, s5_b_im, s5_c_re, s5_c_im, s5_d, w_glu, b_glu, w_branch_attn, w_branch_s5, w_out,
           final_norm_g):
    assert w_in.shape[0] == 1, "single-layer block"
    bsz, n, _ = x.shape
    n_ctx = ctx.shape[1]
    assert n % TOK_TILE == 0 and n_ctx % TOK_TILE == 0 and bsz == SUBLANES

    ada_rows = 2 * SUBLANES
    cc = jnp.concatenate([c, c_ctx[None], jnp.zeros((ada_rows - bsz - 1, D_MODEL), F32)], axis=0)
    mod = _ada_call(cc, w_ada[0], b_ada[0][None])
    mod3 = mod[:bsz].reshape(bsz, 1, 3 * D_MODEL)
    mod_ctx3 = mod[bsz:bsz + 1].reshape(1, 1, 3 * D_MODEL)

    offs = [0]
    for s in IN_SIZES:
        offs.append(offs[-1] + s)
    w_bf = w_in[0].astype(BF16)
    w_pre = jnp.concatenate([w_bf[:, offs[0]:offs[3]], w_bf[:, offs[4]:offs[5]]], axis=1)
    w_gates = jnp.concatenate([w_bf[:, offs[3]:offs[4]], w_bf[:, offs[5]:]], axis=1)

    ng = norm_g[0][None]
    qg = jnp.tile(q_norm_g[0], N_HEADS)[None]
    kg = jnp.tile(k_norm_g[0], N_KV_HEADS)[None]
    onesq, onesk = _block_ones(ATTN_WIDTH), _block_ones(KV_WIDTH)
    cos, sin = _rope_tables(n)
    row_perm = _row_perm(bsz, PERM_TOK)
    lane_perm = _lane_perm()

    q, k_all, v_all, up = _pre_call(x, ctx, mod3, mod_ctx3, ng, w_pre, onesq, onesk, qg, kg, cos, sin, row_perm)

    y_attn = _attn_call(q, k_all, v_all, _kv_expand(), tq=1024)

    m, s_in, gmat, a16 = _s5_operators(s5_lam_re[0], s5_lam_im[0], s5_log_dt[0], s5_b_re[0], s5_b_im[0],
                                       s5_c_re[0], s5_c_im[0], s5_d[0])
    yp = _s5_call(up, lane_perm, m, s_in, gmat, a16, _lane_perm(transpose=True), n_ctx // S5_CHUNK)

    return _final_call(x, mod3, ng, w_gates, y_attn, yp, _row_perm(bsz, PERM_TOK, transpose=True),
                       w_glu[0].astype(BF16), b_glu[0][None],
                       w_branch_attn[0].astype(BF16), w_branch_s5[0].astype(BF16), w_out[0].astype(BF16),
                       final_norm_g[None])
```
